```python
import math
import jax, jax.numpy as jnp
from jax import lax
import numpy as np

D_MODEL = 4096
BATCH = 4
SEQ = 2048
DEPTH = 1
DEC_BATCH = 128
DEC_SEQ = 8
PAST_LEN = 16384
PAGE_SIZE = 128

N_META = 16
RET_WIDTH = D_MODEL // 2
S5_WIDTH = D_MODEL - RET_WIDTH
RET_HEADS = 8
RET_HEAD_DIM = RET_WIDTH // RET_HEADS
RET_CHUNK = 128
ROPE_BASE = 10000.0
S5_GROUP = 16
S5_GROUPS = S5_WIDTH // S5_GROUP
S5_STATE = 64
D_FF = -(-8 * D_MODEL // (3 * 256)) * 256
IN_COLS = 4 * RET_WIDTH + S5_WIDTH
MIX_WIDTH = RET_WIDTH + S5_WIDTH
EPS = 1e-6
GN_EPS = 1e-5

kernel_name = "hymba_retention_s5_step"


def rms_norm(x, g):
    x32 = x.astype(jnp.float32)
    y = x32 * lax.rsqrt(jnp.mean(x32 * x32, axis=-1, keepdims=True) + EPS)
    return (y * g.astype(jnp.float32)).astype(x.dtype)


def retention_log_gamma():
    return jnp.log(1.0 - 2.0 ** (-5.0 - jnp.arange(RET_HEADS, dtype=jnp.float32)))


def rotary(x, pos):
    half = RET_HEAD_DIM // 2
    inv = ROPE_BASE ** (-jnp.arange(half, dtype=jnp.float32) / half)
    ang = pos.astype(jnp.float32)[:, None] * inv[None, :]
    cos = jnp.cos(ang)[None, :, None, :]
    sin = jnp.sin(ang)[None, :, None, :]
    x32 = x.astype(jnp.float32)
    x1, x2 = x32[..., :half], x32[..., half:]
    return jnp.concatenate([x1 * cos - x2 * sin, x1 * sin + x2 * cos], axis=-1)


def retention_chunk(S, q, k, v, log_gamma):
    L = q.shape[2]
    idx = jnp.arange(L, dtype=jnp.float32)
    lg = log_gamma[:, None]
    diff = idx[:, None] - idx[None, :]
    mask = jnp.where(diff[None] >= 0,
                     jnp.exp(jnp.maximum(diff, 0.0)[None] * lg[:, :, None]), 0.0)
    scores = jnp.einsum('bhid,bhjd->bhij', q, k) * mask[None]
    intra = jnp.einsum('bhij,bhjv->bhiv', scores, v)
    q_dec = q * jnp.exp(lg * (idx + 1.0)[None, :])[None, :, :, None]
    inter = jnp.einsum('bhid,bhdv->bhiv', q_dec, S)
    k_dec = k * jnp.exp(lg * (L - 1.0 - idx)[None, :])[None, :, :, None]
    S_new = jnp.exp(log_gamma * L)[None, :, None, None] * S + jnp.einsum('bhjd,bhjv->bhdv', k_dec, v)
    return intra + inter, S_new


def retention_seq(S, q, k, v, lead, log_gamma):
    o0, S = retention_chunk(S, q[:, :, :lead], k[:, :, :lead], v[:, :, :lead], log_gamma)
    rest = q.shape[2] - lead
    if rest == 0:
        return o0, S
    n = rest // RET_CHUNK
    b, h = q.shape[0], q.shape[1]

    def to_chunks(t):
        return jnp.moveaxis(t[:, :, lead:].reshape(b, h, n, RET_CHUNK, t.shape[-1]), 2, 0)

    def step(S_c, xs):
        o, S_n = retention_chunk(S_c, xs[0], xs[1], xs[2], log_gamma)
        return S_n, o

    S, o_rest = lax.scan(step, S, (to_chunks(q), to_chunks(k), to_chunks(v)))
    o_rest = jnp.moveaxis(o_rest, 0, 2).reshape(b, h, rest, RET_HEAD_DIM)
    return jnp.concatenate([o0, o_rest], axis=2), S


def head_group_norm(o, g):
    mu = jnp.mean(o, axis=-1, keepdims=True)
    var = jnp.mean(jnp.square(o - mu), axis=-1, keepdims=True)
    y = (o - mu) * lax.rsqrt(var + GN_EPS)
    b, h, l, d = o.shape
    y = jnp.transpose(y, (0, 2, 1, 3)).reshape(b, l, h * d)
    return y * g.astype(jnp.float32)


def s5_discretize(lam_re, lam_im, log_dt, b_re, b_im):
    lam_re = lam_re.astype(jnp.float32)
    lam_im = lam_im.astype(jnp.float32)
    dt = jnp.exp(log_dt.astype(jnp.float32))[:, None]
    mag = jnp.exp(lam_re * dt)
    ar = mag * jnp.cos(lam_im * dt)
    ai = mag * jnp.sin(lam_im * dt)
    nr, ni = ar - 1.0, ai
    den = lam_re * lam_re + lam_im * lam_im
    fr = (nr * lam_re + ni * lam_im) / den
    fi = (ni * lam_re - nr * lam_im) / den
    b_re = b_re.astype(jnp.float32)
    b_im = b_im.astype(jnp.float32)
    bbr = fr[..., None] * b_re - fi[..., None] * b_im
    bbi = fr[..., None] * b_im + fi[..., None] * b_re
    return ar, ai, bbr, bbi


def s5_scan(u, x0_re, x0_im, ar, ai, bbr, bbi):
    bu_re = jnp.einsum('blgc,gpc->blgp', u, bbr)
    bu_im = jnp.einsum('blgc,gpc->blgp', u, bbi)
    bu_re = bu_re.at[:, 0].add(ar * x0_re - ai * x0_im)
    bu_im = bu_im.at[:, 0].add(ar * x0_im + ai * x0_re)
    a_re = jnp.broadcast_to(ar, bu_re.shape)
    a_im = jnp.broadcast_to(ai, bu_im.shape)

    def combine(e1, e2):
        a1r, a1i, b1r, b1i = e1
        a2r, a2i, b2r, b2i = e2
        return (a2r * a1r - a2i * a1i,
                a2r * a1i + a2i * a1r,
                a2r * b1r - a2i * b1i + b2r,
                a2r * b1i + a2i * b1r + b2i)

    _, _, xr, xi = lax.associative_scan(combine, (a_re, a_im, bu_re, bu_im), axis=1)
    return xr, xi


def block(x, pos, S0, x0r, x0i, lead, norm1_g, w_in, ret_gn_g, s5_lam_re, s5_lam_im,
          s5_log_dt, s5_b_re, s5_b_im, s5_c_re, s5_c_im, s5_d, w_glu, b_glu, s5_norm_g,
          w_out, norm2_g, w_gate, w_up, w_down):
    B, L, _ = x.shape
    h = rms_norm(x, norm1_g)
    proj = h @ w_in
    q = proj[..., :RET_WIDTH].reshape(B, L, RET_HEADS, RET_HEAD_DIM)
    k = proj[..., RET_WIDTH:2 * RET_WIDTH].reshape(B, L, RET_HEADS, RET_HEAD_DIM)
    v = proj[..., 2 * RET_WIDTH:3 * RET_WIDTH].reshape(B, L, RET_HEADS, RET_HEAD_DIM)
    g = proj[..., 3 * RET_WIDTH:4 * RET_WIDTH]
    u = proj[..., 4 * RET_WIDTH:]

    q = jnp.transpose(rotary(q, pos), (0, 2, 1, 3))
    k = jnp.transpose(rotary(k, pos) * (RET_HEAD_DIM ** -0.5), (0, 2, 1, 3))
    v = jnp.transpose(v.astype(jnp.float32), (0, 2, 1, 3))
    o, S_new = retention_seq(S0.astype(jnp.float32), q, k, v, lead, retention_log_gamma())
    ret_out = head_group_norm(o, ret_gn_g) * jax.nn.silu(g.astype(jnp.float32))

    u32 = u.astype(jnp.float32).reshape(B, L, S5_GROUPS, S5_GROUP)
    ar, ai, bbr, bbi = s5_discretize(s5_lam_re, s5_lam_im, s5_log_dt, s5_b_re, s5_b_im)
    xr, xi = s5_scan(u32, x0r.astype(jnp.float32), x0i.astype(jnp.float32), ar, ai, bbr, bbi)
    y = (jnp.einsum('blgp,gcp->blgc', xr, s5_c_re.astype(jnp.float32))
         - jnp.einsum('blgp,gcp->blgc', xi, s5_c_im.astype(jnp.float32))
         + s5_d.astype(jnp.float32).reshape(S5_GROUPS, S5_GROUP) * u32).reshape(B, L, S5_WIDTH)
    z = jax.nn.gelu(y)
    s5_out = z * jax.nn.sigmoid(z @ w_glu.astype(jnp.float32) + b_glu.astype(jnp.float32))
    s5_out = rms_norm(s5_out, s5_norm_g)

    mix = jnp.concatenate([ret_out, s5_out], axis=-1).astype(x.dtype) @ w_out
    x = x + mix
    h2 = rms_norm(x, norm2_g)
    x = x + (jax.nn.silu(h2 @ w_gate) * (h2 @ w_up)) @ w_down
    return x, S_new, xr[:, -1], xi[:, -1]


def setup_inputs(seed: int = 0) -> dict:
    key = jax.random.key(seed)
    ks = jax.random.split(key, 32)
    f32 = jnp.float32
    nrm = lambda k, shape, s: jax.random.normal(k, shape, f32) * s
    lam_re = -0.5 + nrm(ks[0], (DEPTH, S5_GROUPS, S5_STATE), 0.01)
    lam_im = (math.pi * jnp.arange(S5_STATE, dtype=f32))[None, None, :] + nrm(ks[1], (DEPTH, S5_GROUPS, S5_STATE), 0.01)
    log_dt = jax.random.uniform(ks[2], (DEPTH, S5_GROUPS), f32, math.log(0.001), math.log(0.1))
    return {
        "x_prompt": nrm(ks[3], (BATCH, SEQ, D_MODEL), 1.0),
        "x_sample": nrm(ks[4], (DEC_BATCH, DEC_SEQ, D_MODEL), 1.0),
        "state_ret": nrm(ks[5], (DEPTH, DEC_BATCH, RET_HEADS, RET_HEAD_DIM, RET_HEAD_DIM), 0.5),
        "state_s5_re": nrm(ks[6], (DEPTH, DEC_BATCH, S5_GROUPS, S5_STATE), 1.0),
        "state_s5_im": nrm(ks[7], (DEPTH, DEC_BATCH, S5_GROUPS, S5_STATE), 1.0),
        "meta_tokens": nrm(ks[8], (N_META, D_MODEL), 1.0),
        "norm1_g": 1.0 + nrm(ks[9], (DEPTH, D_MODEL), 0.02),
        "w_in": nrm(ks[10], (DEPTH, D_MODEL, IN_COLS), D_MODEL ** -0.5),
        "ret_gn_g": 1.0 + nrm(ks[11], (DEPTH, RET_WIDTH), 0.02),
        "s5_lam_re": lam_re,
        "s5_lam_im": lam_im,
        "s5_log_dt": log_dt,
        "s5_b_re": nrm(ks[12], (DEPTH, S5_GROUPS, S5_STATE, S5_GROUP), (2 * S5_GROUP) ** -0.5),
        "s5_b_im": nrm(ks[13], (DEPTH, S5_GROUPS, S5_STATE, S5_GROUP), (2 * S5_GROUP) ** -0.5),
        "s5_c_re": nrm(ks[14], (DEPTH, S5_GROUPS, S5_GROUP, S5_STATE), (2 * S5_STATE) ** -0.5),
        "s5_c_im": nrm(ks[15], (DEPTH, S5_GROUPS, S5_GROUP, S5_STATE), (2 * S5_STATE) ** -0.5),
        "s5_d": nrm(ks[16], (DEPTH, S5_WIDTH), 1.0),
        "w_glu": nrm(ks[17], (DEPTH, S5_WIDTH, S5_WIDTH), S5_WIDTH ** -0.5),
        "b_glu": nrm(ks[18], (DEPTH, S5_WIDTH), 0.01),
        "s5_norm_g": 1.0 + nrm(ks[19], (DEPTH, S5_WIDTH), 0.02),
        "w_out": nrm(ks[20], (DEPTH, MIX_WIDTH, D_MODEL), MIX_WIDTH ** -0.5),
        "norm2_g": 1.0 + nrm(ks[21], (DEPTH, D_MODEL), 0.02),
        "w_gate": nrm(ks[22], (DEPTH, D_MODEL, D_FF), D_MODEL ** -0.5),
        "w_up": nrm(ks[23], (DEPTH, D_MODEL, D_FF), D_MODEL ** -0.5),
        "w_down": nrm(ks[24], (DEPTH, D_FF, D_MODEL), D_FF ** -0.5),
        "final_norm_g": 1.0 + nrm(ks[25], (D_MODEL,), 0.02),
    }


def reference(x_prompt, x_sample, state_ret, state_s5_re, state_s5_im, meta_tokens,
              norm1_g, w_in, ret_gn_g, s5_lam_re, s5_lam_im, s5_log_dt, s5_b_re, s5_b_im,
              s5_c_re, s5_c_im, s5_d, w_glu, b_glu, s5_norm_g, w_out, norm2_g,
              w_gate, w_up, w_down, final_norm_g):
    T = N_META + x_prompt.shape[1]
    xp = jnp.concatenate(
        [jnp.broadcast_to(meta_tokens.astype(x_prompt.dtype)[None], (x_prompt.shape[0], N_META, D_MODEL)),
         x_prompt], axis=1)
    xs = x_sample
    pos_p = jnp.arange(T, dtype=jnp.int32)
    pos_s = PAST_LEN + jnp.arange(xs.shape[1], dtype=jnp.int32)
    bp = x_prompt.shape[0]
    zero_ret = jnp.zeros((bp, RET_HEADS, RET_HEAD_DIM, RET_HEAD_DIM), jnp.float32)
    zero_s5 = jnp.zeros((bp, S5_GROUPS, S5_STATE), jnp.float32)

    ret_p, s5r_p, s5i_p, ret_s, s5r_s, s5i_s = [], [], [], [], [], []
    for l in range(DEPTH):
        w = (norm1_g[l], w_in[l], ret_gn_g[l], s5_lam_re[l], s5_lam_im[l], s5_log_dt[l],
             s5_b_re[l], s5_b_im[l], s5_c_re[l], s5_c_im[l], s5_d[l], w_glu[l], b_glu[l],
             s5_norm_g[l], w_out[l], norm2_g[l], w_gate[l], w_up[l], w_down[l])
        xp, Sp, rp, ip = block(xp, pos_p, zero_ret, zero_s5, zero_s5, N_META, *w)
        xs, Ss, rs, is_ = block(xs, pos_s, state_ret[l], state_s5_re[l], state_s5_im[l], xs.shape[1], *w)
        ret_p.append(Sp); s5r_p.append(rp); s5i_p.append(ip)
        ret_s.append(Ss); s5r_s.append(rs); s5i_s.append(is_)

    y_prompt = rms_norm(xp, final_norm_g)[:, N_META:]
    y_sample = rms_norm(xs, final_norm_g)
    return (y_prompt, y_sample,
            jnp.stack(ret_p), jnp.stack(s5r_p), jnp.stack(s5i_p),
            jnp.stack(ret_s), jnp.stack(s5r_s), jnp.stack(s5i_s))
```

```python
import functools
import math

import numpy as np
import jax
import jax.numpy as jnp
from jax import lax
from jax.experimental import pallas as pl
from jax.experimental.pallas import tpu as pltpu

D_MODEL = 4096
N_META = 16
PAST_LEN = 16384
RET_WIDTH = D_MODEL // 2
S5_WIDTH = D_MODEL - RET_WIDTH
RET_HEADS = 8
RET_HEAD_DIM = RET_WIDTH // RET_HEADS
RET_CHUNK = 128
ROPE_BASE = 10000.0
S5_GROUP = 16
S5_GROUPS = S5_WIDTH // S5_GROUP
S5_STATE = 64
S5_NSTATE = S5_GROUPS * S5_STATE
IN_COLS = 4 * RET_WIDTH + S5_WIDTH
EPS = 1e-6
GN_EPS = 1e-5

LANES = 128
S5_GB = 16
S5_CH = S5_GB * S5_GROUP
S5_ST = S5_GB * S5_STATE
S5_SLABS = 2 * S5_ST // LANES
VMEM_LIMIT = 56 * 1024 * 1024

BF16 = jnp.bfloat16
F32 = jnp.float32


def _params(*sem):
    return pltpu.CompilerParams(dimension_semantics=sem, vmem_limit_bytes=VMEM_LIMIT)


def _dot(a, b):
    return jnp.dot(a, b, preferred_element_type=F32)


def _rmsnorm_kernel(*refs, n_src, n_first):
    x_refs, g_ref, o_ref = refs[:n_src], refs[n_src], refs[n_src + 1]
    x = x_refs[0][...]
    if n_src == 2:
        x = jnp.where(pl.program_id(0) < n_first, x, x_refs[1][...])
    r = lax.rsqrt(jnp.mean(x * x, axis=-1, keepdims=True) + EPS)
    o_ref[...] = (x * r * g_ref[...]).astype(o_ref.dtype)


def _rmsnorm(srcs, g, tm, out_dtype):
    d = srcs[0].shape[1]
    nblk = [s.shape[0] // tm for s in srcs]
    n_first = nblk[0]
    specs = [pl.BlockSpec((tm, d), lambda i: (jnp.minimum(i, n_first - 1), 0))]
    if len(srcs) == 2:
        specs.append(pl.BlockSpec((tm, d), lambda i: (jnp.maximum(i - n_first, 0), 0)))
    rows = sum(s.shape[0] for s in srcs)
    return pl.pallas_call(
        functools.partial(_rmsnorm_kernel, n_src=len(srcs), n_first=n_first),
        grid=(sum(nblk),),
        in_specs=specs + [pl.BlockSpec((1, d), lambda i: (0, 0))],
        out_specs=pl.BlockSpec((tm, d), lambda i: (i, 0)),
        out_shape=jax.ShapeDtypeStruct((rows, d), out_dtype),
        compiler_params=_params("parallel"),
        name="rmsnorm",
    )(*srcs, g.reshape(1, d))


def _final_norm_kernel(x_ref, g_ref, yp_ref, ys_ref, *, n_first):
    x = x_ref[...]
    r = lax.rsqrt(jnp.mean(x * x, axis=-1, keepdims=True) + EPS)
    y = x * r * g_ref[...]
    i = pl.program_id(0)

    @pl.when(i < n_first)
    def _():
        yp_ref[...] = y

    @pl.when(i >= n_first)
    def _():
        ys_ref[...] = y


def _final_norm(x, g, rows_p, tm):
    rows, d = x.shape
    n_first = rows_p // tm
    return pl.pallas_call(
        functools.partial(_final_norm_kernel, n_first=n_first),
        grid=(rows // tm,),
        in_specs=[pl.BlockSpec((tm, d), lambda i: (i, 0)),
                  pl.BlockSpec((1, d), lambda i: (0, 0))],
        out_specs=[pl.BlockSpec((tm, d), lambda i: (jnp.minimum(i, n_first - 1), 0)),
                   pl.BlockSpec((tm, d), lambda i: (jnp.maximum(i - n_first, 0), 0))],
        out_shape=[jax.ShapeDtypeStruct((rows_p, d), F32),
                   jax.ShapeDtypeStruct((rows - rows_p, d), F32)],
        compiler_params=_params("arbitrary"),
        name="final_norm",
    )(x, g.reshape(1, d))


def _inproj_kernel(h_ref, w_ref, o_ref):
    o_ref[...] = _dot(h_ref[...], w_ref[...].astype(BF16))


def _inproj(h, w, tm, tn):
    m, k = h.shape
    n = w.shape[1]
    return pl.pallas_call(
        _inproj_kernel,
        grid=(m // tm, n // tn),
        in_specs=[pl.BlockSpec((tm, k), lambda i, j: (i, 0)),
                  pl.BlockSpec((k, tn), lambda i, j: (0, j))],
        out_specs=pl.BlockSpec((tm, tn), lambda i, j: (i, j)),
        out_shape=jax.ShapeDtypeStruct((m, n), F32),
        compiler_params=_params("parallel", "arbitrary"),
        name="inproj",
    )(h, w)


def _outproj_kernel(ret_ref, s5_ref, wt_ref, wb_ref, xp_ref, xs_ref, o_ref, *, n_first):
    x = jnp.where(pl.program_id(0) < n_first, xp_ref[...], xs_ref[...])
    acc = _dot(ret_ref[...], wt_ref[...].astype(BF16))
    acc += _dot(s5_ref[...], wb_ref[...].astype(BF16))
    o_ref[...] = x + acc


def _outproj(ret, s5, w_out, xp, xs, tm, tn):
    m, kh = ret.shape
    n = w_out.shape[1]
    n_first = xp.shape[0] // tm
    nj = n // tn
    return pl.pallas_call(
        functools.partial(_outproj_kernel, n_first=n_first),
        grid=(m // tm, nj),
        in_specs=[pl.BlockSpec((tm, kh), lambda i, j: (i, 0)),
                  pl.BlockSpec((tm, kh), lambda i, j: (i, 0)),
                  pl.BlockSpec((kh, tn), lambda i, j: (0, j)),
                  pl.BlockSpec((kh, tn), lambda i, j: (1, j)),
                  pl.BlockSpec((tm, tn), lambda i, j: (jnp.minimum(i, n_first - 1),
                                                       jnp.where(i < n_first, j, nj - 1))),
                  pl.BlockSpec((tm, tn), lambda i, j: (jnp.maximum(i - n_first, 0),
                                                       jnp.where(i < n_first, 0, j)))],
        out_specs=pl.BlockSpec((tm, tn), lambda i, j: (i, j)),
        out_shape=jax.ShapeDtypeStruct((m, n), F32),
        compiler_params=_params("parallel", "arbitrary"),
        name="outproj",
    )(ret, s5, w_out, w_out, xp, xs)


def _ffn_up_kernel(h_ref, wg_ref, wu_ref, o_ref):
    tf = wg_ref.shape[1]
    w = jnp.concatenate([wg_ref[...].astype(BF16), wu_ref[...].astype(BF16)], axis=1)
    gu = _dot(h_ref[...], w)
    o_ref[...] = (jax.nn.silu(gu[:, :tf]) * gu[:, tf:]).astype(o_ref.dtype)


def _ffn_up(h, w_gate, w_up, tm, tf):
    m, k = h.shape
    f = w_gate.shape[1]
    return pl.pallas_call(
        _ffn_up_kernel,
        grid=(m // tm, f // tf),
        in_specs=[pl.BlockSpec((tm, k), lambda i, j: (i, 0)),
                  pl.BlockSpec((k, tf), lambda i, j: (0, j)),
                  pl.BlockSpec((k, tf), lambda i, j: (0, j))],
        out_specs=pl.BlockSpec((tm, tf), lambda i, j: (i, j)),
        out_shape=jax.ShapeDtypeStruct((m, f), BF16),
        compiler_params=_params("parallel", "arbitrary"),
        name="ffn_up",
    )(h, w_gate, w_up)


def _ffn_down_kernel(a_ref, w_ref, x_ref, o_ref):
    k = pl.program_id(2)
    part = _dot(a_ref[...], w_ref[...].astype(BF16))

    @pl.when(k == 0)
    def _():
        o_ref[...] = x_ref[...] + part

    @pl.when(k != 0)
    def _():
        o_ref[...] += part


def _ffn_down(a, w_down, x, tm, tn, tk):
    m, f = a.shape
    n = w_down.shape[1]
    return pl.pallas_call(
        _ffn_down_kernel,
        grid=(m // tm, n // tn, f // tk),
        in_specs=[pl.BlockSpec((tm, tk), lambda i, j, k: (i, k)),
                  pl.BlockSpec((tk, tn), lambda i, j, k: (k, j)),
                  pl.BlockSpec((tm, tn), lambda i, j, k: (i, j))],
        out_specs=pl.BlockSpec((tm, tn), lambda i, j, k: (i, j)),
        out_shape=jax.ShapeDtypeStruct((m, n), F32),
        compiler_params=_params("parallel", "parallel", "arbitrary"),
        name="ffn_down",
    )(a, w_down, x)


def _rotary(x, cos, sin):
    half = RET_HEAD_DIM // 2
    x1, x2 = x[:, :half], x[:, half:]
    return jnp.concatenate([x1 * cos - x2 * sin, x1 * sin + x2 * cos], axis=-1)


def _retention_chunk(lg, sdec, s, q, k, v, g, cos, sin, gn):
    n = q.shape[0]
    ri = lax.broadcasted_iota(jnp.int32, (n, n), 0)
    ci = lax.broadcasted_iota(jnp.int32, (n, n), 1)
    diff = (ri - ci).astype(F32)
    mask = jnp.where(diff >= 0, jnp.exp(jnp.maximum(diff, 0.0) * lg), 0.0)
    row = lax.broadcasted_iota(jnp.int32, (n, 1), 0).astype(F32)
    q_dec = jnp.exp(lg * (row + 1.0))
    k_dec = jnp.exp(lg * (n - 1.0 - row))

    qr = _rotary(q, cos, sin)
    kr = _rotary(k, cos, sin) * (RET_HEAD_DIM ** -0.5)
    vb = v.astype(BF16)
    scores = lax.dot_general(qr.astype(BF16), kr.astype(BF16), (((1,), (1,)), ((), ())),
                             preferred_element_type=F32) * mask
    o = _dot(scores.astype(BF16), vb) + _dot((qr * q_dec).astype(BF16), s.astype(BF16))
    s_new = sdec * s + lax.dot_general((kr * k_dec).astype(BF16), vb, (((0,), (0,)), ((), ())),
                                       preferred_element_type=F32)
    mu = jnp.mean(o, axis=-1, keepdims=True)
    oc = o - mu
    var = jnp.mean(oc * oc, axis=-1, keepdims=True)
    y = oc * lax.rsqrt(var + GN_EPS) * gn
    return y * jax.nn.silu(g), s_new


def _ret_seq_kernel(dec_ref, q_ref, k_ref, v_ref, g_ref, cos_ref, sin_ref, s0_ref, gn_ref,
                    o_ref, sf_ref, s_scr, *, chunk, nchunks):
    h = pl.program_id(1)
    lg = dec_ref[0, h]
    sdec = dec_ref[1, h]
    s_scr[...] = s0_ref[...]
    gn = gn_ref[...]

    def body(c, carry):
        rows = pl.ds(pl.multiple_of(c * chunk, chunk), chunk)
        y, s_new = _retention_chunk(lg, sdec, s_scr[...], q_ref[rows, :], k_ref[rows, :],
                                    v_ref[rows, :], g_ref[rows, :], cos_ref[rows, :],
                                    sin_ref[rows, :], gn)
        o_ref[rows, :] = y.astype(o_ref.dtype)
        s_scr[...] = s_new
        return carry

    lax.fori_loop(0, nchunks, body, 0)
    sf_ref[...] = s_scr[...]


def _ret_decay_consts(chunk):
    lg = np.log(1.0 - 2.0 ** (-5.0 - np.arange(RET_HEADS, dtype=np.float64)))
    return jnp.asarray(np.stack([lg, np.exp(lg * chunk)]), dtype=F32)


def _rope_tables(pos):
    half = RET_HEAD_DIM // 2
    inv = ROPE_BASE ** (-np.arange(half, dtype=np.float64) / half)
    ang = np.asarray(pos, dtype=np.float64)[:, None] * inv[None, :]
    return jnp.asarray(np.cos(ang), dtype=F32), jnp.asarray(np.sin(ang), dtype=F32)


def _retention_seq(proj, row_block0, nbatch, seq, chunk, pos0, s0, gn_g):
    cos, sin = _rope_tables(pos0 + np.arange(seq))
    hd = RET_HEAD_DIM

    def col(off):
        return pl.BlockSpec((seq, hd), lambda b, h: (row_block0 + b, off + h))

    tbl = pl.BlockSpec((seq, hd // 2), lambda b, h: (0, 0))
    return pl.pallas_call(
        functools.partial(_ret_seq_kernel, chunk=chunk, nchunks=seq // chunk),
        grid=(nbatch, RET_HEADS),
        in_specs=[pl.BlockSpec(memory_space=pltpu.SMEM),
                  col(0), col(RET_HEADS), col(2 * RET_HEADS), col(3 * RET_HEADS), tbl, tbl,
                  pl.BlockSpec((None, None, hd, hd), lambda b, h: (0, h, 0, 0)),
                  pl.BlockSpec((1, hd), lambda b, h: (0, h))],
        out_specs=[pl.BlockSpec((seq, hd), lambda b, h: (b, h)),
                   pl.BlockSpec((None, None, hd, hd), lambda b, h: (b, h, 0, 0))],
        out_shape=[jax.ShapeDtypeStruct((nbatch * seq, RET_WIDTH), BF16),
                   jax.ShapeDtypeStruct((nbatch, RET_HEADS, hd, hd), F32)],
        scratch_shapes=[pltpu.VMEM((hd, hd), F32)],
        compiler_params=_params("parallel", "arbitrary"),
        name="retention_seq",
    )(_ret_decay_consts(chunk), proj, proj, proj, proj, cos, sin, s0, gn_g.reshape(1, RET_WIDTH))


def _ret_step_kernel(dec_ref, q_ref, k_ref, v_ref, g_ref, cos_ref, sin_ref, s_ref, gn_ref,
                     o_ref, sn_ref, *, nb, seq):
    hd = RET_HEAD_DIM
    cos, sin = cos_ref[...], sin_ref[...]
    for b in range(nb):
        rows = slice(b * seq, (b + 1) * seq)
        for h in range(RET_HEADS):
            cols = slice(h * hd, (h + 1) * hd)
            y, s_new = _retention_chunk(dec_ref[0, h], dec_ref[1, h], s_ref[b, h],
                                        q_ref[rows, cols], k_ref[rows, cols], v_ref[rows, cols],
                                        g_ref[rows, cols], cos, sin, gn_ref[:, cols])
            o_ref[rows, cols] = y.astype(o_ref.dtype)
            sn_ref[b, h] = s_new


def _retention_step(proj, row0, nbatch, seq, pos0, state, gn_g, nb):
    cos, sin = _rope_tables(pos0 + np.arange(seq))
    hd = RET_HEAD_DIM
    rb0 = row0 // (nb * seq)

    def col(off):
        return pl.BlockSpec((nb * seq, RET_WIDTH), lambda i: (rb0 + i, off))

    tbl = pl.BlockSpec((seq, hd // 2), lambda i: (0, 0))
    st = pl.BlockSpec((nb, RET_HEADS, hd, hd), lambda i: (i, 0, 0, 0))
    return pl.pallas_call(
        functools.partial(_ret_step_kernel, nb=nb, seq=seq),
        grid=(nbatch // nb,),
        in_specs=[pl.BlockSpec(memory_space=pltpu.SMEM),
                  col(0), col(1), col(2), col(3), tbl, tbl, st,
                  pl.BlockSpec((1, RET_WIDTH), lambda i: (0, 0))],
        out_specs=[pl.BlockSpec((nb * seq, RET_WIDTH), lambda i: (i, 0)), st],
        out_shape=[jax.ShapeDtypeStruct((nbatch * seq, RET_WIDTH), BF16),
                   jax.ShapeDtypeStruct(state.shape, F32)],
        compiler_params=_params("parallel"),
        name="retention_step",
    )(_ret_decay_consts(seq), proj, proj, proj, proj, cos, sin, state,
      gn_g.reshape(1, RET_WIDTH))


def _s5_disc_kernel(lr_ref, li_ref, ldt_ref, br_ref, bi_ref, ar_ref, ai_ref, bbr_ref, bbi_ref):
    lr, li = lr_ref[...], li_ref[...]
    dt = jnp.exp(ldt_ref[...])
    mag = jnp.exp(lr * dt)
    ar = mag * jnp.cos(li * dt)
    ai = mag * jnp.sin(li * dt)
    nr, ni = ar - 1.0, ai
    den = lr * lr + li * li
    fr = (nr * lr + ni * li) / den
    fi = (ni * lr - nr * li) / den
    br, bi = br_ref[...], bi_ref[...]
    ar_ref[...] = ar
    ai_ref[...] = ai
    bbr_ref[...] = fr * br - fi * bi
    bbi_ref[...] = fr * bi + fi * br


def _s5_discretize(lam_re, lam_im, log_dt, b_re, b_im):
    g, p, c = b_re.shape
    shp3 = jax.ShapeDtypeStruct((g, 1, p), F32)
    shpb = jax.ShapeDtypeStruct((g, c, p), F32)
    return pl.pallas_call(
        _s5_disc_kernel,
        out_shape=[shp3, shp3, shpb, shpb],
        name="s5_discretize",
    )(lam_re.reshape(g, 1, p), lam_im.reshape(g, 1, p), log_dt.reshape(g, 1, 1),
      jnp.swapaxes(b_re, 1, 2), jnp.swapaxes(b_im, 1, 2))


def _block_diag(w):
    g, a, b = w.shape
    w = w.reshape(g // S5_GB, S5_GB, a, 1, b)
    eye = jnp.eye(S5_GB, dtype=w.dtype).reshape(1, S5_GB, 1, S5_GB, 1)
    return (w * eye).reshape(g // S5_GB, S5_GB * a, S5_GB * b)


def _s5_kernel(u_ref, x0r_ref, x0i_ref, ar_ref, ai_ref, b_ref, c_ref, d_ref,
               z_ref, xfr_ref, xfi_ref, s_scr, *, nb, seq, tc, pitch):
    half = S5_SLABS // 2
    bw = b_ref[...].astype(BF16)
    cw = c_ref[...].astype(BF16)
    dvec = d_ref[...]
    ar = [jnp.broadcast_to(ar_ref[:, s * LANES:(s + 1) * LANES], (nb, LANES)) for s in range(half)]
    ai = [jnp.broadcast_to(ai_ref[:, s * LANES:(s + 1) * LANES], (nb, LANES)) for s in range(half)]
    whole = tc == seq
    spans = [(0, nb * seq)] if whole else [(b * pitch, tc) for b in range(nb)]

    def u_rows(t0, idx):
        if whole:
            return u_ref[...]
        return u_ref[pl.ds(pl.multiple_of(idx * seq + t0, tc), tc), :]

    def step_rows(t):
        if nb == 1:
            return pl.ds(t, 1)
        return pl.ds(t, nb, stride=pitch)

    def chunk(t0, xs):
        for idx, (r0, nrows) in enumerate(spans):
            bu = _dot(u_rows(t0, idx).astype(BF16), bw)
            for s in range(S5_SLABS):
                s_scr[s, r0:r0 + nrows, :] = bu[:, s * LANES:(s + 1) * LANES]

        def step(t, xs):
            rows = step_rows(t)
            new = [None] * S5_SLABS
            for s in range(half):
                xr, xi = xs[s], xs[half + s]
                nr = ar[s] * xr - ai[s] * xi + s_scr[s, rows, :]
                ni = ar[s] * xi + ai[s] * xr + s_scr[half + s, rows, :]
                s_scr[s, rows, :] = nr
                s_scr[half + s, rows, :] = ni
                new[s], new[half + s] = nr, ni
            return tuple(new)

        xs = lax.fori_loop(0, tc, step, xs)
        for idx, (r0, nrows) in enumerate(spans):
            x = jnp.concatenate([s_scr[s, r0:r0 + nrows, :] for s in range(S5_SLABS)], axis=1)
            u = u_rows(t0, idx)
            y = _dot(x.astype(BF16), cw) + dvec * u
            z = jax.nn.gelu(y)
            if whole:
                z_ref[...] = z
            else:
                z_ref[pl.ds(pl.multiple_of(idx * seq + t0, tc), tc), :] = z
        return xs

    xs0 = tuple([x0r_ref[:, s * LANES:(s + 1) * LANES] for s in range(half)]
                + [x0i_ref[:, s * LANES:(s + 1) * LANES] for s in range(half)])
    if whole:
        xs = chunk(0, xs0)
    else:
        xs = lax.fori_loop(0, seq // tc, lambda c, xs: chunk(c * tc, xs), xs0)
    for s in range(half):
        xfr_ref[:, s * LANES:(s + 1) * LANES] = xs[s]
        xfi_ref[:, s * LANES:(s + 1) * LANES] = xs[half + s]


def _s5(proj, row_block0, nb, seq, tc, x0r, x0i, ar, ai, bblk, cblk, d):
    rows = nb * seq
    pitch = seq if tc == seq else tc + 8
    u_col0 = 4 * RET_WIDTH // S5_CH
    nj = S5_GROUPS // S5_GB
    st = pl.BlockSpec((nb, S5_ST), lambda j: (0, j))
    coef = pl.BlockSpec((1, S5_ST), lambda j: (0, j))
    return pl.pallas_call(
        functools.partial(_s5_kernel, nb=nb, seq=seq, tc=tc, pitch=pitch),
        grid=(nj,),
        in_specs=[pl.BlockSpec((rows, S5_CH), lambda j: (row_block0, u_col0 + j)),
                  st, st, coef, coef,
                  pl.BlockSpec((None, S5_CH, 2 * S5_ST), lambda j: (j, 0, 0)),
                  pl.BlockSpec((None, 2 * S5_ST, S5_CH), lambda j: (j, 0, 0)),
                  pl.BlockSpec((1, S5_CH), lambda j: (0, j))],
        out_specs=[pl.BlockSpec((rows, S5_CH), lambda j: (0, j)), st, st],
        out_shape=[jax.ShapeDtypeStruct((rows, S5_WIDTH), F32),
                   jax.ShapeDtypeStruct((nb, S5_NSTATE), F32),
                   jax.ShapeDtypeStruct((nb, S5_NSTATE), F32)],
        scratch_shapes=[pltpu.VMEM((S5_SLABS, nb * pitch, LANES), F32)],
        compiler_params=_params("parallel"),
        name="s5_scan",
    )(proj, x0r, x0i, ar, ai, bblk, cblk, d.reshape(1, S5_WIDTH))


def _s5_post_kernel(zp_ref, zs_ref, zpc_ref, zsc_ref, w_ref, b_ref, g_ref, o_ref,
                    gate_scr, ssq_scr, *, n_first, tn):
    i, j = pl.program_id(0), pl.program_id(1)
    first = i < n_first
    z = jnp.where(first, zp_ref[...], zs_ref[...])
    zc = jnp.where(first, zpc_ref[...], zsc_ref[...])
    t = _dot(z.astype(BF16), w_ref[...].astype(BF16)) + b_ref[...]
    s = zc * jax.nn.sigmoid(t)
    gate_scr[j] = s
    part = jnp.sum(s * s, axis=-1, keepdims=True)

    @pl.when(j == 0)
    def _():
        ssq_scr[...] = part

    @pl.when(j != 0)
    def _():
        ssq_scr[...] += part

    @pl.when(j == pl.num_programs(1) - 1)
    def _():
        r = lax.rsqrt(ssq_scr[...] / (tn * pl.num_programs(1)) + EPS)
        for jj in range(gate_scr.shape[0]):
            cols = slice(jj * tn, (jj + 1) * tn)
            o_ref[:, cols] = (gate_scr[jj] * r * g_ref[:, cols]).astype(o_ref.dtype)


def _s5_post(zp, zs, w_glu, b_glu, norm_g, tm, tn):
    d = zp.shape[1]
    n_first = zp.shape[0] // tm
    nblk = n_first + zs.shape[0] // tm
    nj = d // tn

    def first_row(i):
        return jnp.minimum(i, n_first - 1)

    def second_row(i):
        return jnp.maximum(i - n_first, 0)

    return pl.pallas_call(
        functools.partial(_s5_post_kernel, n_first=n_first, tn=tn),
        grid=(nblk, nj),
        in_specs=[pl.BlockSpec((tm, d), lambda i, j: (first_row(i), 0)),
                  pl.BlockSpec((tm, d), lambda i, j: (second_row(i), 0)),
                  pl.BlockSpec((tm, tn), lambda i, j: (first_row(i), jnp.where(i < n_first, j, nj - 1))),
                  pl.BlockSpec((tm, tn), lambda i, j: (second_row(i), jnp.where(i < n_first, 0, j))),
                  pl.BlockSpec((d, tn), lambda i, j: (0, j)),
                  pl.BlockSpec((1, tn), lambda i, j: (0, j)),
                  pl.BlockSpec((1, d), lambda i, j: (0, 0))],
        out_specs=pl.BlockSpec((tm, d), lambda i, j: (i, 0)),
        out_shape=jax.ShapeDtypeStruct((nblk * tm, d), BF16),
        scratch_shapes=[pltpu.VMEM((nj, tm, tn), F32), pltpu.VMEM((tm, 1), F32)],
        compiler_params=_params("parallel", "arbitrary"),
        name="s5_post",
    )(zp, zs, zp, zs, w_glu, b_glu.reshape(1, d), norm_g.reshape(1, d))


def kernel(x_prompt, x_sample, state_ret, state_s5_re, state_s5_im, meta_tokens, norm1_g, w_in, ret_gn_g, s5_lam_re, s5_lam_im, s5_log_dt, s5_b_re, s5_b_im, s5_c_re, s5_c_im, s5_d, w_glu, b_glu, s5_norm_g, w_out, norm2_g, w_gate, w_up, w_down, final_norm_g):
    assert norm1_g.shape[0] == 1, "single-layer model"
    batch, seq, d = x_prompt.shape
    dbatch, dseq, _ = x_sample.shape
    rows_p, rows_s = batch * seq, dbatch * dseq
    tm = 1024
    assert rows_p % tm == 0 and rows_s % tm == 0 and seq % RET_CHUNK == 0
    xp = x_prompt.reshape(rows_p, d)
    xs = x_sample.reshape(rows_s, d)
    hd = RET_HEAD_DIM

    ar, ai, bbr, bbi = _s5_discretize(s5_lam_re[0], s5_lam_im[0], s5_log_dt[0], s5_b_re[0], s5_b_im[0])
    ar = ar.reshape(1, S5_NSTATE)
    ai = ai.reshape(1, S5_NSTATE)
    bblk = jnp.concatenate([_block_diag(bbr), _block_diag(bbi)], axis=2)
    cblk = jnp.concatenate([_block_diag(jnp.swapaxes(s5_c_re[0], 1, 2)),
                            _block_diag(-jnp.swapaxes(s5_c_im[0], 1, 2))], axis=1)

    h_meta = _rmsnorm([meta_tokens], norm1_g[0], N_META, BF16)
    proj_meta = _inproj(h_meta, w_in[0], N_META, 512)
    zero_ret = jnp.zeros((1, RET_HEADS, hd, hd), F32)
    _, s_meta = _retention_seq(proj_meta, 0, 1, N_META, N_META, 0, zero_ret, ret_gn_g[0])
    zero_s5 = jnp.zeros((1, S5_NSTATE), F32)
    _, mr, mi = _s5(proj_meta, 0, 1, N_META, N_META, zero_s5, zero_s5, ar, ai, bblk, cblk, s5_d[0])

    h = _rmsnorm([xp, xs], norm1_g[0], 512, BF16)
    proj = _inproj(h, w_in[0], tm, 512)

    ret_p, sret_p = _retention_seq(proj, 0, batch, seq, RET_CHUNK, N_META, s_meta, ret_gn_g[0])
    ret_s, sret_s = _retention_step(proj, rows_p, dbatch, dseq, PAST_LEN, state_ret[0],
                                    ret_gn_g[0], 2)
    ret = jnp.concatenate([ret_p, ret_s], axis=0)

    z_p, s5r_p, s5i_p = _s5(proj, 0, batch, seq, 256,
                            jnp.broadcast_to(mr, (batch, S5_NSTATE)),
                            jnp.broadcast_to(mi, (batch, S5_NSTATE)),
                            ar, ai, bblk, cblk, s5_d[0])
    z_s, s5r_s, s5i_s = _s5(proj, rows_p // rows_s, dbatch, dseq, dseq,
                            state_s5_re[0].reshape(dbatch, S5_NSTATE),
                            state_s5_im[0].reshape(dbatch, S5_NSTATE),
                            ar, ai, bblk, cblk, s5_d[0])
    s5_out = _s5_post(z_p, z_s, w_glu[0], b_glu[0], s5_norm_g[0], 512, 512)

    x1 = _outproj(ret, s5_out, w_out[0], xp, xs, tm, 512)
    h2 = _rmsnorm([x1], norm2_g[0], 512, BF16)
    a = _ffn_up(h2, w_gate[0], w_up[0], tm, 256)
    x2 = _ffn_down(a, w_down[0], x1, tm, 2048, 256)
    y_p, y_s = _final_norm(x2, final_norm_g, rows_p, 512)

    st = (1, -1, S5_GROUPS, S5_STATE)
    return (y_p.reshape(batch, seq, d), y_s.reshape(dbatch, dseq, d),
            sret_p[None], s5r_p.reshape(st), s5i_p.reshape(st),
            sret_s[None], s5r_s.reshape(st), s5i_s.reshape(st))
```

```python
import functools
import math

import numpy as np
import jax
import jax.numpy as jnp
from jax import lax
from jax.experimental import pallas as pl
from jax.experimental.pallas import tpu as pltpu

D_MODEL = 4096
N_META = 16
PAST_LEN = 16384
RET_WIDTH = D_MODEL // 2
S5_WIDTH = D_MODEL - RET_WIDTH
RET_HEADS = 8
RET_HEAD_DIM = RET_WIDTH // RET_HEADS
RET_CHUNK = 128
ROPE_BASE = 10000.0
S5_GROUP = 16
S5_GROUPS = S5_WIDTH // S5_GROUP
S5_STATE = 64
S5_NSTATE = S5_GROUPS * S5_STATE
IN_COLS = 4 * RET_WIDTH + S5_WIDTH
EPS = 1e-6
GN_EPS = 1e-5

LANES = 128
S5_GB = 16
S5_CH = S5_GB * S5_GROUP
S5_ST = S5_GB * S5_STATE
S5_SLABS = 2 * S5_ST // LANES
VMEM_LIMIT = 56 * 1024 * 1024

BF16 = jnp.bfloat16
F32 = jnp.float32


def _params(*sem):
    return pltpu.CompilerParams(dimension_semantics=sem, vmem_limit_bytes=VMEM_LIMIT)


def _dot(a, b):
    return jnp.dot(a, b, preferred_element_type=F32)


def _rmsnorm_kernel(*refs, n_src, n_first):
    x_refs, g_ref, o_ref = refs[:n_src], refs[n_src], refs[n_src + 1]
    x = x_refs[0][...]
    if n_src == 2:
        x = jnp.where(pl.program_id(0) < n_first, x, x_refs[1][...])
    r = lax.rsqrt(jnp.mean(x * x, axis=-1, keepdims=True) + EPS)
    o_ref[...] = (x * r * g_ref[...]).astype(o_ref.dtype)


def _rmsnorm(srcs, g, tm, out_dtype):
    d = srcs[0].shape[1]
    nblk = [s.shape[0] // tm for s in srcs]
    n_first = nblk[0]
    specs = [pl.BlockSpec((tm, d), lambda i: (jnp.minimum(i, n_first - 1), 0))]
    if len(srcs) == 2:
        specs.append(pl.BlockSpec((tm, d), lambda i: (jnp.maximum(i - n_first, 0), 0)))
    rows = sum(s.shape[0] for s in srcs)
    return pl.pallas_call(
        functools.partial(_rmsnorm_kernel, n_src=len(srcs), n_first=n_first),
        grid=(sum(nblk),),
        in_specs=specs + [pl.BlockSpec((1, d), lambda i: (0, 0))],
        out_specs=pl.BlockSpec((tm, d), lambda i: (i, 0)),
        out_shape=jax.ShapeDtypeStruct((rows, d), out_dtype),
        compiler_params=_params("parallel"),
        name="rmsnorm",
    )(*srcs, g.reshape(1, d))


def _final_norm_kernel(x_ref, g_ref, yp_ref, ys_ref, *, n_first):
    x = x_ref[...]
    r = lax.rsqrt(jnp.mean(x * x, axis=-1, keepdims=True) + EPS)
    y = x * r * g_ref[...]
    i = pl.program_id(0)

    @pl.when(i < n_first)
    def _():
        yp_ref[...] = y

    @pl.when(i >= n_first)
    def _():
        ys_ref[...] = y


def _final_norm(x, g, rows_p, tm):
    rows, d = x.shape
    n_first = rows_p // tm
    return pl.pallas_call(
        functools.partial(_final_norm_kernel, n_first=n_first),
        grid=(rows // tm,),
        in_specs=[pl.BlockSpec((tm, d), lambda i: (i, 0)),
                  pl.BlockSpec((1, d), lambda i: (0, 0))],
        out_specs=[pl.BlockSpec((tm, d), lambda i: (jnp.minimum(i, n_first - 1), 0)),
                   pl.BlockSpec((tm, d), lambda i: (jnp.maximum(i - n_first, 0), 0))],
        out_shape=[jax.ShapeDtypeStruct((rows_p, d), F32),
                   jax.ShapeDtypeStruct((rows - rows_p, d), F32)],
        compiler_params=_params("arbitrary"),
        name="final_norm",
    )(x, g.reshape(1, d))


def _inproj_kernel(h_ref, w_ref, o_ref):
    o_ref[...] = _dot(h_ref[...], w_ref[...].astype(BF16))


def _inproj(h, w, tm, tn):
    m, k = h.shape
    n = w.shape[1]
    return pl.pallas_call(
        _inproj_kernel,
        grid=(m // tm, n // tn),
        in_specs=[pl.BlockSpec((tm, k), lambda i, j: (i, 0)),
                  pl.BlockSpec((k, tn), lambda i, j: (0, j))],
        out_specs=pl.BlockSpec((tm, tn), lambda i, j: (i, j)),
        out_shape=jax.ShapeDtypeStruct((m, n), F32),
        compiler_params=_params("parallel", "arbitrary"),
        name="inproj",
    )(h, w)


def _outproj_kernel(retp_ref, rets_ref, s5_ref, wt_ref, wb_ref, xp_ref, xs_ref, o_ref, *, n_first):
    first = pl.program_id(0) < n_first
    x = jnp.where(first, xp_ref[...], xs_ref[...])
    ret = jnp.where(first, retp_ref[...], rets_ref[...])
    acc = _dot(ret, wt_ref[...].astype(BF16))
    acc += _dot(s5_ref[...], wb_ref[...].astype(BF16))
    o_ref[...] = x + acc


def _outproj(ret_p, ret_s, s5, w_out, xp, xs, tm, tn):
    m, kh = s5.shape
    n = w_out.shape[1]
    n_first = xp.shape[0] // tm
    nj = n // tn
    return pl.pallas_call(
        functools.partial(_outproj_kernel, n_first=n_first),
        grid=(m // tm, nj),
        in_specs=[pl.BlockSpec((tm, kh), lambda i, j: (jnp.minimum(i, n_first - 1), 0)),
                  pl.BlockSpec((tm, kh), lambda i, j: (jnp.maximum(i - n_first, 0), 0)),
                  pl.BlockSpec((tm, kh), lambda i, j: (i, 0)),
                  pl.BlockSpec((kh, tn), lambda i, j: (0, j)),
                  pl.BlockSpec((kh, tn), lambda i, j: (1, j)),
                  pl.BlockSpec((tm, tn), lambda i, j: (jnp.minimum(i, n_first - 1),
                                                       jnp.where(i < n_first, j, nj - 1))),
                  pl.BlockSpec((tm, tn), lambda i, j: (jnp.maximum(i - n_first, 0),
                                                       jnp.where(i < n_first, 0, j)))],
        out_specs=pl.BlockSpec((tm, tn), lambda i, j: (i, j)),
        out_shape=jax.ShapeDtypeStruct((m, n), F32),
        compiler_params=_params("parallel", "arbitrary"),
        name="outproj",
    )(ret_p, ret_s, s5, w_out, w_out, xp, xs)


def _ffn_up_kernel(h_ref, wg_ref, wu_ref, o_ref):
    tf = wg_ref.shape[1]
    w = jnp.concatenate([wg_ref[...].astype(BF16), wu_ref[...].astype(BF16)], axis=1)
    gu = _dot(h_ref[...], w)
    o_ref[...] = (jax.nn.silu(gu[:, :tf]) * gu[:, tf:]).astype(o_ref.dtype)


def _ffn_up(h, w_gate, w_up, tm, tf):
    m, k = h.shape
    f = w_gate.shape[1]
    return pl.pallas_call(
        _ffn_up_kernel,
        grid=(m // tm, f // tf),
        in_specs=[pl.BlockSpec((tm, k), lambda i, j: (i, 0)),
                  pl.BlockSpec((k, tf), lambda i, j: (0, j)),
                  pl.BlockSpec((k, tf), lambda i, j: (0, j))],
        out_specs=pl.BlockSpec((tm, tf), lambda i, j: (i, j)),
        out_shape=jax.ShapeDtypeStruct((m, f), BF16),
        compiler_params=_params("parallel", "arbitrary"),
        name="ffn_up",
    )(h, w_gate, w_up)


def _ffn_down_kernel(a_ref, w_ref, x_ref, o_ref):
    o_ref[...] = x_ref[...] + _dot(a_ref[...], w_ref[...].astype(BF16))


def _ffn_down(a, w_down, x, kblock, nkblocks, tm, tn):
    m, f = a.shape
    n = w_down.shape[1]
    tk = f // nkblocks
    return pl.pallas_call(
        _ffn_down_kernel,
        grid=(m // tm, n // tn),
        in_specs=[pl.BlockSpec((tm, tk), lambda i, j: (i, kblock)),
                  pl.BlockSpec((tk, tn), lambda i, j: (kblock, j)),
                  pl.BlockSpec((tm, tn), lambda i, j: (i, j))],
        out_specs=pl.BlockSpec((tm, tn), lambda i, j: (i, j)),
        out_shape=jax.ShapeDtypeStruct((m, n), F32),
        compiler_params=_params("parallel", "arbitrary"),
        name="ffn_down",
    )(a, w_down, x)


def _rotary(x, cos, sin):
    half = RET_HEAD_DIM // 2
    x1, x2 = x[:, :half], x[:, half:]
    return jnp.concatenate([x1 * cos - x2 * sin, x1 * sin + x2 * cos], axis=-1)


def _retention_chunk(lg, sdec, s, q, k, v, g, cos, sin, gn):
    n = q.shape[0]
    ri = lax.broadcasted_iota(jnp.int32, (n, n), 0)
    ci = lax.broadcasted_iota(jnp.int32, (n, n), 1)
    diff = (ri - ci).astype(F32)
    mask = jnp.where(diff >= 0, jnp.exp(jnp.maximum(diff, 0.0) * lg), 0.0)
    row = lax.broadcasted_iota(jnp.int32, (n, 1), 0).astype(F32)
    q_dec = jnp.exp(lg * (row + 1.0))
    k_dec = jnp.exp(lg * (n - 1.0 - row))

    qr = _rotary(q, cos, sin)
    kr = _rotary(k, cos, sin) * (RET_HEAD_DIM ** -0.5)
    vb = v.astype(BF16)
    scores = lax.dot_general(qr.astype(BF16), kr.astype(BF16), (((1,), (1,)), ((), ())),
                             preferred_element_type=F32) * mask
    o = _dot(scores.astype(BF16), vb) + _dot((qr * q_dec).astype(BF16), s.astype(BF16))
    s_new = sdec * s + lax.dot_general((kr * k_dec).astype(BF16), vb, (((0,), (0,)), ((), ())),
                                       preferred_element_type=F32)
    mu = jnp.mean(o, axis=-1, keepdims=True)
    oc = o - mu
    var = jnp.mean(oc * oc, axis=-1, keepdims=True)
    y = oc * lax.rsqrt(var + GN_EPS) * gn
    return y * jax.nn.silu(g), s_new


def _ret_seq_kernel(dec_ref, q_ref, k_ref, v_ref, g_ref, cos_ref, sin_ref, s0_ref, gn_ref,
                    o_ref, sf_ref, s_scr, *, chunk, nchunks):
    h = pl.program_id(1)
    lg = dec_ref[0, h]
    sdec = dec_ref[1, h]
    s_scr[...] = s0_ref[...]
    gn = gn_ref[...]

    def body(c, carry):
        rows = pl.ds(pl.multiple_of(c * chunk, chunk), chunk)
        y, s_new = _retention_chunk(lg, sdec, s_scr[...], q_ref[rows, :], k_ref[rows, :],
                                    v_ref[rows, :], g_ref[rows, :], cos_ref[rows, :],
                                    sin_ref[rows, :], gn)
        o_ref[rows, :] = y.astype(o_ref.dtype)
        s_scr[...] = s_new
        return carry

    lax.fori_loop(0, nchunks, body, 0, unroll=min(4, nchunks))
    sf_ref[...] = s_scr[...]


def _ret_decay_consts(chunk):
    lg = np.log(1.0 - 2.0 ** (-5.0 - np.arange(RET_HEADS, dtype=np.float64)))
    return jnp.asarray(np.stack([lg, np.exp(lg * chunk)]), dtype=F32)


def _rope_tables(pos):
    half = RET_HEAD_DIM // 2
    inv = ROPE_BASE ** (-np.arange(half, dtype=np.float64) / half)
    ang = np.asarray(pos, dtype=np.float64)[:, None] * inv[None, :]
    return jnp.asarray(np.cos(ang), dtype=F32), jnp.asarray(np.sin(ang), dtype=F32)


def _retention_seq(proj, row_block0, nbatch, seq, chunk, pos0, s0, gn_g):
    cos, sin = _rope_tables(pos0 + np.arange(seq))
    hd = RET_HEAD_DIM

    def col(off):
        return pl.BlockSpec((seq, hd), lambda b, h: (row_block0 + b, off + h))

    tbl = pl.BlockSpec((seq, hd // 2), lambda b, h: (0, 0))
    return pl.pallas_call(
        functools.partial(_ret_seq_kernel, chunk=chunk, nchunks=seq // chunk),
        grid=(nbatch, RET_HEADS),
        in_specs=[pl.BlockSpec(memory_space=pltpu.SMEM),
                  col(0), col(RET_HEADS), col(2 * RET_HEADS), col(3 * RET_HEADS), tbl, tbl,
                  pl.BlockSpec((None, None, hd, hd), lambda b, h: (0, h, 0, 0)),
                  pl.BlockSpec((1, hd), lambda b, h: (0, h))],
        out_specs=[pl.BlockSpec((seq, hd), lambda b, h: (b, h)),
                   pl.BlockSpec((None, None, hd, hd), lambda b, h: (b, h, 0, 0))],
        out_shape=[jax.ShapeDtypeStruct((nbatch * seq, RET_WIDTH), BF16),
                   jax.ShapeDtypeStruct((nbatch, RET_HEADS, hd, hd), F32)],
        scratch_shapes=[pltpu.VMEM((hd, hd), F32)],
        compiler_params=_params("parallel", "arbitrary"),
        name="retention_seq",
    )(_ret_decay_consts(chunk), proj, proj, proj, proj, cos, sin, s0, gn_g.reshape(1, RET_WIDTH))


def _ret_step_kernel(dec_ref, q_ref, k_ref, v_ref, g_ref, cos_ref, sin_ref, s_ref, gn_ref,
                     o_ref, sn_ref, *, nb, seq):
    hd = RET_HEAD_DIM
    cos, sin = cos_ref[...], sin_ref[...]
    for b in range(nb):
        rows = slice(b * seq, (b + 1) * seq)
        for h in range(RET_HEADS):
            cols = slice(h * hd, (h + 1) * hd)
            y, s_new = _retention_chunk(dec_ref[0, h], dec_ref[1, h], s_ref[b, h],
                                        q_ref[rows, cols], k_ref[rows, cols], v_ref[rows, cols],
                                        g_ref[rows, cols], cos, sin, gn_ref[:, cols])
            o_ref[rows, cols] = y.astype(o_ref.dtype)
            sn_ref[b, h] = s_new


def _retention_step(proj, row0, nbatch, seq, pos0, state, gn_g, nb):
    cos, sin = _rope_tables(pos0 + np.arange(seq))
    hd = RET_HEAD_DIM
    rb0 = row0 // (nb * seq)

    def col(off):
        return pl.BlockSpec((nb * seq, RET_WIDTH), lambda i: (rb0 + i, off))

    tbl = pl.BlockSpec((seq, hd // 2), lambda i: (0, 0))
    st = pl.BlockSpec((nb, RET_HEADS, hd, hd), lambda i: (i, 0, 0, 0))
    return pl.pallas_call(
        functools.partial(_ret_step_kernel, nb=nb, seq=seq),
        grid=(nbatch // nb,),
        in_specs=[pl.BlockSpec(memory_space=pltpu.SMEM),
                  col(0), col(1), col(2), col(3), tbl, tbl, st,
                  pl.BlockSpec((1, RET_WIDTH), lambda i: (0, 0))],
        out_specs=[pl.BlockSpec((nb * seq, RET_WIDTH), lambda i: (i, 0)), st],
        out_shape=[jax.ShapeDtypeStruct((nbatch * seq, RET_WIDTH), BF16),
                   jax.ShapeDtypeStruct(state.shape, F32)],
        compiler_params=_params("parallel"),
        name="retention_step",
    )(_ret_decay_consts(seq), proj, proj, proj, proj, cos, sin, state,
      gn_g.reshape(1, RET_WIDTH))


def _s5_disc_kernel(lr_ref, li_ref, ldt_ref, br_ref, bi_ref, ar_ref, ai_ref, bbr_ref, bbi_ref):
    lr, li = lr_ref[...], li_ref[...]
    dt = jnp.exp(ldt_ref[...])
    mag = jnp.exp(lr * dt)
    ar = mag * jnp.cos(li * dt)
    ai = mag * jnp.sin(li * dt)
    nr, ni = ar - 1.0, ai
    den = lr * lr + li * li
    fr = (nr * lr + ni * li) / den
    fi = (ni * lr - nr * li) / den
    br, bi = br_ref[...], bi_ref[...]
    ar_ref[...] = ar
    ai_ref[...] = ai
    bbr_ref[...] = fr * br - fi * bi
    bbi_ref[...] = fr * bi + fi * br


def _s5_discretize(lam_re, lam_im, log_dt, b_re, b_im):
    g, p, c = b_re.shape
    shp3 = jax.ShapeDtypeStruct((g, 1, p), F32)
    shpb = jax.ShapeDtypeStruct((g, c, p), F32)
    return pl.pallas_call(
        _s5_disc_kernel,
        out_shape=[shp3, shp3, shpb, shpb],
        name="s5_discretize",
    )(lam_re.reshape(g, 1, p), lam_im.reshape(g, 1, p), log_dt.reshape(g, 1, 1),
      jnp.swapaxes(b_re, 1, 2), jnp.swapaxes(b_im, 1, 2))


def _block_diag(w):
    g, a, b = w.shape
    t = jnp.tile(w.reshape(g // S5_GB, S5_GB * a, b), (1, 1, S5_GB))
    row_group = lax.broadcasted_iota(jnp.int32, t.shape, 1) // a
    col_group = lax.broadcasted_iota(jnp.int32, t.shape, 2) // b
    return jnp.where(row_group == col_group, t, 0.0)


def _s5_kernel(u_ref, x0r_ref, x0i_ref, ar_ref, ai_ref, b_ref, c_ref, d_ref,
               z_ref, xfr_ref, xfi_ref, s_scr, *, nb, seq, tc, pitch):
    npair = S5_SLABS // 4
    bw = b_ref[...].astype(BF16)
    cw = c_ref[...].astype(BF16)
    dvec = d_ref[...]
    whole = tc == seq
    spans = [(0, nb * seq)] if whole else [(b * pitch, tc) for b in range(nb)]

    def lanes(s):
        return slice(s * LANES, (s + 1) * LANES)

    def paired(ref, p, rows):
        return jnp.concatenate([jnp.broadcast_to(ref[:, lanes(p + hf * npair)], (rows, LANES))
                                for hf in range(2)], axis=0)

    def place(s):
        comp, ls = divmod(s, 2 * npair)
        hf, p = divmod(ls, npair)
        return comp * npair + p, hf * nb * pitch

    ar = [paired(ar_ref, p, nb) for p in range(npair)]
    ai = [paired(ai_ref, p, nb) for p in range(npair)]

    def u_rows(t0, idx):
        if whole:
            return u_ref[...]
        return u_ref[pl.ds(pl.multiple_of(idx * seq + t0, tc), tc), :]

    def chunk(t0, xs):
        for idx, (r0, nrows) in enumerate(spans):
            bu = _dot(u_rows(t0, idx).astype(BF16), bw)
            for s in range(S5_SLABS):
                q, off = place(s)
                s_scr[q, off + r0:off + r0 + nrows, :] = bu[:, lanes(s)]

        def step(t, xs):
            rows = pl.ds(t, 2 * nb, stride=pitch)
            new = [None] * (2 * npair)
            for p in range(npair):
                xr, xi = xs[p], xs[npair + p]
                nr = ar[p] * xr - ai[p] * xi + s_scr[p, rows, :]
                ni = ar[p] * xi + ai[p] * xr + s_scr[npair + p, rows, :]
                s_scr[p, rows, :] = nr
                s_scr[npair + p, rows, :] = ni
                new[p], new[npair + p] = nr, ni
            return tuple(new)

        xs = lax.fori_loop(0, tc, step, xs, unroll=8 if nb <= 8 else 1)
        for idx, (r0, nrows) in enumerate(spans):
            parts = []
            for s in range(S5_SLABS):
                q, off = place(s)
                parts.append(s_scr[q, off + r0:off + r0 + nrows, :])
            u = u_rows(t0, idx)
            y = _dot(jnp.concatenate(parts, axis=1).astype(BF16), cw) + dvec * u
            z = jax.nn.gelu(y)
            if whole:
                z_ref[...] = z
            else:
                z_ref[pl.ds(pl.multiple_of(idx * seq + t0, tc), tc), :] = z
        return xs

    xs0 = tuple([paired(x0r_ref, p, nb) for p in range(npair)]
                + [paired(x0i_ref, p, nb) for p in range(npair)])
    if whole:
        xs = chunk(0, xs0)
    else:
        xs = lax.fori_loop(0, seq // tc, lambda c, xs: chunk(c * tc, xs), xs0)
    for p in range(npair):
        for hf in range(2):
            xfr_ref[:, lanes(p + hf * npair)] = xs[p][hf * nb:(hf + 1) * nb]
            xfi_ref[:, lanes(p + hf * npair)] = xs[npair + p][hf * nb:(hf + 1) * nb]


def _s5(proj, row_block0, nb, seq, tc, x0r, x0i, ar, ai, bblk, cblk, d):
    rows = nb * seq
    pitch = seq if tc == seq else tc + 8
    u_col0 = 4 * RET_WIDTH // S5_CH
    nj = S5_GROUPS // S5_GB
    st = pl.BlockSpec((nb, S5_ST), lambda j: (0, j))
    coef = pl.BlockSpec((1, S5_ST), lambda j: (0, j))
    return pl.pallas_call(
        functools.partial(_s5_kernel, nb=nb, seq=seq, tc=tc, pitch=pitch),
        grid=(nj,),
        in_specs=[pl.BlockSpec((rows, S5_CH), lambda j: (row_block0, u_col0 + j)),
                  st, st, coef, coef,
                  pl.BlockSpec((None, S5_CH, 2 * S5_ST), lambda j: (j, 0, 0)),
                  pl.BlockSpec((None, 2 * S5_ST, S5_CH), lambda j: (j, 0, 0)),
                  pl.BlockSpec((1, S5_CH), lambda j: (0, j))],
        out_specs=[pl.BlockSpec((rows, S5_CH), lambda j: (0, j)), st, st],
        out_shape=[jax.ShapeDtypeStruct((rows, S5_WIDTH), F32),
                   jax.ShapeDtypeStruct((nb, S5_NSTATE), F32),
                   jax.ShapeDtypeStruct((nb, S5_NSTATE), F32)],
        scratch_shapes=[pltpu.VMEM((S5_SLABS // 2, 2 * nb * pitch, LANES), F32)],
        compiler_params=_params("parallel"),
        name="s5_scan",
    )(proj, x0r, x0i, ar, ai, bblk, cblk, d.reshape(1, S5_WIDTH))


def _s5_post_kernel(zp_ref, zs_ref, zpc_ref, zsc_ref, w_ref, b_ref, g_ref, o_ref,
                    zb_scr, gate_scr, ssq_scr, *, n_first, tn):
    i, j = pl.program_id(0), pl.program_id(1)
    first = i < n_first

    @pl.when(j == 0)
    def _():
        zb_scr[...] = jnp.where(first, zp_ref[...], zs_ref[...]).astype(BF16)
        ssq_scr[...] = jnp.zeros_like(ssq_scr)

    zc = jnp.where(first, zpc_ref[...], zsc_ref[...])
    t = _dot(zb_scr[...], w_ref[...].astype(BF16)) + b_ref[...]
    s = zc * jax.nn.sigmoid(t)
    gate_scr[j] = s
    ssq_scr[...] += jnp.sum(s * s, axis=-1, keepdims=True)

    @pl.when(j == pl.num_programs(1) - 1)
    def _():
        r = lax.rsqrt(ssq_scr[...] / (tn * gate_scr.shape[0]) + EPS)
        for jj in range(gate_scr.shape[0]):
            cols = slice(jj * tn, (jj + 1) * tn)
            o_ref[:, cols] = (gate_scr[jj] * r * g_ref[:, cols]).astype(o_ref.dtype)


def _s5_post(zp, zs, w_glu, b_glu, norm_g, tm, tn):
    d = zp.shape[1]
    n_first = zp.shape[0] // tm
    nblk = n_first + zs.shape[0] // tm
    nj = d // tn

    def first_row(i):
        return jnp.minimum(i, n_first - 1)

    def second_row(i):
        return jnp.maximum(i - n_first, 0)

    return pl.pallas_call(
        functools.partial(_s5_post_kernel, n_first=n_first, tn=tn),
        grid=(nblk, nj),
        in_specs=[pl.BlockSpec((tm, d), lambda i, j: (first_row(i), 0)),
                  pl.BlockSpec((tm, d), lambda i, j: (second_row(i), 0)),
                  pl.BlockSpec((tm, tn), lambda i, j: (first_row(i), jnp.where(i < n_first, j, nj - 1))),
                  pl.BlockSpec((tm, tn), lambda i, j: (second_row(i), jnp.where(i < n_first, 0, j))),
                  pl.BlockSpec((d, tn), lambda i, j: (0, j)),
                  pl.BlockSpec((1, tn), lambda i, j: (0, j)),
                  pl.BlockSpec((1, d), lambda i, j: (0, 0))],
        out_specs=pl.BlockSpec((tm, d), lambda i, j: (i, 0)),
        out_shape=jax.ShapeDtypeStruct((nblk * tm, d), BF16),
        scratch_shapes=[pltpu.VMEM((tm, d), BF16), pltpu.VMEM((nj, tm, tn), F32),
                        pltpu.VMEM((tm, 1), F32)],
        compiler_params=_params("parallel", "arbitrary"),
        name="s5_post",
    )(zp, zs, zp, zs, w_glu, b_glu.reshape(1, d), norm_g.reshape(1, d))


def kernel(x_prompt, x_sample, state_ret, state_s5_re, state_s5_im, meta_tokens, norm1_g, w_in, ret_gn_g, s5_lam_re, s5_lam_im, s5_log_dt, s5_b_re, s5_b_im, s5_c_re, s5_c_im, s5_d, w_glu, b_glu, s5_norm_g, w_out, norm2_g, w_gate, w_up, w_down, final_norm_g):
    assert norm1_g.shape[0] == 1, "single-layer model"
    batch, seq, d = x_prompt.shape
    dbatch, dseq, _ = x_sample.shape
    rows_p, rows_s = batch * seq, dbatch * dseq
    tm = 1024
    assert rows_p % tm == 0 and rows_s % tm == 0 and seq % RET_CHUNK == 0
    xp = x_prompt.reshape(rows_p, d)
    xs = x_sample.reshape(rows_s, d)
    hd = RET_HEAD_DIM

    ar, ai, bbr, bbi = _s5_discretize(s5_lam_re[0], s5_lam_im[0], s5_log_dt[0], s5_b_re[0], s5_b_im[0])
    ar = ar.reshape(1, S5_NSTATE)
    ai = ai.reshape(1, S5_NSTATE)
    bblk = jnp.concatenate([_block_diag(bbr), _block_diag(bbi)], axis=2)
    cblk = jnp.concatenate([_block_diag(jnp.swapaxes(s5_c_re[0], 1, 2)),
                            _block_diag(-jnp.swapaxes(s5_c_im[0], 1, 2))], axis=1)

    h_meta = _rmsnorm([meta_tokens], norm1_g[0], N_META, BF16)
    proj_meta = _inproj(h_meta, w_in[0], N_META, 512)
    zero_ret = jnp.zeros((1, RET_HEADS, hd, hd), F32)
    _, s_meta = _retention_seq(proj_meta, 0, 1, N_META, N_META, 0, zero_ret, ret_gn_g[0])
    zero_s5 = jnp.zeros((1, S5_NSTATE), F32)
    _, mr, mi = _s5(proj_meta, 0, 1, N_META, N_META, zero_s5, zero_s5, ar, ai, bblk, cblk, s5_d[0])

    h = _rmsnorm([xp, xs], norm1_g[0], 512, BF16)
    proj = _inproj(h, w_in[0], tm, 512)

    ret_p, sret_p = _retention_seq(proj, 0, batch, seq, RET_CHUNK, N_META, s_meta, ret_gn_g[0])
    ret_s, sret_s = _retention_step(proj, rows_p, dbatch, dseq, PAST_LEN, state_ret[0],
                                    ret_gn_g[0], 2)

    z_p, s5r_p, s5i_p = _s5(proj, 0, batch, seq, 256,
                            jnp.broadcast_to(mr, (batch, S5_NSTATE)),
                            jnp.broadcast_to(mi, (batch, S5_NSTATE)),
                            ar, ai, bblk, cblk, s5_d[0])
    z_s, s5r_s, s5i_s = _s5(proj, rows_p // rows_s, dbatch, dseq, dseq,
                            state_s5_re[0].reshape(dbatch, S5_NSTATE),
                            state_s5_im[0].reshape(dbatch, S5_NSTATE),
                            ar, ai, bblk, cblk, s5_d[0])
    s5_out = _s5_post(z_p, z_s, w_glu[0], b_glu[0], s5_norm_g[0], 512, 512)

    x1 = _outproj(ret_p, ret_s, s5_out, w_out[0], xp, xs, tm, 256)
    h2 = _rmsnorm([x1], norm2_g[0], 512, BF16)
    a = _ffn_up(h2, w_gate[0], w_up[0], tm, 256)
    x2 = _ffn_down(a, w_down[0], x1, 0, 2, tm, 256)
    x2 = _ffn_down(a, w_down[0], x2, 1, 2, tm, 256)
    y_p, y_s = _final_norm(x2, final_norm_g, rows_p, 512)

    st = (1, -1, S5_GROUPS, S5_STATE)
    return (y_p.reshape(batch, seq, d), y_s.reshape(dbatch, dseq, d),
            sret_p[None], s5r_p.reshape(st), s5i_p.reshape(st),
            sret_s[None], s5r_s.reshape(st), s5i_s.reshape(st))
```

```python
import functools
import math

import numpy as np
import jax
import jax.numpy as jnp
from jax import lax
from jax.experimental import pallas as pl
from jax.experimental.pallas import tpu as pltpu

D_MODEL = 4096
N_META = 16
PAST_LEN = 16384
RET_WIDTH = D_MODEL // 2
S5_WIDTH = D_MODEL - RET_WIDTH
RET_HEADS = 8
RET_HEAD_DIM = RET_WIDTH // RET_HEADS
RET_CHUNK = 128
ROPE_BASE = 10000.0
S5_GROUP = 16
S5_GROUPS = S5_WIDTH // S5_GROUP
S5_STATE = 64
S5_NSTATE = S5_GROUPS * S5_STATE
IN_COLS = 4 * RET_WIDTH + S5_WIDTH
EPS = 1e-6
GN_EPS = 1e-5

LANES = 128
S5_GB = 16
S5_CH = S5_GB * S5_GROUP
S5_ST = S5_GB * S5_STATE
S5_SLABS = 2 * S5_ST // LANES
VMEM_LIMIT = 56 * 1024 * 1024

BF16 = jnp.bfloat16
F32 = jnp.float32


def _params(*sem):
    return pltpu.CompilerParams(dimension_semantics=sem, vmem_limit_bytes=VMEM_LIMIT)


def _dot(a, b):
    return jnp.dot(a, b, preferred_element_type=F32)


def _rmsnorm_kernel(*refs, n_src, n_first):
    x_refs, g_ref, o_ref = refs[:n_src], refs[n_src], refs[n_src + 1]
    x = x_refs[0][...]
    if n_src == 2:
        x = jnp.where(pl.program_id(0) < n_first, x, x_refs[1][...])
    r = lax.rsqrt(jnp.mean(x * x, axis=-1, keepdims=True) + EPS)
    o_ref[...] = (x * r * g_ref[...]).astype(o_ref.dtype)


def _rmsnorm(srcs, g, tm, out_dtype):
    d = srcs[0].shape[1]
    nblk = [s.shape[0] // tm for s in srcs]
    n_first = nblk[0]
    specs = [pl.BlockSpec((tm, d), lambda i: (jnp.minimum(i, n_first - 1), 0))]
    if len(srcs) == 2:
        specs.append(pl.BlockSpec((tm, d), lambda i: (jnp.maximum(i - n_first, 0), 0)))
    rows = sum(s.shape[0] for s in srcs)
    return pl.pallas_call(
        functools.partial(_rmsnorm_kernel, n_src=len(srcs), n_first=n_first),
        grid=(sum(nblk),),
        in_specs=specs + [pl.BlockSpec((1, d), lambda i: (0, 0))],
        out_specs=pl.BlockSpec((tm, d), lambda i: (i, 0)),
        out_shape=jax.ShapeDtypeStruct((rows, d), out_dtype),
        compiler_params=_params("parallel"),
        name="rmsnorm",
    )(*srcs, g.reshape(1, d))


def _final_norm_kernel(x_ref, g_ref, yp_ref, ys_ref, *, n_first):
    x = x_ref[...]
    r = lax.rsqrt(jnp.mean(x * x, axis=-1, keepdims=True) + EPS)
    y = x * r * g_ref[...]
    i = pl.program_id(0)

    @pl.when(i < n_first)
    def _():
        yp_ref[...] = y

    @pl.when(i >= n_first)
    def _():
        ys_ref[...] = y


def _final_norm(x, g, rows_p, tm):
    rows, d = x.shape
    n_first = rows_p // tm
    return pl.pallas_call(
        functools.partial(_final_norm_kernel, n_first=n_first),
        grid=(rows // tm,),
        in_specs=[pl.BlockSpec((tm, d), lambda i: (i, 0)),
                  pl.BlockSpec((1, d), lambda i: (0, 0))],
        out_specs=[pl.BlockSpec((tm, d), lambda i: (jnp.minimum(i, n_first - 1), 0)),
                   pl.BlockSpec((tm, d), lambda i: (jnp.maximum(i - n_first, 0), 0))],
        out_shape=[jax.ShapeDtypeStruct((rows_p, d), F32),
                   jax.ShapeDtypeStruct((rows - rows_p, d), F32)],
        compiler_params=_params("arbitrary"),
        name="final_norm",
    )(x, g.reshape(1, d))


def _inproj_kernel(h_ref, w_ref, o_ref):
    o_ref[...] = _dot(h_ref[...], w_ref[...].astype(BF16))


def _inproj(h, w, tm, tn):
    m, k = h.shape
    n = w.shape[1]
    return pl.pallas_call(
        _inproj_kernel,
        grid=(m // tm, n // tn),
        in_specs=[pl.BlockSpec((tm, k), lambda i, j: (i, 0)),
                  pl.BlockSpec((k, tn), lambda i, j: (0, j))],
        out_specs=pl.BlockSpec((tm, tn), lambda i, j: (i, j)),
        out_shape=jax.ShapeDtypeStruct((m, n), F32),
        compiler_params=_params("parallel", "arbitrary"),
        name="inproj",
    )(h, w)


def _outproj_kernel(retp_ref, rets_ref, s5_ref, wt_ref, wb_ref, xp_ref, xs_ref, g_ref,
                    o_ref, xg_ref, ssq_ref, *, n_first):
    first = pl.program_id(0) < n_first
    j = pl.program_id(1)
    x = jnp.where(first, xp_ref[...], xs_ref[...])
    ret = jnp.where(first, retp_ref[...], rets_ref[...])
    acc = _dot(ret, wt_ref[...].astype(BF16))
    acc += _dot(s5_ref[...], wb_ref[...].astype(BF16))
    x1 = x + acc
    o_ref[...] = x1
    xg_ref[...] = (x1 * g_ref[...]).astype(xg_ref.dtype)
    sq = x1 * x1
    part = sq[:, :LANES]
    for c in range(1, sq.shape[1] // LANES):
        part += sq[:, c * LANES:(c + 1) * LANES]

    @pl.when(j == 0)
    def _():
        ssq_ref[...] = part

    @pl.when(j != 0)
    def _():
        ssq_ref[...] += part


def _outproj(ret_p, ret_s, s5, w_out, xp, xs, g, tm, tn):
    m, kh = s5.shape
    n = w_out.shape[1]
    n_first = xp.shape[0] // tm
    nj = n // tn
    return pl.pallas_call(
        functools.partial(_outproj_kernel, n_first=n_first),
        grid=(m // tm, nj),
        in_specs=[pl.BlockSpec((tm, kh), lambda i, j: (jnp.minimum(i, n_first - 1), 0)),
                  pl.BlockSpec((tm, kh), lambda i, j: (jnp.maximum(i - n_first, 0), 0),
                               pipeline_mode=pl.Buffered(1)),
                  pl.BlockSpec((tm, kh), lambda i, j: (i, 0)),
                  pl.BlockSpec((kh, tn), lambda i, j: (0, j)),
                  pl.BlockSpec((kh, tn), lambda i, j: (1, j)),
                  pl.BlockSpec((tm, tn), lambda i, j: (jnp.minimum(i, n_first - 1),
                                                       jnp.where(i < n_first, j, nj - 1))),
                  pl.BlockSpec((tm, tn), lambda i, j: (jnp.maximum(i - n_first, 0),
                                                       jnp.where(i < n_first, 0, j)),
                               pipeline_mode=pl.Buffered(1)),
                  pl.BlockSpec((1, tn), lambda i, j: (0, j))],
        out_specs=[pl.BlockSpec((tm, tn), lambda i, j: (i, j)),
                   pl.BlockSpec((tm, tn), lambda i, j: (i, j)),
                   pl.BlockSpec((tm, LANES), lambda i, j: (i, 0))],
        out_shape=[jax.ShapeDtypeStruct((m, n), F32),
                   jax.ShapeDtypeStruct((m, n), BF16),
                   jax.ShapeDtypeStruct((m, LANES), F32)],
        compiler_params=_params("parallel", "arbitrary"),
        name="outproj",
    )(ret_p, ret_s, s5, w_out, w_out, xp, xs, g.reshape(1, n))


def _ffn_up_kernel(xg_ref, ssq_ref, wg_ref, wu_ref, o_ref, r_scr):
    tf = wg_ref.shape[1]

    @pl.when(pl.program_id(1) == 0)
    def _():
        r_scr[...] = lax.rsqrt(jnp.sum(ssq_ref[...], axis=-1, keepdims=True) / xg_ref.shape[1] + EPS)

    w = jnp.concatenate([wg_ref[...].astype(BF16), wu_ref[...].astype(BF16)], axis=1)
    gu = _dot(xg_ref[...], w) * r_scr[...]
    o_ref[...] = (jax.nn.silu(gu[:, :tf]) * gu[:, tf:]).astype(o_ref.dtype)


def _ffn_up(xg, ssq, w_gate, w_up, tm, tf):
    m, k = xg.shape
    f = w_gate.shape[1]
    return pl.pallas_call(
        _ffn_up_kernel,
        grid=(m // tm, f // tf),
        in_specs=[pl.BlockSpec((tm, k), lambda i, j: (i, 0)),
                  pl.BlockSpec((tm, LANES), lambda i, j: (i, 0)),
                  pl.BlockSpec((k, tf), lambda i, j: (0, j)),
                  pl.BlockSpec((k, tf), lambda i, j: (0, j))],
        out_specs=pl.BlockSpec((tm, tf), lambda i, j: (i, j)),
        out_shape=jax.ShapeDtypeStruct((m, f), BF16),
        scratch_shapes=[pltpu.VMEM((tm, 1), F32)],
        compiler_params=_params("parallel", "arbitrary"),
        name="ffn_up",
    )(xg, ssq, w_gate, w_up)


def _ffn_down_kernel(a_ref, w_ref, x_ref, o_ref):
    o_ref[...] = x_ref[...] + _dot(a_ref[...], w_ref[...].astype(BF16))


def _ffn_down(a, w_down, x, kblock, nkblocks, tm, tn):
    m, f = a.shape
    n = w_down.shape[1]
    tk = f // nkblocks
    return pl.pallas_call(
        _ffn_down_kernel,
        grid=(m // tm, n // tn),
        in_specs=[pl.BlockSpec((tm, tk), lambda i, j: (i, kblock)),
                  pl.BlockSpec((tk, tn), lambda i, j: (kblock, j)),
                  pl.BlockSpec((tm, tn), lambda i, j: (i, j))],
        out_specs=pl.BlockSpec((tm, tn), lambda i, j: (i, j)),
        out_shape=jax.ShapeDtypeStruct((m, n), F32),
        compiler_params=_params("parallel", "arbitrary"),
        name="ffn_down",
    )(a, w_down, x)


def _rotary(x, cos, sin):
    half = RET_HEAD_DIM // 2
    x1, x2 = x[:, :half], x[:, half:]
    return jnp.concatenate([x1 * cos - x2 * sin, x1 * sin + x2 * cos], axis=-1)


def _retention_chunk(lg, sdec, s, q, k, v, g, cos, sin, gn):
    n = q.shape[0]
    ri = lax.broadcasted_iota(jnp.int32, (n, n), 0)
    ci = lax.broadcasted_iota(jnp.int32, (n, n), 1)
    diff = (ri - ci).astype(F32)
    mask = jnp.where(diff >= 0, jnp.exp(jnp.maximum(diff, 0.0) * lg), 0.0)
    row = lax.broadcasted_iota(jnp.int32, (n, 1), 0).astype(F32)
    q_dec = jnp.exp(lg * (row + 1.0))
    k_dec = jnp.exp(lg * (n - 1.0 - row))

    qr = _rotary(q, cos, sin)
    kr = _rotary(k, cos, sin) * (RET_HEAD_DIM ** -0.5)
    vb = v.astype(BF16)
    scores = lax.dot_general(qr.astype(BF16), kr.astype(BF16), (((1,), (1,)), ((), ())),
                             preferred_element_type=F32) * mask
    o = _dot(scores.astype(BF16), vb) + _dot((qr * q_dec).astype(BF16), s.astype(BF16))
    s_new = sdec * s + lax.dot_general((kr * k_dec).astype(BF16), vb, (((0,), (0,)), ((), ())),
                                       preferred_element_type=F32)
    mu = jnp.mean(o, axis=-1, keepdims=True)
    oc = o - mu
    var = jnp.mean(oc * oc, axis=-1, keepdims=True)
    y = oc * lax.rsqrt(var + GN_EPS) * gn
    return y * jax.nn.silu(g), s_new


def _ret_seq_kernel(dec_ref, q_ref, k_ref, v_ref, g_ref, cos_ref, sin_ref, s0_ref, gn_ref,
                    o_ref, sf_ref, s_scr, *, chunk, nchunks):
    h = pl.program_id(1)
    lg = dec_ref[0, h]
    sdec = dec_ref[1, h]
    s_scr[...] = s0_ref[...]
    gn = gn_ref[...]

    def body(c, carry):
        rows = pl.ds(pl.multiple_of(c * chunk, chunk), chunk)
        y, s_new = _retention_chunk(lg, sdec, s_scr[...], q_ref[rows, :], k_ref[rows, :],
                                    v_ref[rows, :], g_ref[rows, :], cos_ref[rows, :],
                                    sin_ref[rows, :], gn)
        o_ref[rows, :] = y.astype(o_ref.dtype)
        s_scr[...] = s_new
        return carry

    lax.fori_loop(0, nchunks, body, 0, unroll=min(4, nchunks))
    sf_ref[...] = s_scr[...]


def _ret_decay_consts(chunk):
    lg = np.log(1.0 - 2.0 ** (-5.0 - np.arange(RET_HEADS, dtype=np.float64)))
    return jnp.asarray(np.stack([lg, np.exp(lg * chunk)]), dtype=F32)


def _rope_tables(pos):
    half = RET_HEAD_DIM // 2
    inv = ROPE_BASE ** (-np.arange(half, dtype=np.float64) / half)
    ang = np.asarray(pos, dtype=np.float64)[:, None] * inv[None, :]
    return jnp.asarray(np.cos(ang), dtype=F32), jnp.asarray(np.sin(ang), dtype=F32)


def _retention_seq(proj, row_block0, nbatch, seq, chunk, pos0, s0, gn_g):
    cos, sin = _rope_tables(pos0 + np.arange(seq))
    hd = RET_HEAD_DIM

    def col(off):
        return pl.BlockSpec((seq, hd), lambda b, h: (row_block0 + b, off + h))

    tbl = pl.BlockSpec((seq, hd // 2), lambda b, h: (0, 0))
    return pl.pallas_call(
        functools.partial(_ret_seq_kernel, chunk=chunk, nchunks=seq // chunk),
        grid=(nbatch, RET_HEADS),
        in_specs=[pl.BlockSpec(memory_space=pltpu.SMEM),
                  col(0), col(RET_HEADS), col(2 * RET_HEADS), col(3 * RET_HEADS), tbl, tbl,
                  pl.BlockSpec((None, None, hd, hd), lambda b, h: (0, h, 0, 0)),
                  pl.BlockSpec((1, hd), lambda b, h: (0, h))],
        out_specs=[pl.BlockSpec((seq, hd), lambda b, h: (b, h)),
                   pl.BlockSpec((None, None, hd, hd), lambda b, h: (b, h, 0, 0))],
        out_shape=[jax.ShapeDtypeStruct((nbatch * seq, RET_WIDTH), BF16),
                   jax.ShapeDtypeStruct((nbatch, RET_HEADS, hd, hd), F32)],
        scratch_shapes=[pltpu.VMEM((hd, hd), F32)],
        compiler_params=_params("parallel", "arbitrary"),
        name="retention_seq",
    )(_ret_decay_consts(chunk), proj, proj, proj, proj, cos, sin, s0, gn_g.reshape(1, RET_WIDTH))


def _ret_step_kernel(dec_ref, q_ref, k_ref, v_ref, g_ref, cos_ref, sin_ref, s_ref, gn_ref,
                     o_ref, sn_ref, *, nb, seq):
    hd = RET_HEAD_DIM
    cos, sin = cos_ref[...], sin_ref[...]
    for b in range(nb):
        rows = slice(b * seq, (b + 1) * seq)
        for h in range(RET_HEADS):
            cols = slice(h * hd, (h + 1) * hd)
            y, s_new = _retention_chunk(dec_ref[0, h], dec_ref[1, h], s_ref[b, h],
                                        q_ref[rows, cols], k_ref[rows, cols], v_ref[rows, cols],
                                        g_ref[rows, cols], cos, sin, gn_ref[:, cols])
            o_ref[rows, cols] = y.astype(o_ref.dtype)
            sn_ref[b, h] = s_new


def _retention_step(proj, row0, nbatch, seq, pos0, state, gn_g, nb):
    cos, sin = _rope_tables(pos0 + np.arange(seq))
    hd = RET_HEAD_DIM
    rb0 = row0 // (nb * seq)

    def col(off):
        return pl.BlockSpec((nb * seq, RET_WIDTH), lambda i: (rb0 + i, off))

    tbl = pl.BlockSpec((seq, hd // 2), lambda i: (0, 0))
    st = pl.BlockSpec((nb, RET_HEADS, hd, hd), lambda i: (i, 0, 0, 0))
    return pl.pallas_call(
        functools.partial(_ret_step_kernel, nb=nb, seq=seq),
        grid=(nbatch // nb,),
        in_specs=[pl.BlockSpec(memory_space=pltpu.SMEM),
                  col(0), col(1), col(2), col(3), tbl, tbl, st,
                  pl.BlockSpec((1, RET_WIDTH), lambda i: (0, 0))],
        out_specs=[pl.BlockSpec((nb * seq, RET_WIDTH), lambda i: (i, 0)), st],
        out_shape=[jax.ShapeDtypeStruct((nbatch * seq, RET_WIDTH), BF16),
                   jax.ShapeDtypeStruct(state.shape, F32)],
        compiler_params=_params("parallel"),
        name="retention_step",
    )(_ret_decay_consts(seq), proj, proj, proj, proj, cos, sin, state,
      gn_g.reshape(1, RET_WIDTH))


def _s5_disc_kernel(lr_ref, li_ref, ldt_ref, br_ref, bi_ref, ar_ref, ai_ref, bbr_ref, bbi_ref):
    lr, li = lr_ref[...], li_ref[...]
    dt = jnp.exp(ldt_ref[...])
    mag = jnp.exp(lr * dt)
    ar = mag * jnp.cos(li * dt)
    ai = mag * jnp.sin(li * dt)
    nr, ni = ar - 1.0, ai
    den = lr * lr + li * li
    fr = (nr * lr + ni * li) / den
    fi = (ni * lr - nr * li) / den
    br, bi = br_ref[...], bi_ref[...]
    ar_ref[...] = ar
    ai_ref[...] = ai
    bbr_ref[...] = fr * br - fi * bi
    bbi_ref[...] = fr * bi + fi * br


def _s5_discretize(lam_re, lam_im, log_dt, b_re, b_im):
    g, p, c = b_re.shape
    shp3 = jax.ShapeDtypeStruct((g, 1, p), F32)
    shpb = jax.ShapeDtypeStruct((g, c, p), F32)
    return pl.pallas_call(
        _s5_disc_kernel,
        out_shape=[shp3, shp3, shpb, shpb],
        name="s5_discretize",
    )(lam_re.reshape(g, 1, p), lam_im.reshape(g, 1, p), log_dt.reshape(g, 1, 1),
      jnp.swapaxes(b_re, 1, 2), jnp.swapaxes(b_im, 1, 2))


def _s5_kernel(u_ref, x0r_ref, x0i_ref, ar_ref, ai_ref, bbr_ref, bbi_ref, cr_ref, ci_ref, d_ref,
               z_ref, xfr_ref, xfi_ref, s_scr, bw_scr, cw_scr, *, nb, seq, tc, pitch):
    npair = S5_SLABS // 4
    bw_scr[...] = jnp.zeros_like(bw_scr)
    cw_scr[...] = jnp.zeros_like(cw_scr)
    for g in range(S5_GB):
        ch = slice(g * S5_GROUP, (g + 1) * S5_GROUP)
        for comp, (b_ref, c_ref, sign) in enumerate(((bbr_ref, cr_ref, 1.0), (bbi_ref, ci_ref, -1.0))):
            st = slice(comp * S5_ST + g * S5_STATE, comp * S5_ST + (g + 1) * S5_STATE)
            bw_scr[ch, st] = b_ref[g]
            cw_scr[st, ch] = sign * c_ref[g]
    bw = bw_scr[...].astype(BF16)
    cw = cw_scr[...].astype(BF16)
    dvec = d_ref[...]
    whole = tc == seq
    spans = [(0, nb * seq)] if whole else [(b * pitch, tc) for b in range(nb)]

    def lanes(s):
        return slice(s * LANES, (s + 1) * LANES)

    def paired(ref, p, rows):
        return jnp.concatenate([jnp.broadcast_to(ref[:, lanes(p + hf * npair)], (rows, LANES))
                                for hf in range(2)], axis=0)

    def place(s):
        comp, ls = divmod(s, 2 * npair)
        hf, p = divmod(ls, npair)
        return comp * npair + p, hf * nb * pitch

    ar = [paired(ar_ref, p, nb) for p in range(npair)]
    ai = [paired(ai_ref, p, nb) for p in range(npair)]

    def u_rows(t0, idx):
        if whole:
            return u_ref[...]
        return u_ref[pl.ds(pl.multiple_of(idx * seq + t0, tc), tc), :]

    def chunk(t0, xs):
        for idx, (r0, nrows) in enumerate(spans):
            bu = _dot(u_rows(t0, idx).astype(BF16), bw)
            for s in range(S5_SLABS):
                q, off = place(s)
                s_scr[q, off + r0:off + r0 + nrows, :] = bu[:, lanes(s)]

        def step(t, xs):
            rows = pl.ds(t, 2 * nb, stride=pitch)
            new = [None] * (2 * npair)
            for p in range(npair):
                xr, xi = xs[p], xs[npair + p]
                nr = ar[p] * xr - ai[p] * xi + s_scr[p, rows, :]
                ni = ar[p] * xi + ai[p] * xr + s_scr[npair + p, rows, :]
                s_scr[p, rows, :] = nr
                s_scr[npair + p, rows, :] = ni
                new[p], new[npair + p] = nr, ni
            return tuple(new)

        xs = lax.fori_loop(0, tc, step, xs, unroll=8 if nb <= 8 else 1)
        for idx, (r0, nrows) in enumerate(spans):
            parts = []
            for s in range(S5_SLABS):
                q, off = place(s)
                parts.append(s_scr[q, off + r0:off + r0 + nrows, :])
            u = u_rows(t0, idx)
            y = _dot(jnp.concatenate(parts, axis=1).astype(BF16), cw) + dvec * u
            z = jax.nn.gelu(y).astype(z_ref.dtype)
            if whole:
                z_ref[...] = z
            else:
                z_ref[pl.ds(pl.multiple_of(idx * seq + t0, tc), tc), :] = z
        return xs

    xs0 = tuple([paired(x0r_ref, p, nb) for p in range(npair)]
                + [paired(x0i_ref, p, nb) for p in range(npair)])
    if whole:
        xs = chunk(0, xs0)
    else:
        xs = lax.fori_loop(0, seq // tc, lambda c, xs: chunk(c * tc, xs), xs0)
    for p in range(npair):
        for hf in range(2):
            xfr_ref[:, lanes(p + hf * npair)] = xs[p][hf * nb:(hf + 1) * nb]
            xfi_ref[:, lanes(p + hf * npair)] = xs[npair + p][hf * nb:(hf + 1) * nb]


def _s5(proj, row_block0, nb, seq, tc, x0r, x0i, ar, ai, bbr, bbi, c_re, c_im, d):
    rows = nb * seq
    pitch = seq if tc == seq else tc + 8
    u_col0 = 4 * RET_WIDTH // S5_CH
    nj = S5_GROUPS // S5_GB
    st = pl.BlockSpec((nb, S5_ST), lambda j: (0, j))
    coef = pl.BlockSpec((1, S5_ST), lambda j: (0, j))
    bspec = pl.BlockSpec((S5_GB, S5_GROUP, S5_STATE), lambda j: (j, 0, 0))
    cspec = pl.BlockSpec((S5_GB, S5_STATE, S5_GROUP), lambda j: (j, 0, 0))
    return pl.pallas_call(
        functools.partial(_s5_kernel, nb=nb, seq=seq, tc=tc, pitch=pitch),
        grid=(nj,),
        in_specs=[pl.BlockSpec((rows, S5_CH), lambda j: (row_block0, u_col0 + j)),
                  st, st, coef, coef, bspec, bspec, cspec, cspec,
                  pl.BlockSpec((1, S5_CH), lambda j: (0, j))],
        out_specs=[pl.BlockSpec((rows, S5_CH), lambda j: (0, j)), st, st],
        out_shape=[jax.ShapeDtypeStruct((rows, S5_WIDTH), BF16),
                   jax.ShapeDtypeStruct((nb, S5_NSTATE), F32),
                   jax.ShapeDtypeStruct((nb, S5_NSTATE), F32)],
        scratch_shapes=[pltpu.VMEM((S5_SLABS // 2, 2 * nb * pitch, LANES), F32),
                        pltpu.VMEM((S5_CH, 2 * S5_ST), F32),
                        pltpu.VMEM((2 * S5_ST, S5_CH), F32)],
        compiler_params=_params("parallel"),
        name="s5_scan",
    )(proj, x0r, x0i, ar, ai, bbr, bbi, c_re, c_im, d.reshape(1, S5_WIDTH))


def _s5_post_kernel(zp_ref, zs_ref, zpc_ref, zsc_ref, w_ref, b_ref, g_ref, o_ref,
                    zb_scr, gate_scr, ssq_scr, *, n_first, tn):
    i, j = pl.program_id(0), pl.program_id(1)
    first = i < n_first

    @pl.when(j == 0)
    def _():
        zb_scr[...] = jnp.where(first, zp_ref[...], zs_ref[...])
        ssq_scr[...] = jnp.zeros_like(ssq_scr)

    zc = jnp.where(first, zpc_ref[...], zsc_ref[...]).astype(F32)
    t = _dot(zb_scr[...], w_ref[...].astype(BF16)) + b_ref[...]
    s = zc * jax.nn.sigmoid(t)
    gate_scr[j] = s
    ssq_scr[...] += jnp.sum(s * s, axis=-1, keepdims=True)

    @pl.when(j == pl.num_programs(1) - 1)
    def _():
        r = lax.rsqrt(ssq_scr[...] / (tn * gate_scr.shape[0]) + EPS)
        for jj in range(gate_scr.shape[0]):
            cols = slice(jj * tn, (jj + 1) * tn)
            o_ref[:, cols] = (gate_scr[jj] * r * g_ref[:, cols]).astype(o_ref.dtype)


def _s5_post(zp, zs, w_glu, b_glu, norm_g, tm, tn):
    d = zp.shape[1]
    n_first = zp.shape[0] // tm
    nblk = n_first + zs.shape[0] // tm
    nj = d // tn

    def first_row(i):
        return jnp.minimum(i, n_first - 1)

    def second_row(i):
        return jnp.maximum(i - n_first, 0)

    return pl.pallas_call(
        functools.partial(_s5_post_kernel, n_first=n_first, tn=tn),
        grid=(nblk, nj),
        in_specs=[pl.BlockSpec((tm, d), lambda i, j: (first_row(i), 0)),
                  pl.BlockSpec((tm, d), lambda i, j: (second_row(i), 0)),
                  pl.BlockSpec((tm, tn), lambda i, j: (first_row(i), jnp.where(i < n_first, j, nj - 1))),
                  pl.BlockSpec((tm, tn), lambda i, j: (second_row(i), jnp.where(i < n_first, 0, j))),
                  pl.BlockSpec((d, tn), lambda i, j: (0, j)),
                  pl.BlockSpec((1, tn), lambda i, j: (0, j)),
                  pl.BlockSpec((1, d), lambda i, j: (0, 0))],
        out_specs=pl.BlockSpec((tm, d), lambda i, j: (i, 0)),
        out_shape=jax.ShapeDtypeStruct((nblk * tm, d), BF16),
        scratch_shapes=[pltpu.VMEM((tm, d), BF16), pltpu.VMEM((nj, tm, tn), F32),
                        pltpu.VMEM((tm, 1), F32)],
        compiler_params=_params("parallel", "arbitrary"),
        name="s5_post",
    )(zp, zs, zp, zs, w_glu, b_glu.reshape(1, d), norm_g.reshape(1, d))


def kernel(x_prompt, x_sample, state_ret, state_s5_re, state_s5_im, meta_tokens, norm1_g, w_in, ret_gn_g, s5_lam_re, s5_lam_im, s5_log_dt, s5_b_re, s5_b_im, s5_c_re, s5_c_im, s5_d, w_glu, b_glu, s5_norm_g, w_out, norm2_g, w_gate, w_up, w_down, final_norm_g):
    assert norm1_g.shape[0] == 1, "single-layer model"
    batch, seq, d = x_prompt.shape
    dbatch, dseq, _ = x_sample.shape
    rows_p, rows_s = batch * seq, dbatch * dseq
    tm = 1024
    assert rows_p % tm == 0 and rows_s % tm == 0 and seq % RET_CHUNK == 0
    xp = x_prompt.reshape(rows_p, d)
    xs = x_sample.reshape(rows_s, d)
    hd = RET_HEAD_DIM

    ar, ai, bbr, bbi = _s5_discretize(s5_lam_re[0], s5_lam_im[0], s5_log_dt[0], s5_b_re[0], s5_b_im[0])
    ar = ar.reshape(1, S5_NSTATE)
    ai = ai.reshape(1, S5_NSTATE)
    s5_w = (ar, ai, bbr, bbi, jnp.swapaxes(s5_c_re[0], 1, 2), jnp.swapaxes(s5_c_im[0], 1, 2), s5_d[0])

    h_meta = _rmsnorm([meta_tokens], norm1_g[0], N_META, BF16)
    proj_meta = _inproj(h_meta, w_in[0], N_META, 512)
    zero_ret = jnp.zeros((1, RET_HEADS, hd, hd), F32)
    _, s_meta = _retention_seq(proj_meta, 0, 1, N_META, N_META, 0, zero_ret, ret_gn_g[0])
    zero_s5 = jnp.zeros((1, S5_NSTATE), F32)
    _, mr, mi = _s5(proj_meta, 0, 1, N_META, N_META, zero_s5, zero_s5, *s5_w)

    h = _rmsnorm([xp, xs], norm1_g[0], 512, BF16)
    proj = _inproj(h, w_in[0], tm, 512)

    ret_p, sret_p = _retention_seq(proj, 0, batch, seq, RET_CHUNK, N_META, s_meta, ret_gn_g[0])
    ret_s, sret_s = _retention_step(proj, rows_p, dbatch, dseq, PAST_LEN, state_ret[0],
                                    ret_gn_g[0], 2)

    z_p, s5r_p, s5i_p = _s5(proj, 0, batch, seq, 256,
                            jnp.broadcast_to(mr, (batch, S5_NSTATE)),
                            jnp.broadcast_to(mi, (batch, S5_NSTATE)), *s5_w)
    z_s, s5r_s, s5i_s = _s5(proj, rows_p // rows_s, dbatch, dseq, dseq,
                            state_s5_re[0].reshape(dbatch, S5_NSTATE),
                            state_s5_im[0].reshape(dbatch, S5_NSTATE), *s5_w)
    s5_out = _s5_post(z_p, z_s, w_glu[0], b_glu[0], s5_norm_g[0], tm, 512)

    x1, x1g, x1_ssq = _outproj(ret_p, ret_s, s5_out, w_out[0], xp, xs, norm2_g[0], tm, 512)
    a = _ffn_up(x1g, x1_ssq, w_gate[0], w_up[0], tm, 256)
    x2 = _ffn_down(a, w_down[0], x1, 0, 2, tm, 256)
    x2 = _ffn_down(a, w_down[0], x2, 1, 2, tm, 256)
    y_p, y_s = _final_norm(x2, final_norm_g, rows_p, 512)

    st = (1, -1, S5_GROUPS, S5_STATE)
    return (y_p.reshape(batch, seq, d), y_s.reshape(dbatch, dseq, d),
            sret_p[None], s5r_p.reshape(st), s5i_p.reshape(st),
            sret_s[None], s5r_s.reshape(st), s5i_s.reshape(st))
```

```python
import functools
import math

import numpy as np
import jax
import jax.numpy as jnp
from jax import lax
from jax.experimental import pallas as pl
from jax.experimental.pallas import tpu as pltpu

D_MODEL = 4096
N_META = 16
PAST_LEN = 16384
RET_WIDTH = D_MODEL // 2
S5_WIDTH = D_MODEL - RET_WIDTH
RET_HEADS = 8
RET_HEAD_DIM = RET_WIDTH // RET_HEADS
RET_CHUNK = 128
ROPE_BASE = 10000.0
S5_GROUP = 16
S5_GROUPS = S5_WIDTH // S5_GROUP
S5_STATE = 64
S5_NSTATE = S5_GROUPS * S5_STATE
IN_COLS = 4 * RET_WIDTH + S5_WIDTH
EPS = 1e-6
GN_EPS = 1e-5

LANES = 128
S5_GB = 16
S5_CH = S5_GB * S5_GROUP
S5_ST = S5_GB * S5_STATE
S5_SLABS = 2 * S5_ST // LANES
VMEM_LIMIT = 56 * 1024 * 1024

BF16 = jnp.bfloat16
F32 = jnp.float32


def _params(*sem):
    return pltpu.CompilerParams(dimension_semantics=sem, vmem_limit_bytes=VMEM_LIMIT)


def _dot(a, b):
    return jnp.dot(a, b, preferred_element_type=F32)


def _rmsnorm_kernel(*refs, n_src, n_first):
    x_refs, g_ref, o_ref = refs[:n_src], refs[n_src], refs[n_src + 1]
    x = x_refs[0][...]
    if n_src == 2:
        x = jnp.where(pl.program_id(0) < n_first, x, x_refs[1][...])
    r = lax.rsqrt(jnp.mean(x * x, axis=-1, keepdims=True) + EPS)
    o_ref[...] = (x * r * g_ref[...]).astype(o_ref.dtype)


def _rmsnorm(srcs, g, tm, out_dtype):
    d = srcs[0].shape[1]
    nblk = [s.shape[0] // tm for s in srcs]
    n_first = nblk[0]
    specs = [pl.BlockSpec((tm, d), lambda i: (jnp.minimum(i, n_first - 1), 0))]
    if len(srcs) == 2:
        specs.append(pl.BlockSpec((tm, d), lambda i: (jnp.maximum(i - n_first, 0), 0)))
    rows = sum(s.shape[0] for s in srcs)
    return pl.pallas_call(
        functools.partial(_rmsnorm_kernel, n_src=len(srcs), n_first=n_first),
        grid=(sum(nblk),),
        in_specs=specs + [pl.BlockSpec((1, d), lambda i: (0, 0))],
        out_specs=pl.BlockSpec((tm, d), lambda i: (i, 0)),
        out_shape=jax.ShapeDtypeStruct((rows, d), out_dtype),
        compiler_params=_params("parallel"),
        name="rmsnorm",
    )(*srcs, g.reshape(1, d))


def _final_norm_kernel(x_ref, g_ref, yp_ref, ys_ref, *, n_first):
    x = x_ref[...]
    r = lax.rsqrt(jnp.mean(x * x, axis=-1, keepdims=True) + EPS)
    y = x * r * g_ref[...]
    i = pl.program_id(0)

    @pl.when(i < n_first)
    def _():
        yp_ref[...] = y

    @pl.when(i >= n_first)
    def _():
        ys_ref[...] = y


def _final_norm(x, g, rows_p, tm):
    rows, d = x.shape
    n_first = rows_p // tm
    return pl.pallas_call(
        functools.partial(_final_norm_kernel, n_first=n_first),
        grid=(rows // tm,),
        in_specs=[pl.BlockSpec((tm, d), lambda i: (i, 0)),
                  pl.BlockSpec((1, d), lambda i: (0, 0))],
        out_specs=[pl.BlockSpec((tm, d), lambda i: (jnp.minimum(i, n_first - 1), 0)),
                   pl.BlockSpec((tm, d), lambda i: (jnp.maximum(i - n_first, 0), 0))],
        out_shape=[jax.ShapeDtypeStruct((rows_p, d), F32),
                   jax.ShapeDtypeStruct((rows - rows_p, d), F32)],
        compiler_params=_params("arbitrary"),
        name="final_norm",
    )(x, g.reshape(1, d))


def _inproj_kernel(h_ref, w_ref, o_ref):
    o_ref[...] = _dot(h_ref[...], w_ref[...].astype(BF16))


def _inproj(h, w, tm, tn):
    m, k = h.shape
    n = w.shape[1]
    return pl.pallas_call(
        _inproj_kernel,
        grid=(m // tm, n // tn),
        in_specs=[pl.BlockSpec((tm, k), lambda i, j: (i, 0)),
                  pl.BlockSpec((k, tn), lambda i, j: (0, j))],
        out_specs=pl.BlockSpec((tm, tn), lambda i, j: (i, j)),
        out_shape=jax.ShapeDtypeStruct((m, n), F32),
        compiler_params=_params("parallel", "arbitrary"),
        name="inproj",
    )(h, w)


def _outproj_kernel(retp_ref, rets_ref, s5_ref, wt_ref, wb_ref, xp_ref, xs_ref, g_ref,
                    o_ref, xg_ref, ssq_ref, *, n_first):
    first = pl.program_id(0) < n_first
    j = pl.program_id(1)
    x = jnp.where(first, xp_ref[...], xs_ref[...])
    ret = jnp.where(first, retp_ref[...], rets_ref[...])
    acc = _dot(ret, wt_ref[...].astype(BF16))
    acc += _dot(s5_ref[...], wb_ref[...].astype(BF16))
    x1 = x + acc
    o_ref[...] = x1
    xg_ref[...] = (x1 * g_ref[...]).astype(xg_ref.dtype)
    sq = x1 * x1
    part = sq[:, :LANES]
    for c in range(1, sq.shape[1] // LANES):
        part += sq[:, c * LANES:(c + 1) * LANES]

    @pl.when(j == 0)
    def _():
        ssq_ref[...] = part

    @pl.when(j != 0)
    def _():
        ssq_ref[...] += part


def _outproj(ret_p, ret_s, s5, w_out, xp, xs, g, tm, tn):
    m, kh = s5.shape
    n = w_out.shape[1]
    n_first = xp.shape[0] // tm
    nj = n // tn
    return pl.pallas_call(
        functools.partial(_outproj_kernel, n_first=n_first),
        grid=(m // tm, nj),
        in_specs=[pl.BlockSpec((tm, kh), lambda i, j: (jnp.minimum(i, n_first - 1), 0)),
                  pl.BlockSpec((tm, kh), lambda i, j: (jnp.maximum(i - n_first, 0), 0),
                               pipeline_mode=pl.Buffered(1)),
                  pl.BlockSpec((tm, kh), lambda i, j: (i, 0)),
                  pl.BlockSpec((kh, tn), lambda i, j: (0, j)),
                  pl.BlockSpec((kh, tn), lambda i, j: (1, j)),
                  pl.BlockSpec((tm, tn), lambda i, j: (jnp.minimum(i, n_first - 1),
                                                       jnp.where(i < n_first, j, nj - 1))),
                  pl.BlockSpec((tm, tn), lambda i, j: (jnp.maximum(i - n_first, 0),
                                                       jnp.where(i < n_first, 0, j)),
                               pipeline_mode=pl.Buffered(1)),
                  pl.BlockSpec((1, tn), lambda i, j: (0, j))],
        out_specs=[pl.BlockSpec((tm, tn), lambda i, j: (i, j)),
                   pl.BlockSpec((tm, tn), lambda i, j: (i, j)),
                   pl.BlockSpec((tm, LANES), lambda i, j: (i, 0))],
        out_shape=[jax.ShapeDtypeStruct((m, n), F32),
                   jax.ShapeDtypeStruct((m, n), BF16),
                   jax.ShapeDtypeStruct((m, LANES), F32)],
        compiler_params=_params("parallel", "arbitrary"),
        name="outproj",
    )(ret_p, ret_s, s5, w_out, w_out, xp, xs, g.reshape(1, n))


def _ffn_up_kernel(xg_ref, ssq_ref, wg_ref, wu_ref, o_ref, r_scr):
    tf = wg_ref.shape[1]

    @pl.when(pl.program_id(1) == 0)
    def _():
        r_scr[...] = lax.rsqrt(jnp.sum(ssq_ref[...], axis=-1, keepdims=True) / xg_ref.shape[1] + EPS)

    w = jnp.concatenate([wg_ref[...].astype(BF16), wu_ref[...].astype(BF16)], axis=1)
    gu = _dot(xg_ref[...], w) * r_scr[...]
    o_ref[...] = (jax.nn.silu(gu[:, :tf]) * gu[:, tf:]).astype(o_ref.dtype)


def _ffn_up(xg, ssq, w_gate, w_up, tm, tf):
    m, k = xg.shape
    f = w_gate.shape[1]
    return pl.pallas_call(
        _ffn_up_kernel,
        grid=(m // tm, f // tf),
        in_specs=[pl.BlockSpec((tm, k), lambda i, j: (i, 0)),
                  pl.BlockSpec((tm, LANES), lambda i, j: (i, 0)),
                  pl.BlockSpec((k, tf), lambda i, j: (0, j)),
                  pl.BlockSpec((k, tf), lambda i, j: (0, j))],
        out_specs=pl.BlockSpec((tm, tf), lambda i, j: (i, j)),
        out_shape=jax.ShapeDtypeStruct((m, f), BF16),
        scratch_shapes=[pltpu.VMEM((tm, 1), F32)],
        compiler_params=_params("parallel", "arbitrary"),
        name="ffn_up",
    )(xg, ssq, w_gate, w_up)


def _ffn_down_kernel(a_ref, w_ref, x_ref, o_ref):
    o_ref[...] = x_ref[...] + _dot(a_ref[...], w_ref[...].astype(BF16))


def _ffn_down(a, w_down, x, kblock, nkblocks, tm, tn):
    m, f = a.shape
    n = w_down.shape[1]
    tk = f // nkblocks
    return pl.pallas_call(
        _ffn_down_kernel,
        grid=(m // tm, n // tn),
        in_specs=[pl.BlockSpec((tm, tk), lambda i, j: (i, kblock)),
                  pl.BlockSpec((tk, tn), lambda i, j: (kblock, j)),
                  pl.BlockSpec((tm, tn), lambda i, j: (i, j))],
        out_specs=pl.BlockSpec((tm, tn), lambda i, j: (i, j)),
        out_shape=jax.ShapeDtypeStruct((m, n), F32),
        compiler_params=_params("parallel", "arbitrary"),
        name="ffn_down",
    )(a, w_down, x)


def _rotary(x, cos, sin):
    half = RET_HEAD_DIM // 2
    x1, x2 = x[:, :half], x[:, half:]
    return jnp.concatenate([x1 * cos - x2 * sin, x1 * sin + x2 * cos], axis=-1)


def _retention_chunk(lg, sdec, s, q, k, v, g, cos, sin, gn):
    n = q.shape[0]
    ri = lax.broadcasted_iota(jnp.int32, (n, n), 0)
    ci = lax.broadcasted_iota(jnp.int32, (n, n), 1)
    diff = (ri - ci).astype(F32)
    mask = jnp.where(diff >= 0, jnp.exp(jnp.maximum(diff, 0.0) * lg), 0.0)
    row = lax.broadcasted_iota(jnp.int32, (n, 1), 0).astype(F32)
    q_dec = jnp.exp(lg * (row + 1.0))
    k_dec = jnp.exp(lg * (n - 1.0 - row))

    qr = _rotary(q, cos, sin)
    kr = _rotary(k, cos, sin) * (RET_HEAD_DIM ** -0.5)
    vb = v.astype(BF16)
    scores = lax.dot_general(qr.astype(BF16), kr.astype(BF16), (((1,), (1,)), ((), ())),
                             preferred_element_type=F32) * mask
    o = _dot(scores.astype(BF16), vb) + _dot((qr * q_dec).astype(BF16), s.astype(BF16))
    s_new = sdec * s + lax.dot_general((kr * k_dec).astype(BF16), vb, (((0,), (0,)), ((), ())),
                                       preferred_element_type=F32)
    mu = jnp.mean(o, axis=-1, keepdims=True)
    oc = o - mu
    var = jnp.mean(oc * oc, axis=-1, keepdims=True)
    y = oc * lax.rsqrt(var + GN_EPS) * gn
    return y * jax.nn.silu(g), s_new


def _ret_seq_kernel(dec_ref, q_ref, k_ref, v_ref, g_ref, cos_ref, sin_ref, s0_ref, gn_ref,
                    o_ref, sf_ref, s_scr, *, chunk, nchunks):
    h = pl.program_id(1)
    lg = dec_ref[0, h]
    sdec = dec_ref[1, h]
    s_scr[...] = s0_ref[...]
    gn = gn_ref[...]

    def body(c, carry):
        rows = pl.ds(pl.multiple_of(c * chunk, chunk), chunk)
        y, s_new = _retention_chunk(lg, sdec, s_scr[...], q_ref[rows, :], k_ref[rows, :],
                                    v_ref[rows, :], g_ref[rows, :], cos_ref[rows, :],
                                    sin_ref[rows, :], gn)
        o_ref[rows, :] = y.astype(o_ref.dtype)
        s_scr[...] = s_new
        return carry

    lax.fori_loop(0, nchunks, body, 0, unroll=min(8, nchunks))
    sf_ref[...] = s_scr[...]


def _ret_decay_consts(chunk):
    lg = np.log(1.0 - 2.0 ** (-5.0 - np.arange(RET_HEADS, dtype=np.float64)))
    return jnp.asarray(np.stack([lg, np.exp(lg * chunk)]), dtype=F32)


def _rope_tables(pos):
    half = RET_HEAD_DIM // 2
    inv = ROPE_BASE ** (-np.arange(half, dtype=np.float64) / half)
    ang = np.asarray(pos, dtype=np.float64)[:, None] * inv[None, :]
    return jnp.asarray(np.cos(ang), dtype=F32), jnp.asarray(np.sin(ang), dtype=F32)


def _retention_seq(proj, row_block0, nbatch, seq, chunk, pos0, s0, gn_g):
    cos, sin = _rope_tables(pos0 + np.arange(seq))
    hd = RET_HEAD_DIM

    def col(off):
        return pl.BlockSpec((seq, hd), lambda b, h: (row_block0 + b, off + h))

    tbl = pl.BlockSpec((seq, hd // 2), lambda b, h: (0, 0))
    return pl.pallas_call(
        functools.partial(_ret_seq_kernel, chunk=chunk, nchunks=seq // chunk),
        grid=(nbatch, RET_HEADS),
        in_specs=[pl.BlockSpec(memory_space=pltpu.SMEM),
                  col(0), col(RET_HEADS), col(2 * RET_HEADS), col(3 * RET_HEADS), tbl, tbl,
                  pl.BlockSpec((None, None, hd, hd), lambda b, h: (0, h, 0, 0)),
                  pl.BlockSpec((1, hd), lambda b, h: (0, h))],
        out_specs=[pl.BlockSpec((seq, hd), lambda b, h: (b, h)),
                   pl.BlockSpec((None, None, hd, hd), lambda b, h: (b, h, 0, 0))],
        out_shape=[jax.ShapeDtypeStruct((nbatch * seq, RET_WIDTH), BF16),
                   jax.ShapeDtypeStruct((nbatch, RET_HEADS, hd, hd), F32)],
        scratch_shapes=[pltpu.VMEM((hd, hd), F32)],
        compiler_params=_params("parallel", "arbitrary"),
        name="retention_seq",
    )(_ret_decay_consts(chunk), proj, proj, proj, proj, cos, sin, s0, gn_g.reshape(1, RET_WIDTH))


def _ret_step_kernel(dec_ref, q_ref, k_ref, v_ref, g_ref, cos_ref, sin_ref, s_ref, gn_ref,
                     o_ref, sn_ref, *, nb, seq):
    hd = RET_HEAD_DIM
    cos, sin = cos_ref[...], sin_ref[...]
    for b in range(nb):
        rows = slice(b * seq, (b + 1) * seq)
        for h in range(RET_HEADS):
            cols = slice(h * hd, (h + 1) * hd)
            y, s_new = _retention_chunk(dec_ref[0, h], dec_ref[1, h], s_ref[b, h],
                                        q_ref[rows, cols], k_ref[rows, cols], v_ref[rows, cols],
                                        g_ref[rows, cols], cos, sin, gn_ref[:, cols])
            o_ref[rows, cols] = y.astype(o_ref.dtype)
            sn_ref[b, h] = s_new


def _retention_step(proj, row0, nbatch, seq, pos0, state, gn_g, nb):
    cos, sin = _rope_tables(pos0 + np.arange(seq))
    hd = RET_HEAD_DIM
    rb0 = row0 // (nb * seq)

    def col(off):
        return pl.BlockSpec((nb * seq, RET_WIDTH), lambda i: (rb0 + i, off))

    tbl = pl.BlockSpec((seq, hd // 2), lambda i: (0, 0))
    st = pl.BlockSpec((nb, RET_HEADS, hd, hd), lambda i: (i, 0, 0, 0))
    return pl.pallas_call(
        functools.partial(_ret_step_kernel, nb=nb, seq=seq),
        grid=(nbatch // nb,),
        in_specs=[pl.BlockSpec(memory_space=pltpu.SMEM),
                  col(0), col(1), col(2), col(3), tbl, tbl, st,
                  pl.BlockSpec((1, RET_WIDTH), lambda i: (0, 0))],
        out_specs=[pl.BlockSpec((nb * seq, RET_WIDTH), lambda i: (i, 0)), st],
        out_shape=[jax.ShapeDtypeStruct((nbatch * seq, RET_WIDTH), BF16),
                   jax.ShapeDtypeStruct(state.shape, F32)],
        compiler_params=_params("parallel"),
        name="retention_step",
    )(_ret_decay_consts(seq), proj, proj, proj, proj, cos, sin, state,
      gn_g.reshape(1, RET_WIDTH))


def _s5_disc_kernel(lr_ref, li_ref, ldt_ref, br_ref, bi_ref, ar_ref, ai_ref, bbr_ref, bbi_ref):
    lr, li = lr_ref[...], li_ref[...]
    dt = jnp.exp(ldt_ref[...])
    mag = jnp.exp(lr * dt)
    ar = mag * jnp.cos(li * dt)
    ai = mag * jnp.sin(li * dt)
    nr, ni = ar - 1.0, ai
    den = lr * lr + li * li
    fr = (nr * lr + ni * li) / den
    fi = (ni * lr - nr * li) / den
    br, bi = br_ref[...], bi_ref[...]
    ar_ref[...] = ar
    ai_ref[...] = ai
    bbr_ref[...] = fr * br - fi * bi
    bbi_ref[...] = fr * bi + fi * br


def _s5_discretize(lam_re, lam_im, log_dt, b_re, b_im):
    g, p, c = b_re.shape
    shp3 = jax.ShapeDtypeStruct((g, 1, p), F32)
    shpb = jax.ShapeDtypeStruct((g, c, p), F32)
    return pl.pallas_call(
        _s5_disc_kernel,
        out_shape=[shp3, shp3, shpb, shpb],
        name="s5_discretize",
    )(lam_re.reshape(g, 1, p), lam_im.reshape(g, 1, p), log_dt.reshape(g, 1, 1),
      jnp.swapaxes(b_re, 1, 2), jnp.swapaxes(b_im, 1, 2))


S5_NPAIR = S5_SLABS // 4


def _lanes(s):
    return slice(s * LANES, (s + 1) * LANES)


def _s5_place(s, nb, pitch):
    comp, ls = s // (2 * S5_NPAIR), s % (2 * S5_NPAIR)
    hf, p = ls // S5_NPAIR, ls % S5_NPAIR
    return comp * S5_NPAIR + p, hf * (nb * pitch)


def _s5_paired(ref, p, nb, order=None):
    def slab(hf):
        cols = _lanes(p + hf * S5_NPAIR)
        if order is None or ref.shape[0] == 1:
            return jnp.broadcast_to(ref[:, cols], (nb, LANES))
        return jnp.concatenate([ref[o:o + 1, cols] for o in order], axis=0)
    return jnp.concatenate([slab(0), slab(1)], axis=0)


def _s5_block_weights(bbr_ref, bbi_ref, cr_ref, ci_ref, bw_scr, cw_scr):
    bw_scr[...] = jnp.zeros_like(bw_scr)
    cw_scr[...] = jnp.zeros_like(cw_scr)
    for g in range(S5_GB):
        ch = slice(g * S5_GROUP, (g + 1) * S5_GROUP)
        for comp, (b_ref, c_ref, sign) in enumerate(((bbr_ref, cr_ref, 1.0), (bbi_ref, ci_ref, -1.0))):
            st = slice(comp * S5_ST + g * S5_STATE, comp * S5_ST + (g + 1) * S5_STATE)
            bw_scr[ch, st] = b_ref[g]
            cw_scr[st, ch] = sign * c_ref[g]


def _s5_step(buf, t, xs, ar, ai, nb, pitch):
    rows = pl.ds(t, 2 * nb, stride=pitch)
    new = [None] * (2 * S5_NPAIR)
    for p in range(S5_NPAIR):
        xr, xi = xs[p], xs[S5_NPAIR + p]
        nr = ar[p] * xr - ai[p] * xi + buf[p, rows, :]
        ni = ar[p] * xi + ai[p] * xr + buf[S5_NPAIR + p, rows, :]
        buf[p, rows, :] = nr
        buf[S5_NPAIR + p, rows, :] = ni
        new[p], new[S5_NPAIR + p] = nr, ni
    return tuple(new)


def _s5_scan_io(x0r_ref, x0i_ref, ar_ref, ai_ref, nb, order=None):
    ar = [_s5_paired(ar_ref, p, nb) for p in range(S5_NPAIR)]
    ai = [_s5_paired(ai_ref, p, nb) for p in range(S5_NPAIR)]
    xs0 = tuple([_s5_paired(x0r_ref, p, nb, order) for p in range(S5_NPAIR)]
                + [_s5_paired(x0i_ref, p, nb, order) for p in range(S5_NPAIR)])
    return ar, ai, xs0


def _s5_store_final(xs, xfr_ref, xfi_ref, nb, order=None):
    for p in range(S5_NPAIR):
        for hf in range(2):
            cols = _lanes(p + hf * S5_NPAIR)
            for x, ref in ((xs[p], xfr_ref), (xs[S5_NPAIR + p], xfi_ref)):
                if order is None:
                    ref[:, cols] = x[hf * nb:(hf + 1) * nb]
                else:
                    for i, o in enumerate(order):
                        ref[o:o + 1, cols] = x[hf * nb + i:hf * nb + i + 1]


def _s5_kernel(u_ref, x0r_ref, x0i_ref, ar_ref, ai_ref, bbr_ref, bbi_ref, cr_ref, ci_ref, d_ref,
               z_ref, xfr_ref, xfi_ref, buf, bw_scr, cw_scr, *, nb, seq):
    _s5_block_weights(bbr_ref, bbi_ref, cr_ref, ci_ref, bw_scr, cw_scr)
    ar, ai, xs = _s5_scan_io(x0r_ref, x0i_ref, ar_ref, ai_ref, nb)
    u = u_ref[...]
    bu = _dot(u.astype(BF16), bw_scr[...].astype(BF16))
    for s in range(S5_SLABS):
        q, off = _s5_place(s, nb, seq)
        buf[q, off:off + nb * seq, :] = bu[:, _lanes(s)]
    xs = lax.fori_loop(0, seq, lambda t, xs: _s5_step(buf, t, xs, ar, ai, nb, seq), xs,
                       unroll=8 if nb <= 8 else 1)
    parts = []
    for s in range(S5_SLABS):
        q, off = _s5_place(s, nb, seq)
        parts.append(buf[q, off:off + nb * seq, :])
    y = _dot(jnp.concatenate(parts, axis=1).astype(BF16), cw_scr[...].astype(BF16)) + d_ref[...] * u
    z_ref[...] = jax.nn.gelu(y).astype(z_ref.dtype)
    _s5_store_final(xs, xfr_ref, xfi_ref, nb)


def _s5_chunked_kernel(u_ref, x0r_ref, x0i_ref, ar_ref, ai_ref, bbr_ref, bbi_ref, cr_ref, ci_ref,
                       d_ref, z_ref, xfr_ref, xfi_ref, buf, bw_scr, cw_scr, *, nb, seq, tc, pitch):
    lead_odd = pitch % 8
    assert nb % 2 == 0 and nb >= 4 and lead_odd == 4 and (nb * pitch) % 8 == 0
    inner = list(range(1, nb - 1))
    even, odd = [0, nb - 1] + inner[nb // 2:], inner[:nb // 2]
    order = [(odd if i % 2 else even)[i // 2] for i in range(nb)]

    _s5_block_weights(bbr_ref, bbi_ref, cr_ref, ci_ref, bw_scr, cw_scr)
    bw = bw_scr[...].astype(BF16)
    cw = cw_scr[...].astype(BF16)
    ar, ai, xs = _s5_scan_io(x0r_ref, x0i_ref, ar_ref, ai_ref, nb, order)
    dvec = d_ref[...]

    def chunk(c, xs):
        t0 = c * tc
        rows = [pl.ds(pl.multiple_of(order[i] * seq + t0, tc), tc) for i in range(nb)]
        lead = [lead_odd if i % 2 else 0 for i in range(nb)]

        def u_window(i):
            if not lead[i]:
                return u_ref[rows[i], :]
            wide = u_ref[pl.ds(pl.multiple_of(order[i] * seq + t0 - 8, 8), tc + 16), :]
            return wide[8 - lead[i]:8 + tc + lead[i]]

        for i in range(nb):
            bu = _dot(u_window(i).astype(BF16), bw)
            for s in range(S5_SLABS):
                q, off = _s5_place(s, nb, pitch)
                r0 = off + i * pitch - lead[i]
                buf[q, r0:r0 + tc + 2 * lead[i], :] = bu[:, _lanes(s)]
        xs = lax.fori_loop(0, tc, lambda t, xs: _s5_step(buf, t, xs, ar, ai, nb, pitch), xs,
                           unroll=8)
        for i in range(nb):
            parts = []
            for s in range(S5_SLABS):
                q, off = _s5_place(s, nb, pitch)
                r0 = off + i * pitch - lead[i]
                parts.append(buf[q, r0:r0 + tc + 2 * lead[i], :])
            y = _dot(jnp.concatenate(parts, axis=1).astype(BF16), cw)[lead[i]:lead[i] + tc]
            y = y + dvec * u_ref[rows[i], :]
            z_ref[rows[i], :] = jax.nn.gelu(y).astype(z_ref.dtype)
        return xs

    xs = lax.fori_loop(0, seq // tc, chunk, xs)
    _s5_store_final(xs, xfr_ref, xfi_ref, nb, order)


def _s5(proj, row_block0, nb, seq, tc, x0r, x0i, ar, ai, bbr, bbi, c_re, c_im, d):
    rows = nb * seq
    u_col0 = 4 * RET_WIDTH // S5_CH
    nj = S5_GROUPS // S5_GB
    st = pl.BlockSpec((nb, S5_ST), lambda j: (0, j))
    coef = pl.BlockSpec((1, S5_ST), lambda j: (0, j))
    bspec = pl.BlockSpec((S5_GB, S5_GROUP, S5_STATE), lambda j: (j, 0, 0))
    cspec = pl.BlockSpec((S5_GB, S5_STATE, S5_GROUP), lambda j: (j, 0, 0))
    weights = [pltpu.VMEM((S5_CH, 2 * S5_ST), F32), pltpu.VMEM((2 * S5_ST, S5_CH), F32)]
    if tc == seq:
        body = functools.partial(_s5_kernel, nb=nb, seq=seq)
        pitch = seq
    else:
        pitch = tc + 4
        body = functools.partial(_s5_chunked_kernel, nb=nb, seq=seq, tc=tc, pitch=pitch)
    scratch = [pltpu.VMEM((S5_SLABS // 2, 2 * nb * pitch, LANES), F32)] + weights
    return pl.pallas_call(
        body,
        grid=(nj,),
        in_specs=[pl.BlockSpec((rows, S5_CH), lambda j: (row_block0, u_col0 + j)),
                  st, st, coef, coef, bspec, bspec, cspec, cspec,
                  pl.BlockSpec((1, S5_CH), lambda j: (0, j))],
        out_specs=[pl.BlockSpec((rows, S5_CH), lambda j: (0, j)), st, st],
        out_shape=[jax.ShapeDtypeStruct((rows, S5_WIDTH), BF16),
                   jax.ShapeDtypeStruct((nb, S5_NSTATE), F32),
                   jax.ShapeDtypeStruct((nb, S5_NSTATE), F32)],
        scratch_shapes=scratch,
        compiler_params=_params("parallel"),
        name="s5_scan",
    )(proj, x0r, x0i, ar, ai, bbr, bbi, c_re, c_im, d.reshape(1, S5_WIDTH))


def _s5_post_kernel(zp_ref, zs_ref, zpc_ref, zsc_ref, w_ref, b_ref, g_ref, o_ref,
                    zb_scr, gate_scr, ssq_scr, *, n_first, tn):
    i, j = pl.program_id(0), pl.program_id(1)
    first = i < n_first

    @pl.when(j == 0)
    def _():
        zb_scr[...] = jnp.where(first, zp_ref[...], zs_ref[...])
        ssq_scr[...] = jnp.zeros_like(ssq_scr)

    zc = jnp.where(first, zpc_ref[...], zsc_ref[...]).astype(F32)
    t = _dot(zb_scr[...], w_ref[...].astype(BF16)) + b_ref[...]
    s = zc * jax.nn.sigmoid(t)
    gate_scr[j] = s
    ssq_scr[...] += jnp.sum(s * s, axis=-1, keepdims=True)

    @pl.when(j == pl.num_programs(1) - 1)
    def _():
        r = lax.rsqrt(ssq_scr[...] / (tn * gate_scr.shape[0]) + EPS)
        for jj in range(gate_scr.shape[0]):
            cols = slice(jj * tn, (jj + 1) * tn)
            o_ref[:, cols] = (gate_scr[jj] * r * g_ref[:, cols]).astype(o_ref.dtype)


def _s5_post(zp, zs, w_glu, b_glu, norm_g, tm, tn):
    d = zp.shape[1]
    n_first = zp.shape[0] // tm
    nblk = n_first + zs.shape[0] // tm
    nj = d // tn

    def first_row(i):
        return jnp.minimum(i, n_first - 1)

    def second_row(i):
        return jnp.maximum(i - n_first, 0)

    return pl.pallas_call(
        functools.partial(_s5_post_kernel, n_first=n_first, tn=tn),
        grid=(nblk, nj),
        in_specs=[pl.BlockSpec((tm, d), lambda i, j: (first_row(i), 0)),
                  pl.BlockSpec((tm, d), lambda i, j: (second_row(i), 0)),
                  pl.BlockSpec((tm, tn), lambda i, j: (first_row(i), jnp.where(i < n_first, j, nj - 1))),
                  pl.BlockSpec((tm, tn), lambda i, j: (second_row(i), jnp.where(i < n_first, 0, j))),
                  pl.BlockSpec((d, tn), lambda i, j: (0, j)),
                  pl.BlockSpec((1, tn), lambda i, j: (0, j)),
                  pl.BlockSpec((1, d), lambda i, j: (0, 0))],
        out_specs=pl.BlockSpec((tm, d), lambda i, j: (i, 0)),
        out_shape=jax.ShapeDtypeStruct((nblk * tm, d), BF16),
        scratch_shapes=[pltpu.VMEM((tm, d), BF16), pltpu.VMEM((nj, tm, tn), F32),
                        pltpu.VMEM((tm, 1), F32)],
        compiler_params=_params("parallel", "arbitrary"),
        name="s5_post",
    )(zp, zs, zp, zs, w_glu, b_glu.reshape(1, d), norm_g.reshape(1, d))


def kernel(x_prompt, x_sample, state_ret, state_s5_re, state_s5_im, meta_tokens, norm1_g, w_in, ret_gn_g, s5_lam_re, s5_lam_im, s5_log_dt, s5_b_re, s5_b_im, s5_c_re, s5_c_im, s5_d, w_glu, b_glu, s5_norm_g, w_out, norm2_g, w_gate, w_up, w_down, final_norm_g):
    assert norm1_g.shape[0] == 1, "single-layer model"
    batch, seq, d = x_prompt.shape
    dbatch, dseq, _ = x_sample.shape
    rows_p, rows_s = batch * seq, dbatch * dseq
    tm = 1024
    assert rows_p % tm == 0 and rows_s % tm == 0 and seq % RET_CHUNK == 0
    xp = x_prompt.reshape(rows_p, d)
    xs = x_sample.reshape(rows_s, d)
    hd = RET_HEAD_DIM

    ar, ai, bbr, bbi = _s5_discretize(s5_lam_re[0], s5_lam_im[0], s5_log_dt[0], s5_b_re[0], s5_b_im[0])
    ar = ar.reshape(1, S5_NSTATE)
    ai = ai.reshape(1, S5_NSTATE)
    s5_w = (ar, ai, bbr, bbi, jnp.swapaxes(s5_c_re[0], 1, 2), jnp.swapaxes(s5_c_im[0], 1, 2), s5_d[0])

    h_meta = _rmsnorm([meta_tokens], norm1_g[0], N_META, BF16)
    proj_meta = _inproj(h_meta, w_in[0], N_META, 512)
    zero_ret = jnp.zeros((1, RET_HEADS, hd, hd), F32)
    _, s_meta = _retention_seq(proj_meta, 0, 1, N_META, N_META, 0, zero_ret, ret_gn_g[0])
    zero_s5 = jnp.zeros((1, S5_NSTATE), F32)
    _, mr, mi = _s5(proj_meta, 0, 1, N_META, N_META, zero_s5, zero_s5, *s5_w)

    h = _rmsnorm([xp, xs], norm1_g[0], 512, BF16)
    proj = _inproj(h, w_in[0], 1536, 512)

    ret_p, sret_p = _retention_seq(proj, 0, batch, seq, RET_CHUNK, N_META, s_meta, ret_gn_g[0])
    ret_s, sret_s = _retention_step(proj, rows_p, dbatch, dseq, PAST_LEN, state_ret[0],
                                    ret_gn_g[0], 4)

    z_p, s5r_p, s5i_p = _s5(proj, 0, batch, seq, 256,
                            jnp.broadcast_to(mr, (batch, S5_NSTATE)),
                            jnp.broadcast_to(mi, (batch, S5_NSTATE)), *s5_w)
    z_s, s5r_s, s5i_s = _s5(proj, rows_p // rows_s, dbatch, dseq, dseq,
                            state_s5_re[0].reshape(dbatch, S5_NSTATE),
                            state_s5_im[0].reshape(dbatch, S5_NSTATE), *s5_w)
    s5_out = _s5_post(z_p, z_s, w_glu[0], b_glu[0], s5_norm_g[0], tm, 512)

    x1, x1g, x1_ssq = _outproj(ret_p, ret_s, s5_out, w_out[0], xp, xs, norm2_g[0], tm, 512)
    a = _ffn_up(x1g, x1_ssq, w_gate[0], w_up[0], tm, 256)
    x2 = _ffn_down(a, w_down[0], x1, 0, 2, tm, 256)
    x2 = _ffn_down(a, w_down[0], x2, 1, 2, tm, 256)
    y_p, y_s = _final_norm(x2, final_norm_g, rows_p, 512)

    st = (1, -1, S5_GROUPS, S5_STATE)
    return (y_p.reshape(batch, seq, d), y_s.reshape(dbatch, dseq, d),
            sret_p[None], s5r_p.reshape(st), s5i_p.reshape(st),
            sret_s[None], s5r_s.reshape(st), s5i_s.reshape(st))
```

```python
import functools
import math

import numpy as np
import jax
import jax.numpy as jnp
from jax import lax
from jax.experimental import pallas as pl
from jax.experimental.pallas import tpu as pltpu

D_MODEL = 4096
N_META = 16
PAST_LEN = 16384
RET_WIDTH = D_MODEL // 2
S5_WIDTH = D_MODEL - RET_WIDTH
RET_HEADS = 8
RET_HEAD_DIM = RET_WIDTH // RET_HEADS
RET_CHUNK = 128
ROPE_BASE = 10000.0
S5_GROUP = 16
S5_GROUPS = S5_WIDTH // S5_GROUP
S5_STATE = 64
S5_NSTATE = S5_GROUPS * S5_STATE
IN_COLS = 4 * RET_WIDTH + S5_WIDTH
EPS = 1e-6
GN_EPS = 1e-5

LANES = 128
S5_GB = 16
S5_CH = S5_GB * S5_GROUP
S5_ST = S5_GB * S5_STATE
S5_SLABS = 2 * S5_ST // LANES
VMEM_LIMIT = 56 * 1024 * 1024

BF16 = jnp.bfloat16
F32 = jnp.float32


def _params(*sem):
    return pltpu.CompilerParams(dimension_semantics=sem, vmem_limit_bytes=VMEM_LIMIT)


def _dot(a, b):
    return jnp.dot(a, b, preferred_element_type=F32)


def _rmsnorm_kernel(*refs, n_src, n_first):
    x_refs, g_ref, o_ref = refs[:n_src], refs[n_src], refs[n_src + 1]
    x = x_refs[0][...]
    if n_src == 2:
        x = jnp.where(pl.program_id(0) < n_first, x, x_refs[1][...])
    r = lax.rsqrt(jnp.mean(x * x, axis=-1, keepdims=True) + EPS)
    o_ref[...] = (x * r * g_ref[...]).astype(o_ref.dtype)


def _rmsnorm(srcs, g, tm, out_dtype):
    d = srcs[0].shape[1]
    nblk = [s.shape[0] // tm for s in srcs]
    n_first = nblk[0]
    specs = [pl.BlockSpec((tm, d), lambda i: (jnp.minimum(i, n_first - 1), 0))]
    if len(srcs) == 2:
        specs.append(pl.BlockSpec((tm, d), lambda i: (jnp.maximum(i - n_first, 0), 0)))
    rows = sum(s.shape[0] for s in srcs)
    return pl.pallas_call(
        functools.partial(_rmsnorm_kernel, n_src=len(srcs), n_first=n_first),
        grid=(sum(nblk),),
        in_specs=specs + [pl.BlockSpec((1, d), lambda i: (0, 0))],
        out_specs=pl.BlockSpec((tm, d), lambda i: (i, 0)),
        out_shape=jax.ShapeDtypeStruct((rows, d), out_dtype),
        compiler_params=_params("parallel"),
        name="rmsnorm",
    )(*srcs, g.reshape(1, d))


def _final_norm_kernel(x_ref, g_ref, yp_ref, ys_ref, *, n_first):
    x = x_ref[...]
    r = lax.rsqrt(jnp.mean(x * x, axis=-1, keepdims=True) + EPS)
    y = x * r * g_ref[...]
    i = pl.program_id(0)

    @pl.when(i < n_first)
    def _():
        yp_ref[...] = y

    @pl.when(i >= n_first)
    def _():
        ys_ref[...] = y


def _final_norm(x, g, rows_p, tm):
    rows, d = x.shape
    n_first = rows_p // tm
    return pl.pallas_call(
        functools.partial(_final_norm_kernel, n_first=n_first),
        grid=(rows // tm,),
        in_specs=[pl.BlockSpec((tm, d), lambda i: (i, 0)),
                  pl.BlockSpec((1, d), lambda i: (0, 0))],
        out_specs=[pl.BlockSpec((tm, d), lambda i: (jnp.minimum(i, n_first - 1), 0)),
                   pl.BlockSpec((tm, d), lambda i: (jnp.maximum(i - n_first, 0), 0))],
        out_shape=[jax.ShapeDtypeStruct((rows_p, d), F32),
                   jax.ShapeDtypeStruct((rows - rows_p, d), F32)],
        compiler_params=_params("arbitrary"),
        name="final_norm",
    )(x, g.reshape(1, d))


def _inproj_kernel(h_ref, hx_ref, w_ref, o_ref, ox_ref):
    w = w_ref[...].astype(BF16)
    o_ref[...] = _dot(h_ref[...], w)

    @pl.when(pl.program_id(0) == 0)
    def _():
        ox_ref[...] = _dot(hx_ref[...], w)


def _inproj(h, h_extra, w, tm, tn):
    m, k = h.shape
    mx = h_extra.shape[0]
    n = w.shape[1]
    nj = n // tn
    return pl.pallas_call(
        _inproj_kernel,
        grid=(m // tm, nj),
        in_specs=[pl.BlockSpec((tm, k), lambda i, j: (i, 0)),
                  pl.BlockSpec((mx, k), lambda i, j: (0, 0)),
                  pl.BlockSpec((k, tn), lambda i, j: (0, j))],
        out_specs=[pl.BlockSpec((tm, tn), lambda i, j: (i, j)),
                   pl.BlockSpec((mx, tn), lambda i, j: (0, jnp.where(i == 0, j, nj - 1)))],
        out_shape=[jax.ShapeDtypeStruct((m, n), F32), jax.ShapeDtypeStruct((mx, n), F32)],
        compiler_params=_params("arbitrary", "arbitrary"),
        name="inproj",
    )(h, h_extra, w)


def _outproj_kernel(retp_ref, rets_ref, s5_ref, wt_ref, wb_ref, xp_ref, xs_ref, g_ref,
                    o_ref, xg_ref, ssq_ref, *, n_first):
    first = pl.program_id(0) < n_first
    j = pl.program_id(1)
    x = jnp.where(first, xp_ref[...], xs_ref[...])
    ret = jnp.where(first, retp_ref[...], rets_ref[...])
    acc = _dot(ret, wt_ref[...])
    acc += _dot(s5_ref[...], wb_ref[...])
    x1 = x + acc
    o_ref[...] = x1
    xg_ref[...] = (x1 * g_ref[...]).astype(xg_ref.dtype)
    sq = x1 * x1
    part = sq[:, :LANES]
    for c in range(1, sq.shape[1] // LANES):
        part += sq[:, c * LANES:(c + 1) * LANES]

    @pl.when(j == 0)
    def _():
        ssq_ref[...] = part

    @pl.when(j != 0)
    def _():
        ssq_ref[...] += part


def _outproj(ret_p, ret_s, s5, w_out, xp, xs, g, tm, tn):
    m, kh = s5.shape
    n = w_out.shape[1]
    n_first = xp.shape[0] // tm
    nj = n // tn
    return pl.pallas_call(
        functools.partial(_outproj_kernel, n_first=n_first),
        grid=(m // tm, nj),
        in_specs=[pl.BlockSpec((tm, kh), lambda i, j: (jnp.minimum(i, n_first - 1), 0)),
                  pl.BlockSpec((tm, kh), lambda i, j: (jnp.maximum(i - n_first, 0), 0),
                               pipeline_mode=pl.Buffered(1)),
                  pl.BlockSpec((tm, kh), lambda i, j: (i, 0)),
                  pl.BlockSpec((kh, tn), lambda i, j: (0, j)),
                  pl.BlockSpec((kh, tn), lambda i, j: (1, j)),
                  pl.BlockSpec((tm, tn), lambda i, j: (jnp.minimum(i, n_first - 1),
                                                       jnp.where(i < n_first, j, nj - 1))),
                  pl.BlockSpec((tm, tn), lambda i, j: (jnp.maximum(i - n_first, 0),
                                                       jnp.where(i < n_first, 0, j)),
                               pipeline_mode=pl.Buffered(1)),
                  pl.BlockSpec((1, tn), lambda i, j: (0, j))],
        out_specs=[pl.BlockSpec((tm, tn), lambda i, j: (i, j)),
                   pl.BlockSpec((tm, tn), lambda i, j: (i, j)),
                   pl.BlockSpec((tm, LANES), lambda i, j: (i, 0))],
        out_shape=[jax.ShapeDtypeStruct((m, n), F32),
                   jax.ShapeDtypeStruct((m, n), BF16),
                   jax.ShapeDtypeStruct((m, LANES), F32)],
        compiler_params=_params("parallel", "arbitrary"),
        name="outproj",
    )(ret_p, ret_s, s5, w_out, w_out, xp, xs, g.reshape(1, n))


def _ffn_up_kernel(xg_ref, ssq_ref, wg_ref, wu_ref, o_ref, r_scr):
    tf = wg_ref.shape[1]

    @pl.when(pl.program_id(1) == 0)
    def _():
        r_scr[...] = lax.rsqrt(jnp.sum(ssq_ref[...], axis=-1, keepdims=True) / xg_ref.shape[1] + EPS)

    w = jnp.concatenate([wg_ref[...].astype(BF16), wu_ref[...].astype(BF16)], axis=1)
    gu = _dot(xg_ref[...], w) * r_scr[...]
    o_ref[...] = (jax.nn.silu(gu[:, :tf]) * gu[:, tf:]).astype(o_ref.dtype)


def _ffn_up(xg, ssq, w_gate, w_up, tm, tf):
    m, k = xg.shape
    f = w_gate.shape[1]
    return pl.pallas_call(
        _ffn_up_kernel,
        grid=(m // tm, f // tf),
        in_specs=[pl.BlockSpec((tm, k), lambda i, j: (i, 0)),
                  pl.BlockSpec((tm, LANES), lambda i, j: (i, 0)),
                  pl.BlockSpec((k, tf), lambda i, j: (0, j)),
                  pl.BlockSpec((k, tf), lambda i, j: (0, j))],
        out_specs=pl.BlockSpec((tm, tf), lambda i, j: (i, j)),
        out_shape=jax.ShapeDtypeStruct((m, f), BF16),
        scratch_shapes=[pltpu.VMEM((tm, 1), F32)],
        compiler_params=_params("parallel", "arbitrary"),
        name="ffn_up",
    )(xg, ssq, w_gate, w_up)


def _ffn_down_kernel(a_ref, w_ref, x_ref, o_ref):
    o_ref[...] = x_ref[...] + _dot(a_ref[...], w_ref[...].astype(BF16))


def _ffn_down(a, w_down, x, kblock, nkblocks, tm, tn):
    m, f = a.shape
    n = w_down.shape[1]
    tk = f // nkblocks
    return pl.pallas_call(
        _ffn_down_kernel,
        grid=(m // tm, n // tn),
        in_specs=[pl.BlockSpec((tm, tk), lambda i, j: (i, kblock)),
                  pl.BlockSpec((tk, tn), lambda i, j: (kblock, j)),
                  pl.BlockSpec((tm, tn), lambda i, j: (i, j))],
        out_specs=pl.BlockSpec((tm, tn), lambda i, j: (i, j)),
        out_shape=jax.ShapeDtypeStruct((m, n), F32),
        compiler_params=_params("parallel", "arbitrary"),
        name="ffn_down",
    )(a, w_down, x)


def _rotary(x, cos, sin):
    half = RET_HEAD_DIM // 2
    x1, x2 = x[:, :half], x[:, half:]
    return jnp.concatenate([x1 * cos - x2 * sin, x1 * sin + x2 * cos], axis=-1)


def _retention_chunk(lg, sdec, s, q, k, v, g, cos, sin, gn):
    n = q.shape[0]
    ri = lax.broadcasted_iota(jnp.int32, (n, n), 0)
    ci = lax.broadcasted_iota(jnp.int32, (n, n), 1)
    diff = (ri - ci).astype(F32)
    mask = jnp.where(diff >= 0, jnp.exp(jnp.maximum(diff, 0.0) * lg), 0.0)
    row = lax.broadcasted_iota(jnp.int32, (n, 1), 0).astype(F32)
    q_dec = jnp.exp(lg * (row + 1.0))
    k_dec = jnp.exp(lg * (n - 1.0 - row))

    qr = _rotary(q, cos, sin)
    kr = _rotary(k, cos, sin) * (RET_HEAD_DIM ** -0.5)
    vb = v.astype(BF16)
    scores = lax.dot_general(qr.astype(BF16), kr.astype(BF16), (((1,), (1,)), ((), ())),
                             preferred_element_type=F32) * mask
    o = _dot(scores.astype(BF16), vb) + _dot((qr * q_dec).astype(BF16), s.astype(BF16))
    s_new = sdec * s + lax.dot_general((kr * k_dec).astype(BF16), vb, (((0,), (0,)), ((), ())),
                                       preferred_element_type=F32)
    mu = jnp.mean(o, axis=-1, keepdims=True)
    oc = o - mu
    var = jnp.mean(oc * oc, axis=-1, keepdims=True)
    y = oc * lax.rsqrt(var + GN_EPS) * gn
    return y * jax.nn.silu(g), s_new


def _ret_seq_kernel(dec_ref, q_ref, k_ref, v_ref, g_ref, cos_ref, sin_ref, s0_ref, gn_ref,
                    o_ref, sf_ref, s_scr, *, chunk, nchunks):
    h = pl.program_id(1)
    lg = dec_ref[0, h]
    sdec = dec_ref[1, h]
    s_scr[...] = s0_ref[...]
    gn = gn_ref[...]

    def body(c, carry):
        rows = pl.ds(pl.multiple_of(c * chunk, chunk), chunk)
        y, s_new = _retention_chunk(lg, sdec, s_scr[...], q_ref[rows, :], k_ref[rows, :],
                                    v_ref[rows, :], g_ref[rows, :], cos_ref[rows, :],
                                    sin_ref[rows, :], gn)
        o_ref[rows, :] = y.astype(o_ref.dtype)
        s_scr[...] = s_new
        return carry

    lax.fori_loop(0, nchunks, body, 0, unroll=min(8, nchunks))
    sf_ref[...] = s_scr[...]


def _ret_decay_consts(chunk):
    lg = np.log(1.0 - 2.0 ** (-5.0 - np.arange(RET_HEADS, dtype=np.float64)))
    return jnp.asarray(np.stack([lg, np.exp(lg * chunk)]), dtype=F32)


def _rope_tables(pos):
    half = RET_HEAD_DIM // 2
    inv = ROPE_BASE ** (-np.arange(half, dtype=np.float64) / half)
    ang = np.asarray(pos, dtype=np.float64)[:, None] * inv[None, :]
    return jnp.asarray(np.cos(ang), dtype=F32), jnp.asarray(np.sin(ang), dtype=F32)


def _retention_seq(proj, row_block0, nbatch, seq, chunk, pos0, s0, gn_g):
    cos, sin = _rope_tables(pos0 + np.arange(seq))
    hd = RET_HEAD_DIM

    def col(off):
        return pl.BlockSpec((seq, hd), lambda b, h: (row_block0 + b, off + h))

    tbl = pl.BlockSpec((seq, hd // 2), lambda b, h: (0, 0))
    return pl.pallas_call(
        functools.partial(_ret_seq_kernel, chunk=chunk, nchunks=seq // chunk),
        grid=(nbatch, RET_HEADS),
        in_specs=[pl.BlockSpec(memory_space=pltpu.SMEM),
                  col(0), col(RET_HEADS), col(2 * RET_HEADS), col(3 * RET_HEADS), tbl, tbl,
                  pl.BlockSpec((None, None, hd, hd), lambda b, h: (0, h, 0, 0)),
                  pl.BlockSpec((1, hd), lambda b, h: (0, h))],
        out_specs=[pl.BlockSpec((seq, hd), lambda b, h: (b, h)),
                   pl.BlockSpec((None, None, hd, hd), lambda b, h: (b, h, 0, 0))],
        out_shape=[jax.ShapeDtypeStruct((nbatch * seq, RET_WIDTH), BF16),
                   jax.ShapeDtypeStruct((nbatch, RET_HEADS, hd, hd), F32)],
        scratch_shapes=[pltpu.VMEM((hd, hd), F32)],
        compiler_params=_params("parallel", "arbitrary"),
        name="retention_seq",
    )(_ret_decay_consts(chunk), proj, proj, proj, proj, cos, sin, s0, gn_g.reshape(1, RET_WIDTH))


def _ret_step_kernel(dec_ref, q_ref, k_ref, v_ref, g_ref, cos_ref, sin_ref, s_ref, gn_ref,
                     o_ref, sn_ref, *, nb, seq):
    hd = RET_HEAD_DIM
    cos, sin = cos_ref[...], sin_ref[...]
    for b in range(nb):
        rows = slice(b * seq, (b + 1) * seq)
        for h in range(RET_HEADS):
            cols = slice(h * hd, (h + 1) * hd)
            y, s_new = _retention_chunk(dec_ref[0, h], dec_ref[1, h], s_ref[b, h],
                                        q_ref[rows, cols], k_ref[rows, cols], v_ref[rows, cols],
                                        g_ref[rows, cols], cos, sin, gn_ref[:, cols])
            o_ref[rows, cols] = y.astype(o_ref.dtype)
            sn_ref[b, h] = s_new


def _retention_step(proj, row0, nbatch, seq, pos0, state, gn_g, nb):
    cos, sin = _rope_tables(pos0 + np.arange(seq))
    hd = RET_HEAD_DIM
    rb0 = row0 // (nb * seq)

    def col(off):
        return pl.BlockSpec((nb * seq, RET_WIDTH), lambda i: (rb0 + i, off))

    tbl = pl.BlockSpec((seq, hd // 2), lambda i: (0, 0))
    st = pl.BlockSpec((nb, RET_HEADS, hd, hd), lambda i: (i, 0, 0, 0))
    return pl.pallas_call(
        functools.partial(_ret_step_kernel, nb=nb, seq=seq),
        grid=(nbatch // nb,),
        in_specs=[pl.BlockSpec(memory_space=pltpu.SMEM),
                  col(0), col(1), col(2), col(3), tbl, tbl, st,
                  pl.BlockSpec((1, RET_WIDTH), lambda i: (0, 0))],
        out_specs=[pl.BlockSpec((nb * seq, RET_WIDTH), lambda i: (i, 0)), st],
        out_shape=[jax.ShapeDtypeStruct((nbatch * seq, RET_WIDTH), BF16),
                   jax.ShapeDtypeStruct(state.shape, F32)],
        compiler_params=_params("parallel"),
        name="retention_step",
    )(_ret_decay_consts(seq), proj, proj, proj, proj, cos, sin, state,
      gn_g.reshape(1, RET_WIDTH))


def _s5_disc_kernel(lr_ref, li_ref, ldt_ref, br_ref, bi_ref, ar_ref, ai_ref, bbr_ref, bbi_ref):
    lr, li = lr_ref[...], li_ref[...]
    dt = jnp.exp(ldt_ref[...])
    mag = jnp.exp(lr * dt)
    ar = mag * jnp.cos(li * dt)
    ai = mag * jnp.sin(li * dt)
    nr, ni = ar - 1.0, ai
    den = lr * lr + li * li
    fr = (nr * lr + ni * li) / den
    fi = (ni * lr - nr * li) / den
    br, bi = br_ref[...], bi_ref[...]
    ar_ref[...] = ar
    ai_ref[...] = ai
    bbr_ref[...] = fr * br - fi * bi
    bbi_ref[...] = fr * bi + fi * br


def _s5_discretize(lam_re, lam_im, log_dt, b_re, b_im):
    g, p, c = b_re.shape
    shp3 = jax.ShapeDtypeStruct((g, 1, p), F32)
    shpb = jax.ShapeDtypeStruct((g, c, p), F32)
    return pl.pallas_call(
        _s5_disc_kernel,
        out_shape=[shp3, shp3, shpb, shpb],
        name="s5_discretize",
    )(lam_re.reshape(g, 1, p), lam_im.reshape(g, 1, p), log_dt.reshape(g, 1, 1),
      jnp.swapaxes(b_re, 1, 2), jnp.swapaxes(b_im, 1, 2))


S5_NPAIR = S5_SLABS // 4


def _lanes(s):
    return slice(s * LANES, (s + 1) * LANES)


def _s5_place(s, nb, pitch):
    comp, ls = s // (2 * S5_NPAIR), s % (2 * S5_NPAIR)
    hf, p = ls // S5_NPAIR, ls % S5_NPAIR
    return comp * S5_NPAIR + p, hf * (nb * pitch)


def _s5_paired(ref, p, nb, order=None):
    def slab(hf):
        cols = _lanes(p + hf * S5_NPAIR)
        if order is None or ref.shape[0] == 1:
            return jnp.broadcast_to(ref[:, cols], (nb, LANES))
        return jnp.concatenate([ref[o:o + 1, cols] for o in order], axis=0)
    return jnp.concatenate([slab(0), slab(1)], axis=0)


def _s5_block_weights(bbr_ref, bbi_ref, cr_ref, ci_ref, bw_scr, cw_scr):
    bw_scr[...] = jnp.zeros_like(bw_scr)
    cw_scr[...] = jnp.zeros_like(cw_scr)
    for g in range(S5_GB):
        ch = slice(g * S5_GROUP, (g + 1) * S5_GROUP)
        for comp, (b_ref, c_ref, sign) in enumerate(((bbr_ref, cr_ref, 1.0), (bbi_ref, ci_ref, -1.0))):
            st = slice(comp * S5_ST + g * S5_STATE, comp * S5_ST + (g + 1) * S5_STATE)
            bw_scr[ch, st] = b_ref[g]
            cw_scr[st, ch] = sign * c_ref[g]


def _s5_step(buf, t, xs, ar, ai, nb, pitch):
    rows = pl.ds(t, 2 * nb, stride=pitch)
    new = [None] * (2 * S5_NPAIR)
    for p in range(S5_NPAIR):
        xr, xi = xs[p], xs[S5_NPAIR + p]
        nr = ar[p] * xr - ai[p] * xi + buf[p, rows, :]
        ni = ar[p] * xi + ai[p] * xr + buf[S5_NPAIR + p, rows, :]
        buf[p, rows, :] = nr
        buf[S5_NPAIR + p, rows, :] = ni
        new[p], new[S5_NPAIR + p] = nr, ni
    return tuple(new)


def _s5_scan_io(x0r_ref, x0i_ref, ar_ref, ai_ref, nb, order=None):
    ar = [_s5_paired(ar_ref, p, nb) for p in range(S5_NPAIR)]
    ai = [_s5_paired(ai_ref, p, nb) for p in range(S5_NPAIR)]
    xs0 = tuple([_s5_paired(x0r_ref, p, nb, order) for p in range(S5_NPAIR)]
                + [_s5_paired(x0i_ref, p, nb, order) for p in range(S5_NPAIR)])
    return ar, ai, xs0


def _s5_store_final(xs, xfr_ref, xfi_ref, nb, order=None):
    for p in range(S5_NPAIR):
        for hf in range(2):
            cols = _lanes(p + hf * S5_NPAIR)
            for x, ref in ((xs[p], xfr_ref), (xs[S5_NPAIR + p], xfi_ref)):
                if order is None:
                    ref[:, cols] = x[hf * nb:(hf + 1) * nb]
                else:
                    for i, o in enumerate(order):
                        ref[o:o + 1, cols] = x[hf * nb + i:hf * nb + i + 1]


def _s5_kernel(u_ref, x0r_ref, x0i_ref, ar_ref, ai_ref, bbr_ref, bbi_ref, cr_ref, ci_ref, d_ref,
               z_ref, xfr_ref, xfi_ref, buf, bw_scr, cw_scr, *, nb, seq):
    _s5_block_weights(bbr_ref, bbi_ref, cr_ref, ci_ref, bw_scr, cw_scr)
    ar, ai, xs = _s5_scan_io(x0r_ref, x0i_ref, ar_ref, ai_ref, nb)
    u = u_ref[...]
    bu = _dot(u.astype(BF16), bw_scr[...].astype(BF16))
    for s in range(S5_SLABS):
        q, off = _s5_place(s, nb, seq)
        buf[q, off:off + nb * seq, :] = bu[:, _lanes(s)]
    xs = lax.fori_loop(0, seq, lambda t, xs: _s5_step(buf, t, xs, ar, ai, nb, seq), xs,
                       unroll=8 if nb <= 8 else 1)
    parts = []
    for s in range(S5_SLABS):
        q, off = _s5_place(s, nb, seq)
        parts.append(buf[q, off:off + nb * seq, :])
    y = _dot(jnp.concatenate(parts, axis=1).astype(BF16), cw_scr[...].astype(BF16)) + d_ref[...] * u
    z_ref[...] = jax.nn.gelu(y).astype(z_ref.dtype)
    _s5_store_final(xs, xfr_ref, xfi_ref, nb)


def _s5_chunked_kernel(u_ref, x0r_ref, x0i_ref, ar_ref, ai_ref, bbr_ref, bbi_ref, cr_ref, ci_ref,
                       d_ref, z_ref, xfr_ref, xfi_ref, buf, bw_scr, cw_scr, *, nb, seq, tc, pitch):
    lead_odd = pitch % 8
    assert nb % 2 == 0 and nb >= 4 and lead_odd == 4 and (nb * pitch) % 8 == 0
    inner = list(range(1, nb - 1))
    even, odd = [0, nb - 1] + inner[nb // 2:], inner[:nb // 2]
    order = [(odd if i % 2 else even)[i // 2] for i in range(nb)]

    _s5_block_weights(bbr_ref, bbi_ref, cr_ref, ci_ref, bw_scr, cw_scr)
    bw = bw_scr[...].astype(BF16)
    cw = cw_scr[...].astype(BF16)
    ar, ai, xs = _s5_scan_io(x0r_ref, x0i_ref, ar_ref, ai_ref, nb, order)
    dvec = d_ref[...]

    def chunk(c, xs):
        t0 = c * tc
        rows = [pl.ds(pl.multiple_of(order[i] * seq + t0, tc), tc) for i in range(nb)]
        lead = [lead_odd if i % 2 else 0 for i in range(nb)]

        def u_window(i):
            if not lead[i]:
                return u_ref[rows[i], :]
            wide = u_ref[pl.ds(pl.multiple_of(order[i] * seq + t0 - 8, 8), tc + 16), :]
            return wide[8 - lead[i]:8 + tc + lead[i]]

        for i in range(nb):
            bu = _dot(u_window(i).astype(BF16), bw)
            for s in range(S5_SLABS):
                q, off = _s5_place(s, nb, pitch)
                r0 = off + i * pitch - lead[i]
                buf[q, r0:r0 + tc + 2 * lead[i], :] = bu[:, _lanes(s)]
        xs = lax.fori_loop(0, tc, lambda t, xs: _s5_step(buf, t, xs, ar, ai, nb, pitch), xs,
                           unroll=8)
        for i in range(nb):
            parts = []
            for s in range(S5_SLABS):
                q, off = _s5_place(s, nb, pitch)
                r0 = off + i * pitch - lead[i]
                parts.append(buf[q, r0:r0 + tc + 2 * lead[i], :])
            y = _dot(jnp.concatenate(parts, axis=1).astype(BF16), cw)[lead[i]:lead[i] + tc]
            y = y + dvec * u_ref[rows[i], :]
            z_ref[rows[i], :] = jax.nn.gelu(y).astype(z_ref.dtype)
        return xs

    xs = lax.fori_loop(0, seq // tc, chunk, xs)
    _s5_store_final(xs, xfr_ref, xfi_ref, nb, order)


def _s5(proj, row_block0, nb, seq, tc, x0r, x0i, ar, ai, bbr, bbi, c_re, c_im, d):
    rows = nb * seq
    u_col0 = 4 * RET_WIDTH // S5_CH
    nj = S5_GROUPS // S5_GB
    st = pl.BlockSpec((nb, S5_ST), lambda j: (0, j))
    coef = pl.BlockSpec((1, S5_ST), lambda j: (0, j))
    bspec = pl.BlockSpec((S5_GB, S5_GROUP, S5_STATE), lambda j: (j, 0, 0))
    cspec = pl.BlockSpec((S5_GB, S5_STATE, S5_GROUP), lambda j: (j, 0, 0))
    weights = [pltpu.VMEM((S5_CH, 2 * S5_ST), F32), pltpu.VMEM((2 * S5_ST, S5_CH), F32)]
    if tc == seq:
        body = functools.partial(_s5_kernel, nb=nb, seq=seq)
        pitch = seq
    else:
        pitch = tc + 4
        body = functools.partial(_s5_chunked_kernel, nb=nb, seq=seq, tc=tc, pitch=pitch)
    scratch = [pltpu.VMEM((S5_SLABS // 2, 2 * nb * pitch, LANES), F32)] + weights
    return pl.pallas_call(
        body,
        grid=(nj,),
        in_specs=[pl.BlockSpec((rows, S5_CH), lambda j: (row_block0, u_col0 + j)),
                  st, st, coef, coef, bspec, bspec, cspec, cspec,
                  pl.BlockSpec((1, S5_CH), lambda j: (0, j))],
        out_specs=[pl.BlockSpec((rows, S5_CH), lambda j: (0, j)), st, st],
        out_shape=[jax.ShapeDtypeStruct((rows, S5_WIDTH), BF16),
                   jax.ShapeDtypeStruct((nb, S5_NSTATE), F32),
                   jax.ShapeDtypeStruct((nb, S5_NSTATE), F32)],
        scratch_shapes=scratch,
        compiler_params=_params("parallel"),
        name="s5_scan",
    )(proj, x0r, x0i, ar, ai, bbr, bbi, c_re, c_im, d.reshape(1, S5_WIDTH))


def _s5_post_kernel(zp_ref, zs_ref, zpc_ref, zsc_ref, w_ref, b_ref, g_ref, o_ref,
                    zb_scr, gate_scr, ssq_scr, *, n_first, tn):
    i, j = pl.program_id(0), pl.program_id(1)
    first = i < n_first

    @pl.when(j == 0)
    def _():
        zb_scr[...] = jnp.where(first, zp_ref[...], zs_ref[...])
        ssq_scr[...] = jnp.zeros_like(ssq_scr)

    zc = jnp.where(first, zpc_ref[...], zsc_ref[...]).astype(F32)
    t = _dot(zb_scr[...], w_ref[...].astype(BF16)) + b_ref[...]
    s = zc * jax.nn.sigmoid(t)
    gate_scr[j] = s
    ssq_scr[...] += jnp.sum(s * s, axis=-1, keepdims=True)

    @pl.when(j == pl.num_programs(1) - 1)
    def _():
        r = lax.rsqrt(ssq_scr[...] / (tn * gate_scr.shape[0]) + EPS)
        for jj in range(gate_scr.shape[0]):
            cols = slice(jj * tn, (jj + 1) * tn)
            o_ref[:, cols] = (gate_scr[jj] * r * g_ref[:, cols]).astype(o_ref.dtype)


def _s5_post(zp, zs, w_glu, b_glu, norm_g, tm, tn):
    d = zp.shape[1]
    n_first = zp.shape[0] // tm
    nblk = n_first + zs.shape[0] // tm
    nj = d // tn

    def first_row(i):
        return jnp.minimum(i, n_first - 1)

    def second_row(i):
        return jnp.maximum(i - n_first, 0)

    return pl.pallas_call(
        functools.partial(_s5_post_kernel, n_first=n_first, tn=tn),
        grid=(nblk, nj),
        in_specs=[pl.BlockSpec((tm, d), lambda i, j: (first_row(i), 0)),
                  pl.BlockSpec((tm, d), lambda i, j: (second_row(i), 0)),
                  pl.BlockSpec((tm, tn), lambda i, j: (first_row(i), jnp.where(i < n_first, j, nj - 1))),
                  pl.BlockSpec((tm, tn), lambda i, j: (second_row(i), jnp.where(i < n_first, 0, j))),
                  pl.BlockSpec((d, tn), lambda i, j: (0, j)),
                  pl.BlockSpec((1, tn), lambda i, j: (0, j)),
                  pl.BlockSpec((1, d), lambda i, j: (0, 0))],
        out_specs=pl.BlockSpec((tm, d), lambda i, j: (i, 0)),
        out_shape=jax.ShapeDtypeStruct((nblk * tm, d), BF16),
        scratch_shapes=[pltpu.VMEM((tm, d), BF16), pltpu.VMEM((nj, tm, tn), F32),
                        pltpu.VMEM((tm, 1), F32)],
        compiler_params=_params("parallel", "arbitrary"),
        name="s5_post",
    )(zp, zs, zp, zs, w_glu, b_glu.reshape(1, d), norm_g.reshape(1, d))


def kernel(x_prompt, x_sample, state_ret, state_s5_re, state_s5_im, meta_tokens, norm1_g, w_in, ret_gn_g, s5_lam_re, s5_lam_im, s5_log_dt, s5_b_re, s5_b_im, s5_c_re, s5_c_im, s5_d, w_glu, b_glu, s5_norm_g, w_out, norm2_g, w_gate, w_up, w_down, final_norm_g):
    assert norm1_g.shape[0] == 1, "single-layer model"
    batch, seq, d = x_prompt.shape
    dbatch, dseq, _ = x_sample.shape
    rows_p, rows_s = batch * seq, dbatch * dseq
    tm = 1024
    assert rows_p % tm == 0 and rows_s % tm == 0 and seq % RET_CHUNK == 0
    xp = x_prompt.reshape(rows_p, d)
    xs = x_sample.reshape(rows_s, d)
    hd = RET_HEAD_DIM

    ar, ai, bbr, bbi = _s5_discretize(s5_lam_re[0], s5_lam_im[0], s5_log_dt[0], s5_b_re[0], s5_b_im[0])
    ar = ar.reshape(1, S5_NSTATE)
    ai = ai.reshape(1, S5_NSTATE)
    s5_w = (ar, ai, bbr, bbi, jnp.swapaxes(s5_c_re[0], 1, 2), jnp.swapaxes(s5_c_im[0], 1, 2), s5_d[0])

    h_meta = _rmsnorm([meta_tokens], norm1_g[0], N_META, BF16)
    h = _rmsnorm([xp, xs], norm1_g[0], 512, BF16)
    proj, proj_meta = _inproj(h, h_meta, w_in[0], 1536, 512)

    zero_ret = jnp.zeros((1, RET_HEADS, hd, hd), F32)
    _, s_meta = _retention_seq(proj_meta, 0, 1, N_META, N_META, 0, zero_ret, ret_gn_g[0])
    zero_s5 = jnp.zeros((1, S5_NSTATE), F32)
    _, mr, mi = _s5(proj_meta, 0, 1, N_META, N_META, zero_s5, zero_s5, *s5_w)

    ret_p, sret_p = _retention_seq(proj, 0, batch, seq, RET_CHUNK, N_META, s_meta, ret_gn_g[0])
    ret_s, sret_s = _retention_step(proj, rows_p, dbatch, dseq, PAST_LEN, state_ret[0],
                                    ret_gn_g[0], 4)

    z_p, s5r_p, s5i_p = _s5(proj, 0, batch, seq, 256,
                            jnp.broadcast_to(mr, (batch, S5_NSTATE)),
                            jnp.broadcast_to(mi, (batch, S5_NSTATE)), *s5_w)
    z_s, s5r_s, s5i_s = _s5(proj, rows_p // rows_s, dbatch, dseq, dseq,
                            state_s5_re[0].reshape(dbatch, S5_NSTATE),
                            state_s5_im[0].reshape(dbatch, S5_NSTATE), *s5_w)
    s5_out = _s5_post(z_p, z_s, w_glu[0], b_glu[0], s5_norm_g[0], tm, 512)

    x1, x1g, x1_ssq = _outproj(ret_p, ret_s, s5_out, w_out[0].astype(BF16), xp, xs, norm2_g[0],
                               tm, 512)
    a = _ffn_up(x1g, x1_ssq, w_gate[0], w_up[0], tm, 256)
    x2 = _ffn_down(a, w_down[0], x1, 0, 2, tm, 256)
    x2 = _ffn_down(a, w_down[0], x2, 1, 2, tm, 256)
    y_p, y_s = _final_norm(x2, final_norm_g, rows_p, 512)

    st = (1, -1, S5_GROUPS, S5_STATE)
    return (y_p.reshape(batch, seq, d), y_s.reshape(dbatch, dseq, d),
            sret_p[None], s5r_p.reshape(st), s5i_p.reshape(st),
            sret_s[None], s5r_s.reshape(st), s5i_s.reshape(st))
```

```python
import functools
import math

import numpy as np
import jax
import jax.numpy as jnp
from jax import lax
from jax.experimental import pallas as pl
from jax.experimental.pallas import tpu as pltpu

D_MODEL = 4096
N_META = 16
PAST_LEN = 16384
RET_WIDTH = D_MODEL // 2
S5_WIDTH = D_MODEL - RET_WIDTH
RET_HEADS = 8
RET_HEAD_DIM = RET_WIDTH // RET_HEADS
RET_CHUNK = 128
ROPE_BASE = 10000.0
S5_GROUP = 16
S5_GROUPS = S5_WIDTH // S5_GROUP
S5_STATE = 64
S5_NSTATE = S5_GROUPS * S5_STATE
IN_COLS = 4 * RET_WIDTH + S5_WIDTH
EPS = 1e-6
GN_EPS = 1e-5

LANES = 128
S5_GB = 16
S5_CH = S5_GB * S5_GROUP
S5_ST = S5_GB * S5_STATE
S5_SLABS = 2 * S5_ST // LANES
VMEM_LIMIT = 56 * 1024 * 1024

BF16 = jnp.bfloat16
F32 = jnp.float32


def _params(*sem):
    return pltpu.CompilerParams(dimension_semantics=sem, vmem_limit_bytes=VMEM_LIMIT)


def _dot(a, b):
    return jnp.dot(a, b, preferred_element_type=F32)


def _rmsnorm_kernel(*refs, n_src, n_first):
    x_refs, g_ref, o_ref = refs[:n_src], refs[n_src], refs[n_src + 1]
    x = x_refs[0][...]
    if n_src == 2:
        x = jnp.where(pl.program_id(0) < n_first, x, x_refs[1][...])
    r = lax.rsqrt(jnp.mean(x * x, axis=-1, keepdims=True) + EPS)
    o_ref[...] = (x * r * g_ref[...]).astype(o_ref.dtype)


def _rmsnorm(srcs, g, tm, out_dtype):
    d = srcs[0].shape[1]
    nblk = [s.shape[0] // tm for s in srcs]
    n_first = nblk[0]
    specs = [pl.BlockSpec((tm, d), lambda i: (jnp.minimum(i, n_first - 1), 0))]
    if len(srcs) == 2:
        specs.append(pl.BlockSpec((tm, d), lambda i: (jnp.maximum(i - n_first, 0), 0)))
    rows = sum(s.shape[0] for s in srcs)
    return pl.pallas_call(
        functools.partial(_rmsnorm_kernel, n_src=len(srcs), n_first=n_first),
        grid=(sum(nblk),),
        in_specs=specs + [pl.BlockSpec((1, d), lambda i: (0, 0))],
        out_specs=pl.BlockSpec((tm, d), lambda i: (i, 0)),
        out_shape=jax.ShapeDtypeStruct((rows, d), out_dtype),
        compiler_params=_params("parallel"),
        name="rmsnorm",
    )(*srcs, g.reshape(1, d))


def _final_norm_kernel(x_ref, g_ref, yp_ref, ys_ref, *, n_first):
    x = x_ref[...]
    r = lax.rsqrt(jnp.mean(x * x, axis=-1, keepdims=True) + EPS)
    y = x * r * g_ref[...]
    i = pl.program_id(0)

    @pl.when(i < n_first)
    def _():
        yp_ref[...] = y

    @pl.when(i >= n_first)
    def _():
        ys_ref[...] = y


def _final_norm(x, g, rows_p, tm):
    rows, d = x.shape
    n_first = rows_p // tm
    return pl.pallas_call(
        functools.partial(_final_norm_kernel, n_first=n_first),
        grid=(rows // tm,),
        in_specs=[pl.BlockSpec((tm, d), lambda i: (i, 0)),
                  pl.BlockSpec((1, d), lambda i: (0, 0))],
        out_specs=[pl.BlockSpec((tm, d), lambda i: (jnp.minimum(i, n_first - 1), 0)),
                   pl.BlockSpec((tm, d), lambda i: (jnp.maximum(i - n_first, 0), 0))],
        out_shape=[jax.ShapeDtypeStruct((rows_p, d), F32),
                   jax.ShapeDtypeStruct((rows - rows_p, d), F32)],
        compiler_params=_params("arbitrary"),
        name="final_norm",
    )(x, g.reshape(1, d))


def _inproj_kernel(h_ref, hx_ref, w_ref, o_ref, ox_ref):
    w = w_ref[...].astype(BF16)
    o_ref[...] = _dot(h_ref[...], w)

    @pl.when(pl.program_id(0) == 0)
    def _():
        ox_ref[...] = _dot(hx_ref[...], w)


def _inproj(h, h_extra, w, tm, tn):
    m, k = h.shape
    mx = h_extra.shape[0]
    n = w.shape[1]
    nj = n // tn
    return pl.pallas_call(
        _inproj_kernel,
        grid=(m // tm, nj),
        in_specs=[pl.BlockSpec((tm, k), lambda i, j: (i, 0)),
                  pl.BlockSpec((mx, k), lambda i, j: (0, 0)),
                  pl.BlockSpec((k, tn), lambda i, j: (0, j))],
        out_specs=[pl.BlockSpec((tm, tn), lambda i, j: (i, j)),
                   pl.BlockSpec((mx, tn), lambda i, j: (0, jnp.where(i == 0, j, nj - 1)))],
        out_shape=[jax.ShapeDtypeStruct((m, n), F32), jax.ShapeDtypeStruct((mx, n), F32)],
        compiler_params=_params("arbitrary", "arbitrary"),
        name="inproj",
    )(h, h_extra, w)


def _outproj_kernel(retp_ref, rets_ref, s5_ref, wt_ref, wb_ref, xp_ref, xs_ref, g_ref,
                    o_ref, xg_ref, ssq_ref, *, n_first):
    first = pl.program_id(0) < n_first
    j = pl.program_id(1)
    x = jnp.where(first, xp_ref[...], xs_ref[...])
    ret = jnp.where(first, retp_ref[...], rets_ref[...])
    acc = _dot(ret, wt_ref[...].astype(BF16))
    acc += _dot(s5_ref[...], wb_ref[...].astype(BF16))
    x1 = x + acc
    o_ref[...] = x1
    xg_ref[...] = (x1 * g_ref[...]).astype(xg_ref.dtype)
    sq = x1 * x1
    part = sq[:, :LANES]
    for c in range(1, sq.shape[1] // LANES):
        part += sq[:, c * LANES:(c + 1) * LANES]

    @pl.when(j == 0)
    def _():
        ssq_ref[...] = part

    @pl.when(j != 0)
    def _():
        ssq_ref[...] += part


def _outproj(ret_p, ret_s, s5, w_out, xp, xs, g, tm, tn):
    m, kh = s5.shape
    n = w_out.shape[1]
    n_first = xp.shape[0] // tm
    nj = n // tn
    return pl.pallas_call(
        functools.partial(_outproj_kernel, n_first=n_first),
        grid=(m // tm, nj),
        in_specs=[pl.BlockSpec((tm, kh), lambda i, j: (jnp.minimum(i, n_first - 1), 0)),
                  pl.BlockSpec((tm, kh), lambda i, j: (jnp.maximum(i - n_first, 0), 0),
                               pipeline_mode=pl.Buffered(1)),
                  pl.BlockSpec((tm, kh), lambda i, j: (i, 0)),
                  pl.BlockSpec((kh, tn), lambda i, j: (0, j)),
                  pl.BlockSpec((kh, tn), lambda i, j: (1, j)),
                  pl.BlockSpec((tm, tn), lambda i, j: (jnp.minimum(i, n_first - 1),
                                                       jnp.where(i < n_first, j, nj - 1))),
                  pl.BlockSpec((tm, tn), lambda i, j: (jnp.maximum(i - n_first, 0),
                                                       jnp.where(i < n_first, 0, j)),
                               pipeline_mode=pl.Buffered(1)),
                  pl.BlockSpec((1, tn), lambda i, j: (0, j))],
        out_specs=[pl.BlockSpec((tm, tn), lambda i, j: (i, j)),
                   pl.BlockSpec((tm, tn), lambda i, j: (i, j)),
                   pl.BlockSpec((tm, LANES), lambda i, j: (i, 0))],
        out_shape=[jax.ShapeDtypeStruct((m, n), F32),
                   jax.ShapeDtypeStruct((m, n), BF16),
                   jax.ShapeDtypeStruct((m, LANES), F32)],
        compiler_params=_params("parallel", "arbitrary"),
        name="outproj",
    )(ret_p, ret_s, s5, w_out, w_out, xp, xs, g.reshape(1, n))


def _ffn_up_kernel(xg_ref, ssq_ref, wg_ref, wu_ref, o_ref, r_scr):
    tf = wg_ref.shape[1]

    @pl.when(pl.program_id(1) == 0)
    def _():
        r_scr[...] = lax.rsqrt(jnp.sum(ssq_ref[...], axis=-1, keepdims=True) / xg_ref.shape[1] + EPS)

    w = jnp.concatenate([wg_ref[...].astype(BF16), wu_ref[...].astype(BF16)], axis=1)
    gu = _dot(xg_ref[...], w) * r_scr[...]
    o_ref[...] = (jax.nn.silu(gu[:, :tf]) * gu[:, tf:]).astype(o_ref.dtype)


def _ffn_up(xg, ssq, w_gate, w_up, tm, tf):
    m, k = xg.shape
    f = w_gate.shape[1]
    return pl.pallas_call(
        _ffn_up_kernel,
        grid=(m // tm, f // tf),
        in_specs=[pl.BlockSpec((tm, k), lambda i, j: (i, 0)),
                  pl.BlockSpec((tm, LANES), lambda i, j: (i, 0)),
                  pl.BlockSpec((k, tf), lambda i, j: (0, j)),
                  pl.BlockSpec((k, tf), lambda i, j: (0, j))],
        out_specs=pl.BlockSpec((tm, tf), lambda i, j: (i, j)),
        out_shape=jax.ShapeDtypeStruct((m, f), BF16),
        scratch_shapes=[pltpu.VMEM((tm, 1), F32)],
        compiler_params=_params("parallel", "arbitrary"),
        name="ffn_up",
    )(xg, ssq, w_gate, w_up)


def _ffn_down_kernel(a_ref, w_ref, x_ref, o_ref):
    o_ref[...] = x_ref[...] + _dot(a_ref[...], w_ref[...].astype(BF16))


def _ffn_down(a, w_down, x, k0, tk, tm, tn):
    m, f = a.shape
    n = w_down.shape[1]
    return pl.pallas_call(
        _ffn_down_kernel,
        grid=(m // tm, n // tn),
        in_specs=[pl.BlockSpec((pl.Element(tm), pl.Element(tk)), lambda i, j: (i * tm, k0)),
                  pl.BlockSpec((pl.Element(tk), pl.Element(tn)), lambda i, j: (k0, j * tn)),
                  pl.BlockSpec((tm, tn), lambda i, j: (i, j))],
        out_specs=pl.BlockSpec((tm, tn), lambda i, j: (i, j)),
        out_shape=jax.ShapeDtypeStruct((m, n), F32),
        compiler_params=_params("parallel", "arbitrary"),
        name="ffn_down",
    )(a, w_down, x)


def _rotary(x, cos, sin):
    half = RET_HEAD_DIM // 2
    x1, x2 = x[:, :half], x[:, half:]
    return jnp.concatenate([x1 * cos - x2 * sin, x1 * sin + x2 * cos], axis=-1)


def _retention_chunk(lg, sdec, s, q, k, v, g, cos, sin, gn):
    n = q.shape[0]
    ri = lax.broadcasted_iota(jnp.int32, (n, n), 0)
    ci = lax.broadcasted_iota(jnp.int32, (n, n), 1)
    diff = (ri - ci).astype(F32)
    mask = jnp.where(diff >= 0, jnp.exp(jnp.maximum(diff, 0.0) * lg), 0.0)
    row = lax.broadcasted_iota(jnp.int32, (n, 1), 0).astype(F32)
    q_dec = jnp.exp(lg * (row + 1.0))
    k_dec = jnp.exp(lg * (n - 1.0 - row))

    qr = _rotary(q, cos, sin)
    kr = _rotary(k, cos, sin) * (RET_HEAD_DIM ** -0.5)
    vb = v.astype(BF16)
    scores = lax.dot_general(qr.astype(BF16), kr.astype(BF16), (((1,), (1,)), ((), ())),
                             preferred_element_type=F32) * mask
    o = _dot(scores.astype(BF16), vb) + _dot((qr * q_dec).astype(BF16), s.astype(BF16))
    s_new = sdec * s + lax.dot_general((kr * k_dec).astype(BF16), vb, (((0,), (0,)), ((), ())),
                                       preferred_element_type=F32)
    mu = jnp.mean(o, axis=-1, keepdims=True)
    oc = o - mu
    var = jnp.mean(oc * oc, axis=-1, keepdims=True)
    y = oc * lax.rsqrt(var + GN_EPS) * gn
    return y * jax.nn.silu(g), s_new


def _ret_seq_kernel(dec_ref, q_ref, k_ref, v_ref, g_ref, cos_ref, sin_ref, s0_ref, gn_ref,
                    o_ref, sf_ref, s_scr, *, chunk, nchunks):
    h = pl.program_id(1)
    lg = dec_ref[0, h]
    sdec = dec_ref[1, h]
    s_scr[...] = s0_ref[...]
    gn = gn_ref[...]

    def body(c, carry):
        rows = pl.ds(pl.multiple_of(c * chunk, chunk), chunk)
        y, s_new = _retention_chunk(lg, sdec, s_scr[...], q_ref[rows, :], k_ref[rows, :],
                                    v_ref[rows, :], g_ref[rows, :], cos_ref[rows, :],
                                    sin_ref[rows, :], gn)
        o_ref[rows, :] = y.astype(o_ref.dtype)
        s_scr[...] = s_new
        return carry

    lax.fori_loop(0, nchunks, body, 0, unroll=min(8, nchunks))
    sf_ref[...] = s_scr[...]


def _ret_decay_consts(chunk):
    lg = np.log(1.0 - 2.0 ** (-5.0 - np.arange(RET_HEADS, dtype=np.float64)))
    return jnp.asarray(np.stack([lg, np.exp(lg * chunk)]), dtype=F32)


def _rope_tables(pos):
    half = RET_HEAD_DIM // 2
    inv = ROPE_BASE ** (-np.arange(half, dtype=np.float64) / half)
    ang = np.asarray(pos, dtype=np.float64)[:, None] * inv[None, :]
    return jnp.asarray(np.cos(ang), dtype=F32), jnp.asarray(np.sin(ang), dtype=F32)


def _retention_seq(proj, row_block0, nbatch, seq, chunk, pos0, s0, gn_g):
    cos, sin = _rope_tables(pos0 + np.arange(seq))
    hd = RET_HEAD_DIM

    def col(off):
        return pl.BlockSpec((seq, hd), lambda b, h: (row_block0 + b, off + h))

    tbl = pl.BlockSpec((seq, hd // 2), lambda b, h: (0, 0))
    return pl.pallas_call(
        functools.partial(_ret_seq_kernel, chunk=chunk, nchunks=seq // chunk),
        grid=(nbatch, RET_HEADS),
        in_specs=[pl.BlockSpec(memory_space=pltpu.SMEM),
                  col(0), col(RET_HEADS), col(2 * RET_HEADS), col(3 * RET_HEADS), tbl, tbl,
                  pl.BlockSpec((None, None, hd, hd), lambda b, h: (0, h, 0, 0)),
                  pl.BlockSpec((1, hd), lambda b, h: (0, h))],
        out_specs=[pl.BlockSpec((seq, hd), lambda b, h: (b, h)),
                   pl.BlockSpec((None, None, hd, hd), lambda b, h: (b, h, 0, 0))],
        out_shape=[jax.ShapeDtypeStruct((nbatch * seq, RET_WIDTH), BF16),
                   jax.ShapeDtypeStruct((nbatch, RET_HEADS, hd, hd), F32)],
        scratch_shapes=[pltpu.VMEM((hd, hd), F32)],
        compiler_params=_params("parallel", "arbitrary"),
        name="retention_seq",
    )(_ret_decay_consts(chunk), proj, proj, proj, proj, cos, sin, s0, gn_g.reshape(1, RET_WIDTH))


def _ret_step_kernel(dec_ref, q_ref, k_ref, v_ref, g_ref, cos_ref, sin_ref, s_ref, gn_ref,
                     o_ref, sn_ref, *, nb, seq):
    hd = RET_HEAD_DIM
    cos, sin = cos_ref[...], sin_ref[...]
    for b in range(nb):
        rows = slice(b * seq, (b + 1) * seq)
        for h in range(RET_HEADS):
            cols = slice(h * hd, (h + 1) * hd)
            y, s_new = _retention_chunk(dec_ref[0, h], dec_ref[1, h], s_ref[b, h],
                                        q_ref[rows, cols], k_ref[rows, cols], v_ref[rows, cols],
                                        g_ref[rows, cols], cos, sin, gn_ref[:, cols])
            o_ref[rows, cols] = y.astype(o_ref.dtype)
            sn_ref[b, h] = s_new


def _retention_step(proj, row0, nbatch, seq, pos0, state, gn_g, nb):
    cos, sin = _rope_tables(pos0 + np.arange(seq))
    hd = RET_HEAD_DIM
    rb0 = row0 // (nb * seq)

    def col(off):
        return pl.BlockSpec((nb * seq, RET_WIDTH), lambda i: (rb0 + i, off))

    tbl = pl.BlockSpec((seq, hd // 2), lambda i: (0, 0))
    st = pl.BlockSpec((nb, RET_HEADS, hd, hd), lambda i: (i, 0, 0, 0))
    return pl.pallas_call(
        functools.partial(_ret_step_kernel, nb=nb, seq=seq),
        grid=(nbatch // nb,),
        in_specs=[pl.BlockSpec(memory_space=pltpu.SMEM),
                  col(0), col(1), col(2), col(3), tbl, tbl, st,
                  pl.BlockSpec((1, RET_WIDTH), lambda i: (0, 0))],
        out_specs=[pl.BlockSpec((nb * seq, RET_WIDTH), lambda i: (i, 0)), st],
        out_shape=[jax.ShapeDtypeStruct((nbatch * seq, RET_WIDTH), BF16),
                   jax.ShapeDtypeStruct(state.shape, F32)],
        compiler_params=_params("parallel"),
        name="retention_step",
    )(_ret_decay_consts(seq), proj, proj, proj, proj, cos, sin, state,
      gn_g.reshape(1, RET_WIDTH))


def _s5_disc_kernel(lr_ref, li_ref, ldt_ref, br_ref, bi_ref, ar_ref, ai_ref, bbr_ref, bbi_ref):
    lr, li = lr_ref[...], li_ref[...]
    dt = jnp.exp(ldt_ref[...])
    mag = jnp.exp(lr * dt)
    ar = mag * jnp.cos(li * dt)
    ai = mag * jnp.sin(li * dt)
    nr, ni = ar - 1.0, ai
    den = lr * lr + li * li
    fr = (nr * lr + ni * li) / den
    fi = (ni * lr - nr * li) / den
    br, bi = br_ref[...], bi_ref[...]
    ar_ref[...] = ar
    ai_ref[...] = ai
    bbr_ref[...] = fr * br - fi * bi
    bbi_ref[...] = fr * bi + fi * br


def _s5_discretize(lam_re, lam_im, log_dt, b_re, b_im):
    g, p, c = b_re.shape
    shp3 = jax.ShapeDtypeStruct((g, 1, p), F32)
    shpb = jax.ShapeDtypeStruct((g, c, p), F32)
    return pl.pallas_call(
        _s5_disc_kernel,
        out_shape=[shp3, shp3, shpb, shpb],
        name="s5_discretize",
    )(lam_re.reshape(g, 1, p), lam_im.reshape(g, 1, p), log_dt.reshape(g, 1, 1),
      jnp.swapaxes(b_re, 1, 2), jnp.swapaxes(b_im, 1, 2))


S5_NPAIR = S5_SLABS // 4


def _lanes(s):
    return slice(s * LANES, (s + 1) * LANES)


def _s5_place(s, nb, pitch):
    comp, ls = s // (2 * S5_NPAIR), s % (2 * S5_NPAIR)
    hf, p = ls // S5_NPAIR, ls % S5_NPAIR
    return comp * S5_NPAIR + p, hf * (nb * pitch)


def _s5_paired(ref, p, nb, order=None):
    def slab(hf):
        cols = _lanes(p + hf * S5_NPAIR)
        if order is None or ref.shape[0] == 1:
            return jnp.broadcast_to(ref[:, cols], (nb, LANES))
        return jnp.concatenate([ref[o:o + 1, cols] for o in order], axis=0)
    return jnp.concatenate([slab(0), slab(1)], axis=0)


def _s5_block_weights(bbr_ref, bbi_ref, cr_ref, ci_ref, bw_scr, cw_scr):
    bw_scr[...] = jnp.zeros_like(bw_scr)
    cw_scr[...] = jnp.zeros_like(cw_scr)
    for g in range(S5_GB):
        ch = slice(g * S5_GROUP, (g + 1) * S5_GROUP)
        for comp, (b_ref, c_ref, sign) in enumerate(((bbr_ref, cr_ref, 1.0), (bbi_ref, ci_ref, -1.0))):
            st = slice(comp * S5_ST + g * S5_STATE, comp * S5_ST + (g + 1) * S5_STATE)
            bw_scr[ch, st] = b_ref[g]
            cw_scr[st, ch] = sign * c_ref[g]


def _s5_step(buf, t, xs, ar, ai, nb, pitch):
    rows = pl.ds(t, 2 * nb, stride=pitch)
    new = [None] * (2 * S5_NPAIR)
    for p in range(S5_NPAIR):
        xr, xi = xs[p], xs[S5_NPAIR + p]
        nr = ar[p] * xr - ai[p] * xi + buf[p, rows, :]
        ni = ar[p] * xi + ai[p] * xr + buf[S5_NPAIR + p, rows, :]
        buf[p, rows, :] = nr
        buf[S5_NPAIR + p, rows, :] = ni
        new[p], new[S5_NPAIR + p] = nr, ni
    return tuple(new)


def _s5_scan_io(x0r_ref, x0i_ref, ar_ref, ai_ref, nb, order=None):
    ar = [_s5_paired(ar_ref, p, nb) for p in range(S5_NPAIR)]
    ai = [_s5_paired(ai_ref, p, nb) for p in range(S5_NPAIR)]
    xs0 = tuple([_s5_paired(x0r_ref, p, nb, order) for p in range(S5_NPAIR)]
                + [_s5_paired(x0i_ref, p, nb, order) for p in range(S5_NPAIR)])
    return ar, ai, xs0


def _s5_store_final(xs, xfr_ref, xfi_ref, nb, order=None):
    for p in range(S5_NPAIR):
        for hf in range(2):
            cols = _lanes(p + hf * S5_NPAIR)
            for x, ref in ((xs[p], xfr_ref), (xs[S5_NPAIR + p], xfi_ref)):
                if order is None:
                    ref[:, cols] = x[hf * nb:(hf + 1) * nb]
                else:
                    for i, o in enumerate(order):
                        ref[o:o + 1, cols] = x[hf * nb + i:hf * nb + i + 1]


def _s5_kernel(u_ref, x0r_ref, x0i_ref, ar_ref, ai_ref, bbr_ref, bbi_ref, cr_ref, ci_ref, d_ref,
               z_ref, xfr_ref, xfi_ref, buf, bw_scr, cw_scr, *, nb, seq):
    _s5_block_weights(bbr_ref, bbi_ref, cr_ref, ci_ref, bw_scr, cw_scr)
    ar, ai, xs = _s5_scan_io(x0r_ref, x0i_ref, ar_ref, ai_ref, nb)
    u = u_ref[...]
    bu = _dot(u.astype(BF16), bw_scr[...].astype(BF16))
    for s in range(S5_SLABS):
        q, off = _s5_place(s, nb, seq)
        buf[q, off:off + nb * seq, :] = bu[:, _lanes(s)]
    xs = lax.fori_loop(0, seq, lambda t, xs: _s5_step(buf, t, xs, ar, ai, nb, seq), xs,
                       unroll=8 if nb <= 8 else 1)
    parts = []
    for s in range(S5_SLABS):
        q, off = _s5_place(s, nb, seq)
        parts.append(buf[q, off:off + nb * seq, :])
    y = _dot(jnp.concatenate(parts, axis=1).astype(BF16), cw_scr[...].astype(BF16)) + d_ref[...] * u
    z_ref[...] = jax.nn.gelu(y).astype(z_ref.dtype)
    _s5_store_final(xs, xfr_ref, xfi_ref, nb)


def _s5_chunked_kernel(u_ref, x0r_ref, x0i_ref, ar_ref, ai_ref, bbr_ref, bbi_ref, cr_ref, ci_ref,
                       d_ref, z_ref, xfr_ref, xfi_ref, buf, bw_scr, cw_scr, *, nb, seq, tc, pitch):
    lead_odd = pitch % 8
    assert nb % 2 == 0 and nb >= 4 and lead_odd == 4 and (nb * pitch) % 8 == 0
    inner = list(range(1, nb - 1))
    even, odd = [0, nb - 1] + inner[nb // 2:], inner[:nb // 2]
    order = [(odd if i % 2 else even)[i // 2] for i in range(nb)]

    _s5_block_weights(bbr_ref, bbi_ref, cr_ref, ci_ref, bw_scr, cw_scr)
    bw = bw_scr[...].astype(BF16)
    cw = cw_scr[...].astype(BF16)
    ar, ai, xs = _s5_scan_io(x0r_ref, x0i_ref, ar_ref, ai_ref, nb, order)
    dvec = d_ref[...]

    def chunk(c, xs):
        t0 = c * tc
        rows = [pl.ds(pl.multiple_of(order[i] * seq + t0, tc), tc) for i in range(nb)]
        lead = [lead_odd if i % 2 else 0 for i in range(nb)]

        def u_window(i):
            if not lead[i]:
                return u_ref[rows[i], :]
            wide = u_ref[pl.ds(pl.multiple_of(order[i] * seq + t0 - 8, 8), tc + 16), :]
            return wide[8 - lead[i]:8 + tc + lead[i]]

        for i in range(nb):
            bu = _dot(u_window(i).astype(BF16), bw)
            for s in range(S5_SLABS):
                q, off = _s5_place(s, nb, pitch)
                r0 = off + i * pitch - lead[i]
                buf[q, r0:r0 + tc + 2 * lead[i], :] = bu[:, _lanes(s)]
        xs = lax.fori_loop(0, tc, lambda t, xs: _s5_step(buf, t, xs, ar, ai, nb, pitch), xs,
                           unroll=8)
        for i in range(nb):
            parts = []
            for s in range(S5_SLABS):
                q, off = _s5_place(s, nb, pitch)
                r0 = off + i * pitch - lead[i]
                parts.append(buf[q, r0:r0 + tc + 2 * lead[i], :])
            y = _dot(jnp.concatenate(parts, axis=1).astype(BF16), cw)[lead[i]:lead[i] + tc]
            y = y + dvec * u_ref[rows[i], :]
            z_ref[rows[i], :] = jax.nn.gelu(y).astype(z_ref.dtype)
        return xs

    xs = lax.fori_loop(0, seq // tc, chunk, xs)
    _s5_store_final(xs, xfr_ref, xfi_ref, nb, order)


def _s5(proj, row_block0, nb, seq, tc, x0r, x0i, ar, ai, bbr, bbi, c_re, c_im, d):
    rows = nb * seq
    u_col0 = 4 * RET_WIDTH // S5_CH
    nj = S5_GROUPS // S5_GB
    st = pl.BlockSpec((nb, S5_ST), lambda j: (0, j))
    coef = pl.BlockSpec((1, S5_ST), lambda j: (0, j))
    bspec = pl.BlockSpec((S5_GB, S5_GROUP, S5_STATE), lambda j: (j, 0, 0))
    cspec = pl.BlockSpec((S5_GB, S5_STATE, S5_GROUP), lambda j: (j, 0, 0))
    weights = [pltpu.VMEM((S5_CH, 2 * S5_ST), F32), pltpu.VMEM((2 * S5_ST, S5_CH), F32)]
    if tc == seq:
        body = functools.partial(_s5_kernel, nb=nb, seq=seq)
        pitch = seq
    else:
        pitch = tc + 4
        body = functools.partial(_s5_chunked_kernel, nb=nb, seq=seq, tc=tc, pitch=pitch)
    scratch = [pltpu.VMEM((S5_SLABS // 2, 2 * nb * pitch, LANES), F32)] + weights
    return pl.pallas_call(
        body,
        grid=(nj,),
        in_specs=[pl.BlockSpec((rows, S5_CH), lambda j: (row_block0, u_col0 + j)),
                  st, st, coef, coef, bspec, bspec, cspec, cspec,
                  pl.BlockSpec((1, S5_CH), lambda j: (0, j))],
        out_specs=[pl.BlockSpec((rows, S5_CH), lambda j: (0, j)), st, st],
        out_shape=[jax.ShapeDtypeStruct((rows, S5_WIDTH), BF16),
                   jax.ShapeDtypeStruct((nb, S5_NSTATE), F32),
                   jax.ShapeDtypeStruct((nb, S5_NSTATE), F32)],
        scratch_shapes=scratch,
        compiler_params=_params("parallel"),
        name="s5_scan",
    )(proj, x0r, x0i, ar, ai, bbr, bbi, c_re, c_im, d.reshape(1, S5_WIDTH))


def _s5_post_kernel(zp_ref, zs_ref, zpc_ref, zsc_ref, w_ref, b_ref, g_ref, o_ref,
                    zb_scr, gate_scr, ssq_scr, *, n_first, tn):
    i, j = pl.program_id(0), pl.program_id(1)
    first = i < n_first

    @pl.when(j == 0)
    def _():
        zb_scr[...] = jnp.where(first, zp_ref[...], zs_ref[...])
        ssq_scr[...] = jnp.zeros_like(ssq_scr)

    zc = jnp.where(first, zpc_ref[...], zsc_ref[...]).astype(F32)
    t = _dot(zb_scr[...], w_ref[...].astype(BF16)) + b_ref[...]
    s = zc * jax.nn.sigmoid(t)
    gate_scr[j] = s
    ssq_scr[...] += jnp.sum(s * s, axis=-1, keepdims=True)

    @pl.when(j == pl.num_programs(1) - 1)
    def _():
        r = lax.rsqrt(ssq_scr[...] / (tn * gate_scr.shape[0]) + EPS)
        for jj in range(gate_scr.shape[0]):
            cols = slice(jj * tn, (jj + 1) * tn)
            o_ref[:, cols] = (gate_scr[jj] * r * g_ref[:, cols]).astype(o_ref.dtype)


def _s5_post(zp, zs, w_glu, b_glu, norm_g, tm, tn):
    d = zp.shape[1]
    n_first = zp.shape[0] // tm
    nblk = n_first + zs.shape[0] // tm
    nj = d // tn

    def first_row(i):
        return jnp.minimum(i, n_first - 1)

    def second_row(i):
        return jnp.maximum(i - n_first, 0)

    return pl.pallas_call(
        functools.partial(_s5_post_kernel, n_first=n_first, tn=tn),
        grid=(nblk, nj),
        in_specs=[pl.BlockSpec((tm, d), lambda i, j: (first_row(i), 0)),
                  pl.BlockSpec((tm, d), lambda i, j: (second_row(i), 0)),
                  pl.BlockSpec((tm, tn), lambda i, j: (first_row(i), jnp.where(i < n_first, j, nj - 1))),
                  pl.BlockSpec((tm, tn), lambda i, j: (second_row(i), jnp.where(i < n_first, 0, j))),
                  pl.BlockSpec((d, tn), lambda i, j: (0, j)),
                  pl.BlockSpec((1, tn), lambda i, j: (0, j)),
                  pl.BlockSpec((1, d), lambda i, j: (0, 0))],
        out_specs=pl.BlockSpec((tm, d), lambda i, j: (i, 0)),
        out_shape=jax.ShapeDtypeStruct((nblk * tm, d), BF16),
        scratch_shapes=[pltpu.VMEM((tm, d), BF16), pltpu.VMEM((nj, tm, tn), F32),
                        pltpu.VMEM((tm, 1), F32)],
        compiler_params=_params("parallel", "arbitrary"),
        name="s5_post",
    )(zp, zs, zp, zs, w_glu, b_glu.reshape(1, d), norm_g.reshape(1, d))


def kernel(x_prompt, x_sample, state_ret, state_s5_re, state_s5_im, meta_tokens, norm1_g, w_in, ret_gn_g, s5_lam_re, s5_lam_im, s5_log_dt, s5_b_re, s5_b_im, s5_c_re, s5_c_im, s5_d, w_glu, b_glu, s5_norm_g, w_out, norm2_g, w_gate, w_up, w_down, final_norm_g):
    assert norm1_g.shape[0] == 1, "single-layer model"
    batch, seq, d = x_prompt.shape
    dbatch, dseq, _ = x_sample.shape
    rows_p, rows_s = batch * seq, dbatch * dseq
    tm = 1024
    assert rows_p % tm == 0 and rows_s % tm == 0 and seq % RET_CHUNK == 0
    xp = x_prompt.reshape(rows_p, d)
    xs = x_sample.reshape(rows_s, d)
    hd = RET_HEAD_DIM

    ar, ai, bbr, bbi = _s5_discretize(s5_lam_re[0], s5_lam_im[0], s5_log_dt[0], s5_b_re[0], s5_b_im[0])
    ar = ar.reshape(1, S5_NSTATE)
    ai = ai.reshape(1, S5_NSTATE)
    s5_w = (ar, ai, bbr, bbi, jnp.swapaxes(s5_c_re[0], 1, 2), jnp.swapaxes(s5_c_im[0], 1, 2), s5_d[0])

    h_meta = _rmsnorm([meta_tokens], norm1_g[0], N_META, BF16)
    h = _rmsnorm([xp, xs], norm1_g[0], 512, BF16)
    proj, proj_meta = _inproj(h, h_meta, w_in[0], 1536, 512)

    zero_ret = jnp.zeros((1, RET_HEADS, hd, hd), F32)
    _, s_meta = _retention_seq(proj_meta, 0, 1, N_META, N_META, 0, zero_ret, ret_gn_g[0])
    zero_s5 = jnp.zeros((1, S5_NSTATE), F32)
    _, mr, mi = _s5(proj_meta, 0, 1, N_META, N_META, zero_s5, zero_s5, *s5_w)

    ret_p, sret_p = _retention_seq(proj, 0, batch, seq, RET_CHUNK, N_META, s_meta, ret_gn_g[0])
    ret_s, sret_s = _retention_step(proj, rows_p, dbatch, dseq, PAST_LEN, state_ret[0],
                                    ret_gn_g[0], 4)

    z_p, s5r_p, s5i_p = _s5(proj, 0, batch, seq, 256,
                            jnp.broadcast_to(mr, (batch, S5_NSTATE)),
                            jnp.broadcast_to(mi, (batch, S5_NSTATE)), *s5_w)
    z_s, s5r_s, s5i_s = _s5(proj, rows_p // rows_s, dbatch, dseq, dseq,
                            state_s5_re[0].reshape(dbatch, S5_NSTATE),
                            state_s5_im[0].reshape(dbatch, S5_NSTATE), *s5_w)
    s5_out = _s5_post(z_p, z_s, w_glu[0], b_glu[0], s5_norm_g[0], tm, 512)

    x1, x1g, x1_ssq = _outproj(ret_p, ret_s, s5_out, w_out[0], xp, xs, norm2_g[0], tm, 512)
    a = _ffn_up(x1g, x1_ssq, w_gate[0], w_up[0], tm, 256)
    k_split = (w_down.shape[1] // 512 + 1) * 256
    x2 = _ffn_down(a, w_down[0], x1, 0, k_split, tm, 256)
    x2 = _ffn_down(a, w_down[0], x2, k_split, w_down.shape[1] - k_split, tm, 256)
    y_p, y_s = _final_norm(x2, final_norm_g, rows_p, 512)

    st = (1, -1, S5_GROUPS, S5_STATE)
    return (y_p.reshape(batch, seq, d), y_s.reshape(dbatch, dseq, d),
            sret_p[None], s5r_p.reshape(st), s5i_p.reshape(st),
            sret_s[None], s5r_s.reshape(st), s5i_s.reshape(st))
```

```python
import functools
import math

import numpy as np
import jax
import jax.numpy as jnp
from jax import lax
from jax.experimental import pallas as pl
from jax.experimental.pallas import tpu as pltpu

D_MODEL = 4096
N_META = 16
PAST_LEN = 16384
RET_WIDTH = D_MODEL // 2
S5_WIDTH = D_MODEL - RET_WIDTH
RET_HEADS = 8
RET_HEAD_DIM = RET_WIDTH // RET_HEADS
RET_CHUNK = 128
ROPE_BASE = 10000.0
S5_GROUP = 16
S5_GROUPS = S5_WIDTH // S5_GROUP
S5_STATE = 64
S5_NSTATE = S5_GROUPS * S5_STATE
IN_COLS = 4 * RET_WIDTH + S5_WIDTH
EPS = 1e-6
GN_EPS = 1e-5

LANES = 128
S5_GB = 16
S5_CH = S5_GB * S5_GROUP
S5_ST = S5_GB * S5_STATE
S5_SLABS = 2 * S5_ST // LANES
VMEM_LIMIT = 56 * 1024 * 1024

BF16 = jnp.bfloat16
F32 = jnp.float32


def _params(*sem):
    return pltpu.CompilerParams(dimension_semantics=sem, vmem_limit_bytes=VMEM_LIMIT)


def _dot(a, b):
    return jnp.dot(a, b, preferred_element_type=F32)


def _rmsnorm_kernel(*refs, n_src, n_first):
    x_refs, g_ref, o_ref = refs[:n_src], refs[n_src], refs[n_src + 1]
    x = x_refs[0][...]
    if n_src == 2:
        x = jnp.where(pl.program_id(0) < n_first, x, x_refs[1][...])
    r = lax.rsqrt(jnp.mean(x * x, axis=-1, keepdims=True) + EPS)
    o_ref[...] = (x * r * g_ref[...]).astype(o_ref.dtype)


def _rmsnorm(srcs, g, tm, out_dtype):
    d = srcs[0].shape[1]
    nblk = [s.shape[0] // tm for s in srcs]
    n_first = nblk[0]
    specs = [pl.BlockSpec((tm, d), lambda i: (jnp.minimum(i, n_first - 1), 0))]
    if len(srcs) == 2:
        specs.append(pl.BlockSpec((tm, d), lambda i: (jnp.maximum(i - n_first, 0), 0)))
    rows = sum(s.shape[0] for s in srcs)
    return pl.pallas_call(
        functools.partial(_rmsnorm_kernel, n_src=len(srcs), n_first=n_first),
        grid=(sum(nblk),),
        in_specs=specs + [pl.BlockSpec((1, d), lambda i: (0, 0))],
        out_specs=pl.BlockSpec((tm, d), lambda i: (i, 0)),
        out_shape=jax.ShapeDtypeStruct((rows, d), out_dtype),
        compiler_params=_params("parallel"),
        name="rmsnorm",
    )(*srcs, g.reshape(1, d))


def _final_norm_kernel(x_ref, g_ref, yp_ref, ys_ref, *, n_first):
    x = x_ref[...]
    r = lax.rsqrt(jnp.mean(x * x, axis=-1, keepdims=True) + EPS)
    y = x * r * g_ref[...]
    i = pl.program_id(0)

    @pl.when(i < n_first)
    def _():
        yp_ref[...] = y

    @pl.when(i >= n_first)
    def _():
        ys_ref[...] = y


def _final_norm(x, g, rows_p, tm):
    rows, d = x.shape
    n_first = rows_p // tm
    return pl.pallas_call(
        functools.partial(_final_norm_kernel, n_first=n_first),
        grid=(rows // tm,),
        in_specs=[pl.BlockSpec((tm, d), lambda i: (i, 0)),
                  pl.BlockSpec((1, d), lambda i: (0, 0))],
        out_specs=[pl.BlockSpec((tm, d), lambda i: (jnp.minimum(i, n_first - 1), 0)),
                   pl.BlockSpec((tm, d), lambda i: (jnp.maximum(i - n_first, 0), 0))],
        out_shape=[jax.ShapeDtypeStruct((rows_p, d), F32),
                   jax.ShapeDtypeStruct((rows - rows_p, d), F32)],
        compiler_params=_params("arbitrary"),
        name="final_norm",
    )(x, g.reshape(1, d))


def _inproj_kernel(h_ref, hx_ref, w_ref, o_ref, ox_ref):
    w = w_ref[...].astype(BF16)
    o_ref[...] = _dot(h_ref[...], w)

    @pl.when(pl.program_id(0) == 0)
    def _():
        ox_ref[...] = _dot(hx_ref[...], w)


def _inproj(h, h_extra, w, tm, tn):
    m, k = h.shape
    mx = h_extra.shape[0]
    n = w.shape[1]
    nj = n // tn
    return pl.pallas_call(
        _inproj_kernel,
        grid=(m // tm, nj),
        in_specs=[pl.BlockSpec((tm, k), lambda i, j: (i, 0)),
                  pl.BlockSpec((mx, k), lambda i, j: (0, 0)),
                  pl.BlockSpec((k, tn), lambda i, j: (0, j))],
        out_specs=[pl.BlockSpec((tm, tn), lambda i, j: (i, j)),
                   pl.BlockSpec((mx, tn), lambda i, j: (0, jnp.where(i == 0, j, nj - 1)))],
        out_shape=[jax.ShapeDtypeStruct((m, n), F32), jax.ShapeDtypeStruct((mx, n), F32)],
        compiler_params=_params("arbitrary", "arbitrary"),
        name="inproj",
    )(h, h_extra, w)


def _outproj_kernel(retp_ref, rets_ref, s5_ref, wt_ref, wb_ref, xp_ref, xs_ref, g_ref,
                    o_ref, xg_ref, ssq_ref, *, n_first):
    first = pl.program_id(0) < n_first
    j = pl.program_id(1)
    x = jnp.where(first, xp_ref[...], xs_ref[...])
    ret = jnp.where(first, retp_ref[...], rets_ref[...])
    acc = _dot(ret, wt_ref[...].astype(BF16))
    acc += _dot(s5_ref[...], wb_ref[...].astype(BF16))
    x1 = x + acc
    o_ref[...] = x1
    xg_ref[...] = (x1 * g_ref[...]).astype(xg_ref.dtype)
    sq = x1 * x1
    part = sq[:, :LANES]
    for c in range(1, sq.shape[1] // LANES):
        part += sq[:, c * LANES:(c + 1) * LANES]

    @pl.when(j == 0)
    def _():
        ssq_ref[...] = part

    @pl.when(j != 0)
    def _():
        ssq_ref[...] += part


def _outproj(ret_p, ret_s, s5, w_out, xp, xs, g, tm, tn):
    m, kh = s5.shape
    n = w_out.shape[1]
    n_first = xp.shape[0] // tm
    nj = n // tn
    return pl.pallas_call(
        functools.partial(_outproj_kernel, n_first=n_first),
        grid=(m // tm, nj),
        in_specs=[pl.BlockSpec((tm, kh), lambda i, j: (jnp.minimum(i, n_first - 1), 0)),
                  pl.BlockSpec((tm, kh), lambda i, j: (jnp.maximum(i - n_first, 0), 0),
                               pipeline_mode=pl.Buffered(1)),
                  pl.BlockSpec((tm, kh), lambda i, j: (i, 0)),
                  pl.BlockSpec((kh, tn), lambda i, j: (0, j)),
                  pl.BlockSpec((kh, tn), lambda i, j: (1, j)),
                  pl.BlockSpec((tm, tn), lambda i, j: (jnp.minimum(i, n_first - 1),
                                                       jnp.where(i < n_first, j, nj - 1))),
                  pl.BlockSpec((tm, tn), lambda i, j: (jnp.maximum(i - n_first, 0),
                                                       jnp.where(i < n_first, 0, j)),
                               pipeline_mode=pl.Buffered(1)),
                  pl.BlockSpec((1, tn), lambda i, j: (0, j))],
        out_specs=[pl.BlockSpec((tm, tn), lambda i, j: (i, j)),
                   pl.BlockSpec((tm, tn), lambda i, j: (i, j)),
                   pl.BlockSpec((tm, LANES), lambda i, j: (i, 0))],
        out_shape=[jax.ShapeDtypeStruct((m, n), F32),
                   jax.ShapeDtypeStruct((m, n), BF16),
                   jax.ShapeDtypeStruct((m, LANES), F32)],
        compiler_params=_params("parallel", "arbitrary"),
        name="outproj",
    )(ret_p, ret_s, s5, w_out, w_out, xp, xs, g.reshape(1, n))


def _ffn_up_kernel(xg_ref, ssq_ref, wg_ref, wu_ref, o_ref, r_scr):
    tf = wg_ref.shape[1]

    @pl.when(pl.program_id(1) == 0)
    def _():
        r_scr[...] = lax.rsqrt(jnp.sum(ssq_ref[...], axis=-1, keepdims=True) / xg_ref.shape[1] + EPS)

    w = jnp.concatenate([wg_ref[...].astype(BF16), wu_ref[...].astype(BF16)], axis=1)
    gu = _dot(xg_ref[...], w) * r_scr[...]
    o_ref[...] = (jax.nn.silu(gu[:, :tf]) * gu[:, tf:]).astype(o_ref.dtype)


def _ffn_up(xg, ssq, w_gate, w_up, tm, tf):
    m, k = xg.shape
    f = w_gate.shape[1]
    return pl.pallas_call(
        _ffn_up_kernel,
        grid=(m // tm, f // tf),
        in_specs=[pl.BlockSpec((tm, k), lambda i, j: (i, 0)),
                  pl.BlockSpec((tm, LANES), lambda i, j: (i, 0)),
                  pl.BlockSpec((k, tf), lambda i, j: (0, j)),
                  pl.BlockSpec((k, tf), lambda i, j: (0, j))],
        out_specs=pl.BlockSpec((tm, tf), lambda i, j: (i, j)),
        out_shape=jax.ShapeDtypeStruct((m, f), BF16),
        scratch_shapes=[pltpu.VMEM((tm, 1), F32)],
        compiler_params=_params("parallel", "arbitrary"),
        name="ffn_up",
    )(xg, ssq, w_gate, w_up)


def _ffn_down_kernel(a_hbm, w_ref, x_ref, o_ref, a_buf, a_sem, *, k0, tm):
    i, j = pl.program_id(0), pl.program_id(1)
    slot = i % 2

    def a_copy(block, s):
        rows = pl.ds(pl.multiple_of(block * tm, tm), tm)
        return pltpu.make_async_copy(a_hbm.at[rows, pl.ds(k0, a_buf.shape[2])], a_buf.at[s],
                                     a_sem.at[s])

    @pl.when((i == 0) & (j == 0))
    def _():
        a_copy(0, 0).start()

    @pl.when(j == 0)
    def _():
        a_copy(i, slot).wait()

        @pl.when(i + 1 < pl.num_programs(0))
        def _():
            a_copy(i + 1, 1 - slot).start()

    o_ref[...] = x_ref[...] + _dot(a_buf[slot], w_ref[...].astype(BF16))


def _ffn_down(a, w_down, x, k0, tk, tm, tn):
    m, f = a.shape
    n = w_down.shape[1]
    return pl.pallas_call(
        functools.partial(_ffn_down_kernel, k0=k0, tm=tm),
        grid=(m // tm, n // tn),
        in_specs=[pl.BlockSpec(memory_space=pl.ANY),
                  pl.BlockSpec((pl.Element(tk), pl.Element(tn)), lambda i, j: (k0, j * tn)),
                  pl.BlockSpec((tm, tn), lambda i, j: (i, j))],
        out_specs=pl.BlockSpec((tm, tn), lambda i, j: (i, j)),
        out_shape=jax.ShapeDtypeStruct((m, n), F32),
        scratch_shapes=[pltpu.VMEM((2, tm, tk), a.dtype), pltpu.SemaphoreType.DMA((2,))],
        compiler_params=_params("arbitrary", "arbitrary"),
        name="ffn_down",
    )(a, w_down, x)


def _rotary(x, cos, sin):
    half = RET_HEAD_DIM // 2
    x1, x2 = x[:, :half], x[:, half:]
    return jnp.concatenate([x1 * cos - x2 * sin, x1 * sin + x2 * cos], axis=-1)


def _retention_chunk(lg, sdec, s, q, k, v, g, cos, sin, gn):
    n = q.shape[0]
    ri = lax.broadcasted_iota(jnp.int32, (n, n), 0)
    ci = lax.broadcasted_iota(jnp.int32, (n, n), 1)
    diff = (ri - ci).astype(F32)
    mask = jnp.where(diff >= 0, jnp.exp(jnp.maximum(diff, 0.0) * lg), 0.0)
    row = lax.broadcasted_iota(jnp.int32, (n, 1), 0).astype(F32)
    q_dec = jnp.exp(lg * (row + 1.0))
    k_dec = jnp.exp(lg * (n - 1.0 - row))

    qr = _rotary(q, cos, sin)
    kr = _rotary(k, cos, sin) * (RET_HEAD_DIM ** -0.5)
    vb = v.astype(BF16)
    scores = lax.dot_general(qr.astype(BF16), kr.astype(BF16), (((1,), (1,)), ((), ())),
                             preferred_element_type=F32) * mask
    o = _dot(scores.astype(BF16), vb) + _dot((qr * q_dec).astype(BF16), s.astype(BF16))
    s_new = sdec * s + lax.dot_general((kr * k_dec).astype(BF16), vb, (((0,), (0,)), ((), ())),
                                       preferred_element_type=F32)
    mu = jnp.mean(o, axis=-1, keepdims=True)
    oc = o - mu
    var = jnp.mean(oc * oc, axis=-1, keepdims=True)
    y = oc * lax.rsqrt(var + GN_EPS) * gn
    return y * jax.nn.silu(g), s_new


def _ret_seq_kernel(dec_ref, q_ref, k_ref, v_ref, g_ref, cos_ref, sin_ref, s0_ref, gn_ref,
                    o_ref, sf_ref, s_scr, *, chunk, nchunks):
    h = pl.program_id(1)
    lg = dec_ref[0, h]
    sdec = dec_ref[1, h]
    s_scr[...] = s0_ref[...]
    gn = gn_ref[...]

    def body(c, carry):
        rows = pl.ds(pl.multiple_of(c * chunk, chunk), chunk)
        y, s_new = _retention_chunk(lg, sdec, s_scr[...], q_ref[rows, :], k_ref[rows, :],
                                    v_ref[rows, :], g_ref[rows, :], cos_ref[rows, :],
                                    sin_ref[rows, :], gn)
        o_ref[rows, :] = y.astype(o_ref.dtype)
        s_scr[...] = s_new
        return carry

    lax.fori_loop(0, nchunks, body, 0, unroll=min(8, nchunks))
    sf_ref[...] = s_scr[...]


def _ret_decay_consts(chunk):
    lg = np.log(1.0 - 2.0 ** (-5.0 - np.arange(RET_HEADS, dtype=np.float64)))
    return jnp.asarray(np.stack([lg, np.exp(lg * chunk)]), dtype=F32)


def _rope_tables(pos):
    half = RET_HEAD_DIM // 2
    inv = ROPE_BASE ** (-np.arange(half, dtype=np.float64) / half)
    ang = np.asarray(pos, dtype=np.float64)[:, None] * inv[None, :]
    return jnp.asarray(np.cos(ang), dtype=F32), jnp.asarray(np.sin(ang), dtype=F32)


def _retention_seq(proj, row_block0, nbatch, seq, chunk, pos0, s0, gn_g):
    cos, sin = _rope_tables(pos0 + np.arange(seq))
    hd = RET_HEAD_DIM

    def col(off):
        return pl.BlockSpec((seq, hd), lambda b, h: (row_block0 + b, off + h))

    tbl = pl.BlockSpec((seq, hd // 2), lambda b, h: (0, 0))
    return pl.pallas_call(
        functools.partial(_ret_seq_kernel, chunk=chunk, nchunks=seq // chunk),
        grid=(nbatch, RET_HEADS),
        in_specs=[pl.BlockSpec(memory_space=pltpu.SMEM),
                  col(0), col(RET_HEADS), col(2 * RET_HEADS), col(3 * RET_HEADS), tbl, tbl,
                  pl.BlockSpec((None, None, hd, hd), lambda b, h: (0, h, 0, 0)),
                  pl.BlockSpec((1, hd), lambda b, h: (0, h))],
        out_specs=[pl.BlockSpec((seq, hd), lambda b, h: (b, h)),
                   pl.BlockSpec((None, None, hd, hd), lambda b, h: (b, h, 0, 0))],
        out_shape=[jax.ShapeDtypeStruct((nbatch * seq, RET_WIDTH), BF16),
                   jax.ShapeDtypeStruct((nbatch, RET_HEADS, hd, hd), F32)],
        scratch_shapes=[pltpu.VMEM((hd, hd), F32)],
        compiler_params=_params("parallel", "arbitrary"),
        name="retention_seq",
    )(_ret_decay_consts(chunk), proj, proj, proj, proj, cos, sin, s0, gn_g.reshape(1, RET_WIDTH))


def _ret_step_kernel(dec_ref, q_ref, k_ref, v_ref, g_ref, cos_ref, sin_ref, s_ref, gn_ref,
                     o_ref, sn_ref, *, nb, seq):
    hd = RET_HEAD_DIM
    cos, sin = cos_ref[...], sin_ref[...]
    for b in range(nb):
        rows = slice(b * seq, (b + 1) * seq)
        for h in range(RET_HEADS):
            cols = slice(h * hd, (h + 1) * hd)
            y, s_new = _retention_chunk(dec_ref[0, h], dec_ref[1, h], s_ref[b, h],
                                        q_ref[rows, cols], k_ref[rows, cols], v_ref[rows, cols],
                                        g_ref[rows, cols], cos, sin, gn_ref[:, cols])
            o_ref[rows, cols] = y.astype(o_ref.dtype)
            sn_ref[b, h] = s_new


def _retention_step(proj, row0, nbatch, seq, pos0, state, gn_g, nb):
    cos, sin = _rope_tables(pos0 + np.arange(seq))
    hd = RET_HEAD_DIM
    rb0 = row0 // (nb * seq)

    def col(off):
        return pl.BlockSpec((nb * seq, RET_WIDTH), lambda i: (rb0 + i, off))

    tbl = pl.BlockSpec((seq, hd // 2), lambda i: (0, 0))
    st = pl.BlockSpec((nb, RET_HEADS, hd, hd), lambda i: (i, 0, 0, 0))
    return pl.pallas_call(
        functools.partial(_ret_step_kernel, nb=nb, seq=seq),
        grid=(nbatch // nb,),
        in_specs=[pl.BlockSpec(memory_space=pltpu.SMEM),
                  col(0), col(1), col(2), col(3), tbl, tbl, st,
                  pl.BlockSpec((1, RET_WIDTH), lambda i: (0, 0))],
        out_specs=[pl.BlockSpec((nb * seq, RET_WIDTH), lambda i: (i, 0)), st],
        out_shape=[jax.ShapeDtypeStruct((nbatch * seq, RET_WIDTH), BF16),
                   jax.ShapeDtypeStruct(state.shape, F32)],
        compiler_params=_params("parallel"),
        name="retention_step",
    )(_ret_decay_consts(seq), proj, proj, proj, proj, cos, sin, state,
      gn_g.reshape(1, RET_WIDTH))


def _s5_disc_kernel(lr_ref, li_ref, ldt_ref, br_ref, bi_ref, ar_ref, ai_ref, bbr_ref, bbi_ref):
    lr, li = lr_ref[...], li_ref[...]
    dt = jnp.exp(ldt_ref[...])
    mag = jnp.exp(lr * dt)
    ar = mag * jnp.cos(li * dt)
    ai = mag * jnp.sin(li * dt)
    nr, ni = ar - 1.0, ai
    den = lr * lr + li * li
    fr = (nr * lr + ni * li) / den
    fi = (ni * lr - nr * li) / den
    br, bi = br_ref[...], bi_ref[...]
    ar_ref[...] = ar
    ai_ref[...] = ai
    bbr_ref[...] = fr * br - fi * bi
    bbi_ref[...] = fr * bi + fi * br


def _s5_discretize(lam_re, lam_im, log_dt, b_re, b_im):
    g, p, c = b_re.shape
    shp3 = jax.ShapeDtypeStruct((g, 1, p), F32)
    shpb = jax.ShapeDtypeStruct((g, c, p), F32)
    return pl.pallas_call(
        _s5_disc_kernel,
        out_shape=[shp3, shp3, shpb, shpb],
        name="s5_discretize",
    )(lam_re.reshape(g, 1, p), lam_im.reshape(g, 1, p), log_dt.reshape(g, 1, 1),
      jnp.swapaxes(b_re, 1, 2), jnp.swapaxes(b_im, 1, 2))


S5_NPAIR = S5_SLABS // 4


def _lanes(s):
    return slice(s * LANES, (s + 1) * LANES)


def _s5_place(s, nb, pitch):
    comp, ls = s // (2 * S5_NPAIR), s % (2 * S5_NPAIR)
    hf, p = ls // S5_NPAIR, ls % S5_NPAIR
    return comp * S5_NPAIR + p, hf * (nb * pitch)


def _s5_paired(ref, p, nb, order=None):
    def slab(hf):
        cols = _lanes(p + hf * S5_NPAIR)
        if order is None or ref.shape[0] == 1:
            return jnp.broadcast_to(ref[:, cols], (nb, LANES))
        return jnp.concatenate([ref[o:o + 1, cols] for o in order], axis=0)
    return jnp.concatenate([slab(0), slab(1)], axis=0)


def _s5_block_weights(bbr_ref, bbi_ref, cr_ref, ci_ref, bw_scr, cw_scr):
    bw_scr[...] = jnp.zeros_like(bw_scr)
    cw_scr[...] = jnp.zeros_like(cw_scr)
    for g in range(S5_GB):
        ch = slice(g * S5_GROUP, (g + 1) * S5_GROUP)
        for comp, (b_ref, c_ref, sign) in enumerate(((bbr_ref, cr_ref, 1.0), (bbi_ref, ci_ref, -1.0))):
            st = slice(comp * S5_ST + g * S5_STATE, comp * S5_ST + (g + 1) * S5_STATE)
            bw_scr[ch, st] = b_ref[g]
            cw_scr[st, ch] = sign * c_ref[g]


def _s5_step(buf, t, xs, ar, ai, nb, pitch):
    rows = pl.ds(t, 2 * nb, stride=pitch)
    new = [None] * (2 * S5_NPAIR)
    for p in range(S5_NPAIR):
        xr, xi = xs[p], xs[S5_NPAIR + p]
        nr = ar[p] * xr - ai[p] * xi + buf[p, rows, :]
        ni = ar[p] * xi + ai[p] * xr + buf[S5_NPAIR + p, rows, :]
        buf[p, rows, :] = nr
        buf[S5_NPAIR + p, rows, :] = ni
        new[p], new[S5_NPAIR + p] = nr, ni
    return tuple(new)


def _s5_scan_io(x0r_ref, x0i_ref, ar_ref, ai_ref, nb, order=None):
    ar = [_s5_paired(ar_ref, p, nb) for p in range(S5_NPAIR)]
    ai = [_s5_paired(ai_ref, p, nb) for p in range(S5_NPAIR)]
    xs0 = tuple([_s5_paired(x0r_ref, p, nb, order) for p in range(S5_NPAIR)]
                + [_s5_paired(x0i_ref, p, nb, order) for p in range(S5_NPAIR)])
    return ar, ai, xs0


def _s5_store_final(xs, xfr_ref, xfi_ref, nb, order=None):
    for p in range(S5_NPAIR):
        for hf in range(2):
            cols = _lanes(p + hf * S5_NPAIR)
            for x, ref in ((xs[p], xfr_ref), (xs[S5_NPAIR + p], xfi_ref)):
                if order is None:
                    ref[:, cols] = x[hf * nb:(hf + 1) * nb]
                else:
                    for i, o in enumerate(order):
                        ref[o:o + 1, cols] = x[hf * nb + i:hf * nb + i + 1]


def _s5_kernel(u_ref, x0r_ref, x0i_ref, ar_ref, ai_ref, bbr_ref, bbi_ref, cr_ref, ci_ref, d_ref,
               z_ref, xfr_ref, xfi_ref, buf, bw_scr, cw_scr, *, nb, seq):
    _s5_block_weights(bbr_ref, bbi_ref, cr_ref, ci_ref, bw_scr, cw_scr)
    ar, ai, xs = _s5_scan_io(x0r_ref, x0i_ref, ar_ref, ai_ref, nb)
    u = u_ref[...]
    bu = _dot(u.astype(BF16), bw_scr[...].astype(BF16))
    for s in range(S5_SLABS):
        q, off = _s5_place(s, nb, seq)
        buf[q, off:off + nb * seq, :] = bu[:, _lanes(s)]
    xs = lax.fori_loop(0, seq, lambda t, xs: _s5_step(buf, t, xs, ar, ai, nb, seq), xs,
                       unroll=8 if nb <= 8 else 1)
    parts = []
    for s in range(S5_SLABS):
        q, off = _s5_place(s, nb, seq)
        parts.append(buf[q, off:off + nb * seq, :])
    y = _dot(jnp.concatenate(parts, axis=1).astype(BF16), cw_scr[...].astype(BF16)) + d_ref[...] * u
    z_ref[...] = jax.nn.gelu(y).astype(z_ref.dtype)
    _s5_store_final(xs, xfr_ref, xfi_ref, nb)


def _s5_chunked_kernel(u_ref, x0r_ref, x0i_ref, ar_ref, ai_ref, bbr_ref, bbi_ref, cr_ref, ci_ref,
                       d_ref, z_ref, xfr_ref, xfi_ref, buf, bw_scr, cw_scr, *, nb, seq, tc, pitch):
    lead_odd = pitch % 8
    assert nb % 2 == 0 and nb >= 4 and lead_odd == 4 and (nb * pitch) % 8 == 0
    inner = list(range(1, nb - 1))
    even, odd = [0, nb - 1] + inner[nb // 2:], inner[:nb // 2]
    order = [(odd if i % 2 else even)[i // 2] for i in range(nb)]

    _s5_block_weights(bbr_ref, bbi_ref, cr_ref, ci_ref, bw_scr, cw_scr)
    bw = bw_scr[...].astype(BF16)
    cw = cw_scr[...].astype(BF16)
    ar, ai, xs = _s5_scan_io(x0r_ref, x0i_ref, ar_ref, ai_ref, nb, order)
    dvec = d_ref[...]

    def chunk(c, xs):
        t0 = c * tc
        rows = [pl.ds(pl.multiple_of(order[i] * seq + t0, tc), tc) for i in range(nb)]
        lead = [lead_odd if i % 2 else 0 for i in range(nb)]

        def u_window(i):
            if not lead[i]:
                return u_ref[rows[i], :]
            wide = u_ref[pl.ds(pl.multiple_of(order[i] * seq + t0 - 8, 8), tc + 16), :]
            return wide[8 - lead[i]:8 + tc + lead[i]]

        for i in range(nb):
            bu = _dot(u_window(i).astype(BF16), bw)
            for s in range(S5_SLABS):
                q, off = _s5_place(s, nb, pitch)
                r0 = off + i * pitch - lead[i]
                buf[q, r0:r0 + tc + 2 * lead[i], :] = bu[:, _lanes(s)]
        xs = lax.fori_loop(0, tc, lambda t, xs: _s5_step(buf, t, xs, ar, ai, nb, pitch), xs,
                           unroll=8)
        for i in range(nb):
            parts = []
            for s in range(S5_SLABS):
                q, off = _s5_place(s, nb, pitch)
                r0 = off + i * pitch - lead[i]
                parts.append(buf[q, r0:r0 + tc + 2 * lead[i], :])
            y = _dot(jnp.concatenate(parts, axis=1).astype(BF16), cw)[lead[i]:lead[i] + tc]
            y = y + dvec * u_ref[rows[i], :]
            z_ref[rows[i], :] = jax.nn.gelu(y).astype(z_ref.dtype)
        return xs

    xs = lax.fori_loop(0, seq // tc, chunk, xs)
    _s5_store_final(xs, xfr_ref, xfi_ref, nb, order)


def _s5(proj, row_block0, nb, seq, tc, x0r, x0i, ar, ai, bbr, bbi, c_re, c_im, d):
    rows = nb * seq
    u_col0 = 4 * RET_WIDTH // S5_CH
    nj = S5_GROUPS // S5_GB
    st = pl.BlockSpec((nb, S5_ST), lambda j: (0, j))
    coef = pl.BlockSpec((1, S5_ST), lambda j: (0, j))
    bspec = pl.BlockSpec((S5_GB, S5_GROUP, S5_STATE), lambda j: (j, 0, 0))
    cspec = pl.BlockSpec((S5_GB, S5_STATE, S5_GROUP), lambda j: (j, 0, 0))
    weights = [pltpu.VMEM((S5_CH, 2 * S5_ST), F32), pltpu.VMEM((2 * S5_ST, S5_CH), F32)]
    if tc == seq:
        body = functools.partial(_s5_kernel, nb=nb, seq=seq)
        pitch = seq
    else:
        pitch = tc + 4
        body = functools.partial(_s5_chunked_kernel, nb=nb, seq=seq, tc=tc, pitch=pitch)
    scratch = [pltpu.VMEM((S5_SLABS // 2, 2 * nb * pitch, LANES), F32)] + weights
    return pl.pallas_call(
        body,
        grid=(nj,),
        in_specs=[pl.BlockSpec((rows, S5_CH), lambda j: (row_block0, u_col0 + j)),
                  st, st, coef, coef, bspec, bspec, cspec, cspec,
                  pl.BlockSpec((1, S5_CH), lambda j: (0, j))],
        out_specs=[pl.BlockSpec((rows, S5_CH), lambda j: (0, j)), st, st],
        out_shape=[jax.ShapeDtypeStruct((rows, S5_WIDTH), BF16),
                   jax.ShapeDtypeStruct((nb, S5_NSTATE), F32),
                   jax.ShapeDtypeStruct((nb, S5_NSTATE), F32)],
        scratch_shapes=scratch,
        compiler_params=_params("parallel"),
        name="s5_scan",
    )(proj, x0r, x0i, ar, ai, bbr, bbi, c_re, c_im, d.reshape(1, S5_WIDTH))


def _s5_post_kernel(zp_ref, zs_ref, zpc_ref, zsc_ref, w_ref, b_ref, g_ref, o_ref,
                    zb_scr, gate_scr, ssq_scr, *, n_first, tn):
    i, j = pl.program_id(0), pl.program_id(1)
    first = i < n_first

    @pl.when(j == 0)
    def _():
        zb_scr[...] = jnp.where(first, zp_ref[...], zs_ref[...])
        ssq_scr[...] = jnp.zeros_like(ssq_scr)

    zc = jnp.where(first, zpc_ref[...], zsc_ref[...]).astype(F32)
    t = _dot(zb_scr[...], w_ref[...].astype(BF16)) + b_ref[...]
    s = zc * jax.nn.sigmoid(t)
    gate_scr[j] = s
    ssq_scr[...] += jnp.sum(s * s, axis=-1, keepdims=True)

    @pl.when(j == pl.num_programs(1) - 1)
    def _():
        r = lax.rsqrt(ssq_scr[...] / (tn * gate_scr.shape[0]) + EPS)
        for jj in range(gate_scr.shape[0]):
            cols = slice(jj * tn, (jj + 1) * tn)
            o_ref[:, cols] = (gate_scr[jj] * r * g_ref[:, cols]).astype(o_ref.dtype)


def _s5_post(zp, zs, w_glu, b_glu, norm_g, tm, tn):
    d = zp.shape[1]
    n_first = zp.shape[0] // tm
    nblk = n_first + zs.shape[0] // tm
    nj = d // tn

    def first_row(i):
        return jnp.minimum(i, n_first - 1)

    def second_row(i):
        return jnp.maximum(i - n_first, 0)

    return pl.pallas_call(
        functools.partial(_s5_post_kernel, n_first=n_first, tn=tn),
        grid=(nblk, nj),
        in_specs=[pl.BlockSpec((tm, d), lambda i, j: (first_row(i), 0)),
                  pl.BlockSpec((tm, d), lambda i, j: (second_row(i), 0)),
                  pl.BlockSpec((tm, tn), lambda i, j: (first_row(i), jnp.where(i < n_first, j, nj - 1))),
                  pl.BlockSpec((tm, tn), lambda i, j: (second_row(i), jnp.where(i < n_first, 0, j))),
                  pl.BlockSpec((d, tn), lambda i, j: (0, j)),
                  pl.BlockSpec((1, tn), lambda i, j: (0, j)),
                  pl.BlockSpec((1, d), lambda i, j: (0, 0))],
        out_specs=pl.BlockSpec((tm, d), lambda i, j: (i, 0)),
        out_shape=jax.ShapeDtypeStruct((nblk * tm, d), BF16),
        scratch_shapes=[pltpu.VMEM((tm, d), BF16), pltpu.VMEM((nj, tm, tn), F32),
                        pltpu.VMEM((tm, 1), F32)],
        compiler_params=_params("parallel", "arbitrary"),
        name="s5_post",
    )(zp, zs, zp, zs, w_glu, b_glu.reshape(1, d), norm_g.reshape(1, d))


def kernel(x_prompt, x_sample, state_ret, state_s5_re, state_s5_im, meta_tokens, norm1_g, w_in, ret_gn_g, s5_lam_re, s5_lam_im, s5_log_dt, s5_b_re, s5_b_im, s5_c_re, s5_c_im, s5_d, w_glu, b_glu, s5_norm_g, w_out, norm2_g, w_gate, w_up, w_down, final_norm_g):
    assert norm1_g.shape[0] == 1, "single-layer model"
    batch, seq, d = x_prompt.shape
    dbatch, dseq, _ = x_sample.shape
    rows_p, rows_s = batch * seq, dbatch * dseq
    tm = 1024
    assert rows_p % tm == 0 and rows_s % tm == 0 and seq % RET_CHUNK == 0
    xp = x_prompt.reshape(rows_p, d)
    xs = x_sample.reshape(rows_s, d)
    hd = RET_HEAD_DIM

    ar, ai, bbr, bbi = _s5_discretize(s5_lam_re[0], s5_lam_im[0], s5_log_dt[0], s5_b_re[0], s5_b_im[0])
    ar = ar.reshape(1, S5_NSTATE)
    ai = ai.reshape(1, S5_NSTATE)
    s5_w = (ar, ai, bbr, bbi, jnp.swapaxes(s5_c_re[0], 1, 2), jnp.swapaxes(s5_c_im[0], 1, 2), s5_d[0])

    h_meta = _rmsnorm([meta_tokens], norm1_g[0], N_META, BF16)
    h = _rmsnorm([xp, xs], norm1_g[0], 512, BF16)
    proj, proj_meta = _inproj(h, h_meta, w_in[0], 1536, 512)

    zero_ret = jnp.zeros((1, RET_HEADS, hd, hd), F32)
    _, s_meta = _retention_seq(proj_meta, 0, 1, N_META, N_META, 0, zero_ret, ret_gn_g[0])
    zero_s5 = jnp.zeros((1, S5_NSTATE), F32)
    _, mr, mi = _s5(proj_meta, 0, 1, N_META, N_META, zero_s5, zero_s5, *s5_w)

    ret_p, sret_p = _retention_seq(proj, 0, batch, seq, RET_CHUNK, N_META, s_meta, ret_gn_g[0])
    ret_s, sret_s = _retention_step(proj, rows_p, dbatch, dseq, PAST_LEN, state_ret[0],
                                    ret_gn_g[0], 4)

    z_p, s5r_p, s5i_p = _s5(proj, 0, batch, seq, 256,
                            jnp.broadcast_to(mr, (batch, S5_NSTATE)),
                            jnp.broadcast_to(mi, (batch, S5_NSTATE)), *s5_w)
    z_s, s5r_s, s5i_s = _s5(proj, rows_p // rows_s, dbatch, dseq, dseq,
                            state_s5_re[0].reshape(dbatch, S5_NSTATE),
                            state_s5_im[0].reshape(dbatch, S5_NSTATE), *s5_w)
    s5_out = _s5_post(z_p, z_s, w_glu[0], b_glu[0], s5_norm_g[0], tm, 512)

    x1, x1g, x1_ssq = _outproj(ret_p, ret_s, s5_out, w_out[0], xp, xs, norm2_g[0], tm, 512)
    a = _ffn_up(x1g, x1_ssq, w_gate[0], w_up[0], tm, 256)
    k_split = (w_down.shape[1] // 512 + 1) * 256
    x2 = _ffn_down(a, w_down[0], x1, 0, k_split, tm, 256)
    x2 = _ffn_down(a, w_down[0], x2, k_split, w_down.shape[1] - k_split, tm, 256)
    y_p, y_s = _final_norm(x2, final_norm_g, rows_p, 512)

    st = (1, -1, S5_GROUPS, S5_STATE)
    return (y_p.reshape(batch, seq, d), y_s.reshape(dbatch, dseq, d),
            sret_p[None], s5r_p.reshape(st), s5i_p.reshape(st),
            sret_s[None], s5r_s.reshape(st), s5i_s.reshape(st))
```

```python
import functools

import numpy as np
import jax
import jax.numpy as jnp
from jax import lax
from jax.experimental import pallas as pl
from jax.experimental.pallas import tpu as pltpu

D_MODEL = 4096
N_META = 16
PAST_LEN = 16384
RET_WIDTH = D_MODEL // 2
S5_WIDTH = D_MODEL - RET_WIDTH
RET_HEADS = 8
RET_HEAD_DIM = RET_WIDTH // RET_HEADS
RET_CHUNK = 128
ROPE_BASE = 10000.0
S5_GROUP = 16
S5_GROUPS = S5_WIDTH // S5_GROUP
S5_STATE = 64
S5_NSTATE = S5_GROUPS * S5_STATE
IN_COLS = 4 * RET_WIDTH + S5_WIDTH
EPS = 1e-6
GN_EPS = 1e-5

LANES = 128
S5_GB = 16
S5_CH = S5_GB * S5_GROUP
S5_ST = S5_GB * S5_STATE
S5_SLABS = 2 * S5_ST // LANES
VMEM_LIMIT = 56 * 1024 * 1024

BF16 = jnp.bfloat16
F32 = jnp.float32


def _params(*sem):
    return pltpu.CompilerParams(dimension_semantics=sem, vmem_limit_bytes=VMEM_LIMIT)


def _dot(a, b):
    return jnp.dot(a, b, preferred_element_type=F32)


def _rmsnorm_kernel(*refs, n_src, n_first):
    x_refs, g_ref, o_ref = refs[:n_src], refs[n_src], refs[n_src + 1]
    x = x_refs[0][...]
    if n_src == 2:
        x = jnp.where(pl.program_id(0) < n_first, x, x_refs[1][...])
    r = lax.rsqrt(jnp.mean(x * x, axis=-1, keepdims=True) + EPS)
    o_ref[...] = (x * r * g_ref[...]).astype(o_ref.dtype)


def _rmsnorm(srcs, g, tm, out_dtype):
    d = srcs[0].shape[1]
    nblk = [s.shape[0] // tm for s in srcs]
    n_first = nblk[0]
    specs = [pl.BlockSpec((tm, d), lambda i: (jnp.minimum(i, n_first - 1), 0))]
    if len(srcs) == 2:
        specs.append(pl.BlockSpec((tm, d), lambda i: (jnp.maximum(i - n_first, 0), 0)))
    rows = sum(s.shape[0] for s in srcs)
    return pl.pallas_call(
        functools.partial(_rmsnorm_kernel, n_src=len(srcs), n_first=n_first),
        grid=(sum(nblk),),
        in_specs=specs + [pl.BlockSpec((1, d), lambda i: (0, 0))],
        out_specs=pl.BlockSpec((tm, d), lambda i: (i, 0)),
        out_shape=jax.ShapeDtypeStruct((rows, d), out_dtype),
        compiler_params=_params("parallel"),
        name="rmsnorm",
    )(*srcs, g.reshape(1, d))


def _final_norm_kernel(x_ref, g_ref, yp_ref, ys_ref, *, n_first):
    x = x_ref[...]
    r = lax.rsqrt(jnp.mean(x * x, axis=-1, keepdims=True) + EPS)
    y = x * r * g_ref[...]
    i = pl.program_id(0)

    @pl.when(i < n_first)
    def _():
        yp_ref[...] = y

    @pl.when(i >= n_first)
    def _():
        ys_ref[...] = y


def _final_norm(x, g, rows_p, tm):
    rows, d = x.shape
    n_first = rows_p // tm
    return pl.pallas_call(
        functools.partial(_final_norm_kernel, n_first=n_first),
        grid=(rows // tm,),
        in_specs=[pl.BlockSpec((tm, d), lambda i: (i, 0)),
                  pl.BlockSpec((1, d), lambda i: (0, 0))],
        out_specs=[pl.BlockSpec((tm, d), lambda i: (jnp.minimum(i, n_first - 1), 0)),
                   pl.BlockSpec((tm, d), lambda i: (jnp.maximum(i - n_first, 0), 0))],
        out_shape=[jax.ShapeDtypeStruct((rows_p, d), F32),
                   jax.ShapeDtypeStruct((rows - rows_p, d), F32)],
        compiler_params=_params("arbitrary"),
        name="final_norm",
    )(x, g.reshape(1, d))


def _inproj_kernel(h_ref, hx_ref, w_ref, o_ref, ox_ref):
    w = w_ref[...].astype(BF16)
    o_ref[...] = _dot(h_ref[...], w)

    @pl.when(pl.program_id(0) == 0)
    def _():
        ox_ref[...] = _dot(hx_ref[...], w)


def _inproj(h, h_extra, w, tm, tn):
    m, k = h.shape
    mx = h_extra.shape[0]
    n = w.shape[1]
    nj = n // tn
    return pl.pallas_call(
        _inproj_kernel,
        grid=(m // tm, nj),
        in_specs=[pl.BlockSpec((tm, k), lambda i, j: (i, 0)),
                  pl.BlockSpec((mx, k), lambda i, j: (0, 0)),
                  pl.BlockSpec((k, tn), lambda i, j: (0, j))],
        out_specs=[pl.BlockSpec((tm, tn), lambda i, j: (i, j)),
                   pl.BlockSpec((mx, tn), lambda i, j: (0, jnp.where(i == 0, j, nj - 1)))],
        out_shape=[jax.ShapeDtypeStruct((m, n), F32), jax.ShapeDtypeStruct((mx, n), F32)],
        compiler_params=_params("arbitrary", "arbitrary"),
        name="inproj",
    )(h, h_extra, w)


def _outproj_kernel(retp_hbm, rets_hbm, s5_hbm, w_ref, xp_ref, xs_ref, g_ref,
                    o_ref, xg_ref, ssq_ref, mix_buf, mix_sem, *, n_first, tm):
    i, j = pl.program_id(0), pl.program_id(1)
    kh = mix_buf.shape[2] // 2
    slot = i % 2

    def ret_copy(block, s, from_prompt):
        src = retp_hbm if from_prompt else rets_hbm
        row0 = (block if from_prompt else block - n_first) * tm
        return pltpu.make_async_copy(src.at[pl.ds(pl.multiple_of(row0, tm), tm)],
                                     mix_buf.at[s, :, pl.ds(0, kh)], mix_sem.at[s, 0])

    def s5_copy(block, s):
        return pltpu.make_async_copy(s5_hbm.at[pl.ds(pl.multiple_of(block * tm, tm), tm)],
                                     mix_buf.at[s, :, pl.ds(kh, kh)], mix_sem.at[s, 1])

    def each_copy(block, s, act):
        @pl.when(block < n_first)
        def _():
            act(ret_copy(block, s, True))

        @pl.when(block >= n_first)
        def _():
            act(ret_copy(block, s, False))

        act(s5_copy(block, s))

    @pl.when((i == 0) & (j == 0))
    def _():
        each_copy(0, 0, lambda c: c.start())

    @pl.when(j == 0)
    def _():
        each_copy(i, slot, lambda c: c.wait())

        @pl.when(i + 1 < pl.num_programs(0))
        def _():
            each_copy(i + 1, 1 - slot, lambda c: c.start())

    x = jnp.where(i < n_first, xp_ref[...], xs_ref[...])
    x1 = x + _dot(mix_buf[slot], w_ref[...].astype(BF16))
    o_ref[...] = x1
    xg_ref[...] = (x1 * g_ref[...]).astype(xg_ref.dtype)
    sq = x1 * x1
    part = sq[:, :LANES]
    for c in range(1, sq.shape[1] // LANES):
        part += sq[:, c * LANES:(c + 1) * LANES]

    @pl.when(j == 0)
    def _():
        ssq_ref[...] = part

    @pl.when(j != 0)
    def _():
        ssq_ref[...] += part


def _outproj(ret_p, ret_s, s5, w_out, xp, xs, g, tm, tn):
    m, kh = s5.shape
    k, n = w_out.shape
    assert ret_p.shape[1] == kh and k == 2 * kh
    n_first = xp.shape[0] // tm
    nj = n // tn
    hbm = pl.BlockSpec(memory_space=pl.ANY)
    return pl.pallas_call(
        functools.partial(_outproj_kernel, n_first=n_first, tm=tm),
        grid=(m // tm, nj),
        in_specs=[hbm, hbm, hbm,
                  pl.BlockSpec((k, tn), lambda i, j: (0, j)),
                  pl.BlockSpec((tm, tn), lambda i, j: (jnp.minimum(i, n_first - 1),
                                                       jnp.where(i < n_first, j, nj - 1))),
                  pl.BlockSpec((tm, tn), lambda i, j: (jnp.maximum(i - n_first, 0),
                                                       jnp.where(i < n_first, 0, j)),
                               pipeline_mode=pl.Buffered(1)),
                  pl.BlockSpec((1, tn), lambda i, j: (0, j))],
        out_specs=[pl.BlockSpec((tm, tn), lambda i, j: (i, j)),
                   pl.BlockSpec((tm, tn), lambda i, j: (i, j)),
                   pl.BlockSpec((tm, LANES), lambda i, j: (i, 0))],
        out_shape=[jax.ShapeDtypeStruct((m, n), F32),
                   jax.ShapeDtypeStruct((m, n), BF16),
                   jax.ShapeDtypeStruct((m, LANES), F32)],
        scratch_shapes=[pltpu.VMEM((2, tm, k), s5.dtype), pltpu.SemaphoreType.DMA((2, 2))],
        compiler_params=_params("arbitrary", "arbitrary"),
        name="outproj",
    )(ret_p, ret_s, s5, w_out, xp, xs, g.reshape(1, n))


def _ffn_up_kernel(xg_ref, ssq_ref, wg_ref, wu_ref, o_ref, r_scr):
    tf = wg_ref.shape[1]

    @pl.when(pl.program_id(1) == 0)
    def _():
        r_scr[...] = lax.rsqrt(jnp.sum(ssq_ref[...], axis=-1, keepdims=True) / xg_ref.shape[1] + EPS)

    w = jnp.concatenate([wg_ref[...].astype(BF16), wu_ref[...].astype(BF16)], axis=1)
    gu = _dot(xg_ref[...], w) * r_scr[...]
    o_ref[...] = (jax.nn.silu(gu[:, :tf]) * gu[:, tf:]).astype(o_ref.dtype)


def _ffn_up(xg, ssq, w_gate, w_up, tm, tf):
    m, k = xg.shape
    f = w_gate.shape[1]
    return pl.pallas_call(
        _ffn_up_kernel,
        grid=(m // tm, f // tf),
        in_specs=[pl.BlockSpec((tm, k), lambda i, j: (i, 0)),
                  pl.BlockSpec((tm, LANES), lambda i, j: (i, 0)),
                  pl.BlockSpec((k, tf), lambda i, j: (0, j)),
                  pl.BlockSpec((k, tf), lambda i, j: (0, j))],
        out_specs=pl.BlockSpec((tm, tf), lambda i, j: (i, j)),
        out_shape=jax.ShapeDtypeStruct((m, f), BF16),
        scratch_shapes=[pltpu.VMEM((tm, 1), F32)],
        compiler_params=_params("parallel", "arbitrary"),
        name="ffn_up",
    )(xg, ssq, w_gate, w_up)


def _ffn_down_kernel(a_hbm, w_ref, x_ref, o_ref, a_buf, a_sem, *, k0, tm):
    i, j = pl.program_id(0), pl.program_id(1)
    slot = i % 2

    def a_copy(block, s):
        rows = pl.ds(pl.multiple_of(block * tm, tm), tm)
        return pltpu.make_async_copy(a_hbm.at[rows, pl.ds(k0, a_buf.shape[2])], a_buf.at[s],
                                     a_sem.at[s])

    @pl.when((i == 0) & (j == 0))
    def _():
        a_copy(0, 0).start()

    @pl.when(j == 0)
    def _():
        a_copy(i, slot).wait()

        @pl.when(i + 1 < pl.num_programs(0))
        def _():
            a_copy(i + 1, 1 - slot).start()

    o_ref[...] = x_ref[...] + _dot(a_buf[slot], w_ref[...].astype(BF16))


def _ffn_down(a, w_down, x, k0, tk, tm, tn):
    m, f = a.shape
    n = w_down.shape[1]
    return pl.pallas_call(
        functools.partial(_ffn_down_kernel, k0=k0, tm=tm),
        grid=(m // tm, n // tn),
        in_specs=[pl.BlockSpec(memory_space=pl.ANY),
                  pl.BlockSpec((pl.Element(tk), pl.Element(tn)), lambda i, j: (k0, j * tn)),
                  pl.BlockSpec((tm, tn), lambda i, j: (i, j))],
        out_specs=pl.BlockSpec((tm, tn), lambda i, j: (i, j)),
        out_shape=jax.ShapeDtypeStruct((m, n), F32),
        scratch_shapes=[pltpu.VMEM((2, tm, tk), a.dtype), pltpu.SemaphoreType.DMA((2,))],
        compiler_params=_params("arbitrary", "arbitrary"),
        name="ffn_down",
    )(a, w_down, x)


def _rotary(x, cos, sin):
    half = RET_HEAD_DIM // 2
    x1, x2 = x[:, :half], x[:, half:]
    return jnp.concatenate([x1 * cos - x2 * sin, x1 * sin + x2 * cos], axis=-1)


def _retention_chunk(lg, sdec, s, q, k, v, g, cos, sin, gn):
    n = q.shape[0]
    ri = lax.broadcasted_iota(jnp.int32, (n, n), 0)
    ci = lax.broadcasted_iota(jnp.int32, (n, n), 1)
    diff = (ri - ci).astype(F32)
    mask = jnp.where(diff >= 0, jnp.exp(jnp.maximum(diff, 0.0) * lg), 0.0)
    row = lax.broadcasted_iota(jnp.int32, (n, 1), 0).astype(F32)
    q_dec = jnp.exp(lg * (row + 1.0))
    k_dec = jnp.exp(lg * (n - 1.0 - row))

    qr = _rotary(q, cos, sin)
    kr = _rotary(k, cos, sin) * (RET_HEAD_DIM ** -0.5)
    vb = v.astype(BF16)
    scores = lax.dot_general(qr.astype(BF16), kr.astype(BF16), (((1,), (1,)), ((), ())),
                             preferred_element_type=F32) * mask
    o = _dot(scores.astype(BF16), vb) + _dot((qr * q_dec).astype(BF16), s.astype(BF16))
    s_new = sdec * s + lax.dot_general((kr * k_dec).astype(BF16), vb, (((0,), (0,)), ((), ())),
                                       preferred_element_type=F32)
    mu = jnp.mean(o, axis=-1, keepdims=True)
    oc = o - mu
    var = jnp.mean(oc * oc, axis=-1, keepdims=True)
    y = oc * lax.rsqrt(var + GN_EPS) * gn
    return y * jax.nn.silu(g), s_new


def _ret_seq_kernel(dec_ref, q_ref, k_ref, v_ref, g_ref, cos_ref, sin_ref, s0_ref, gn_ref,
                    o_ref, sf_ref, s_scr, *, chunk, nchunks):
    h = pl.program_id(1)
    lg = dec_ref[0, h]
    sdec = dec_ref[1, h]
    s_scr[...] = s0_ref[...]
    gn = gn_ref[...]

    def body(c, carry):
        rows = pl.ds(pl.multiple_of(c * chunk, chunk), chunk)
        y, s_new = _retention_chunk(lg, sdec, s_scr[...], q_ref[rows, :], k_ref[rows, :],
                                    v_ref[rows, :], g_ref[rows, :], cos_ref[rows, :],
                                    sin_ref[rows, :], gn)
        o_ref[rows, :] = y.astype(o_ref.dtype)
        s_scr[...] = s_new
        return carry

    lax.fori_loop(0, nchunks, body, 0, unroll=min(8, nchunks))
    sf_ref[...] = s_scr[...]


def _ret_decay_consts(chunk):
    lg = np.log(1.0 - 2.0 ** (-5.0 - np.arange(RET_HEADS, dtype=np.float64)))
    return jnp.asarray(np.stack([lg, np.exp(lg * chunk)]), dtype=F32)


def _rope_tables(pos):
    half = RET_HEAD_DIM // 2
    inv = ROPE_BASE ** (-np.arange(half, dtype=np.float64) / half)
    ang = np.asarray(pos, dtype=np.float64)[:, None] * inv[None, :]
    return jnp.asarray(np.cos(ang), dtype=F32), jnp.asarray(np.sin(ang), dtype=F32)


def _retention_seq(proj, row_block0, nbatch, seq, chunk, pos0, s0, gn_g):
    cos, sin = _rope_tables(pos0 + np.arange(seq))
    hd = RET_HEAD_DIM

    def col(off):
        return pl.BlockSpec((seq, hd), lambda b, h: (row_block0 + b, off + h))

    tbl = pl.BlockSpec((seq, hd // 2), lambda b, h: (0, 0))
    return pl.pallas_call(
        functools.partial(_ret_seq_kernel, chunk=chunk, nchunks=seq // chunk),
        grid=(nbatch, RET_HEADS),
        in_specs=[pl.BlockSpec(memory_space=pltpu.SMEM),
                  col(0), col(RET_HEADS), col(2 * RET_HEADS), col(3 * RET_HEADS), tbl, tbl,
                  pl.BlockSpec((None, None, hd, hd), lambda b, h: (0, h, 0, 0)),
                  pl.BlockSpec((1, hd), lambda b, h: (0, h))],
        out_specs=[pl.BlockSpec((seq, hd), lambda b, h: (b, h)),
                   pl.BlockSpec((None, None, hd, hd), lambda b, h: (b, h, 0, 0))],
        out_shape=[jax.ShapeDtypeStruct((nbatch * seq, RET_WIDTH), BF16),
                   jax.ShapeDtypeStruct((nbatch, RET_HEADS, hd, hd), F32)],
        scratch_shapes=[pltpu.VMEM((hd, hd), F32)],
        compiler_params=_params("parallel", "arbitrary"),
        name="retention_seq",
    )(_ret_decay_consts(chunk), proj, proj, proj, proj, cos, sin, s0, gn_g.reshape(1, RET_WIDTH))


def _ret_step_kernel(dec_ref, q_ref, k_ref, v_ref, g_ref, cos_ref, sin_ref, s_ref, gn_ref,
                     o_ref, sn_ref, *, nb, seq):
    hd = RET_HEAD_DIM
    cos, sin = cos_ref[...], sin_ref[...]
    for b in range(nb):
        rows = slice(b * seq, (b + 1) * seq)
        for h in range(RET_HEADS):
            cols = slice(h * hd, (h + 1) * hd)
            y, s_new = _retention_chunk(dec_ref[0, h], dec_ref[1, h], s_ref[b, h],
                                        q_ref[rows, cols], k_ref[rows, cols], v_ref[rows, cols],
                                        g_ref[rows, cols], cos, sin, gn_ref[:, cols])
            o_ref[rows, cols] = y.astype(o_ref.dtype)
            sn_ref[b, h] = s_new


def _retention_step(proj, row0, nbatch, seq, pos0, state, gn_g, nb):
    cos, sin = _rope_tables(pos0 + np.arange(seq))
    hd = RET_HEAD_DIM
    rb0 = row0 // (nb * seq)

    def col(off):
        return pl.BlockSpec((nb * seq, RET_WIDTH), lambda i: (rb0 + i, off))

    tbl = pl.BlockSpec((seq, hd // 2), lambda i: (0, 0))
    st = pl.BlockSpec((nb, RET_HEADS, hd, hd), lambda i: (i, 0, 0, 0))
    return pl.pallas_call(
        functools.partial(_ret_step_kernel, nb=nb, seq=seq),
        grid=(nbatch // nb,),
        in_specs=[pl.BlockSpec(memory_space=pltpu.SMEM),
                  col(0), col(1), col(2), col(3), tbl, tbl, st,
                  pl.BlockSpec((1, RET_WIDTH), lambda i: (0, 0))],
        out_specs=[pl.BlockSpec((nb * seq, RET_WIDTH), lambda i: (i, 0)), st],
        out_shape=[jax.ShapeDtypeStruct((nbatch * seq, RET_WIDTH), BF16),
                   jax.ShapeDtypeStruct(state.shape, F32)],
        compiler_params=_params("parallel"),
        name="retention_step",
    )(_ret_decay_consts(seq), proj, proj, proj, proj, cos, sin, state,
      gn_g.reshape(1, RET_WIDTH))


def _s5_disc_kernel(lr_ref, li_ref, ldt_ref, br_ref, bi_ref, ar_ref, ai_ref, bbr_ref, bbi_ref):
    lr, li = lr_ref[...], li_ref[...]
    dt = jnp.exp(ldt_ref[...])
    mag = jnp.exp(lr * dt)
    ar = mag * jnp.cos(li * dt)
    ai = mag * jnp.sin(li * dt)
    nr, ni = ar - 1.0, ai
    den = lr * lr + li * li
    fr = (nr * lr + ni * li) / den
    fi = (ni * lr - nr * li) / den
    br, bi = br_ref[...], bi_ref[...]
    ar_ref[...] = ar
    ai_ref[...] = ai
    bbr_ref[...] = fr * br - fi * bi
    bbi_ref[...] = fr * bi + fi * br


def _s5_discretize(lam_re, lam_im, log_dt, b_re, b_im):
    g, p, c = b_re.shape
    shp3 = jax.ShapeDtypeStruct((g, 1, p), F32)
    shpb = jax.ShapeDtypeStruct((g, c, p), F32)
    return pl.pallas_call(
        _s5_disc_kernel,
        out_shape=[shp3, shp3, shpb, shpb],
        name="s5_discretize",
    )(lam_re.reshape(g, 1, p), lam_im.reshape(g, 1, p), log_dt.reshape(g, 1, 1),
      jnp.swapaxes(b_re, 1, 2), jnp.swapaxes(b_im, 1, 2))


S5_NPAIR = S5_SLABS // 4


def _lanes(s):
    return slice(s * LANES, (s + 1) * LANES)


def _s5_place(s, nb, pitch):
    comp, ls = s // (2 * S5_NPAIR), s % (2 * S5_NPAIR)
    hf, p = ls // S5_NPAIR, ls % S5_NPAIR
    return comp * S5_NPAIR + p, hf * (nb * pitch)


def _s5_paired(ref, p, nb, order=None):
    def slab(hf):
        cols = _lanes(p + hf * S5_NPAIR)
        if order is None or ref.shape[0] == 1:
            return jnp.broadcast_to(ref[:, cols], (nb, LANES))
        return jnp.concatenate([ref[o:o + 1, cols] for o in order], axis=0)
    return jnp.concatenate([slab(0), slab(1)], axis=0)


def _s5_block_weights(bbr_ref, bbi_ref, cr_ref, ci_ref, bw_scr, cw_scr):
    bw_scr[...] = jnp.zeros_like(bw_scr)
    cw_scr[...] = jnp.zeros_like(cw_scr)
    for g in range(S5_GB):
        ch = slice(g * S5_GROUP, (g + 1) * S5_GROUP)
        for comp, (b_ref, c_ref, sign) in enumerate(((bbr_ref, cr_ref, 1.0), (bbi_ref, ci_ref, -1.0))):
            st = slice(comp * S5_ST + g * S5_STATE, comp * S5_ST + (g + 1) * S5_STATE)
            bw_scr[ch, st] = b_ref[g]
            cw_scr[st, ch] = sign * c_ref[g]


def _s5_step(buf, t, xs, ar, ai, nb, pitch):
    rows = pl.ds(t, 2 * nb, stride=pitch)
    new = [None] * (2 * S5_NPAIR)
    for p in range(S5_NPAIR):
        xr, xi = xs[p], xs[S5_NPAIR + p]
        nr = ar[p] * xr - ai[p] * xi + buf[p, rows, :]
        ni = ar[p] * xi + ai[p] * xr + buf[S5_NPAIR + p, rows, :]
        buf[p, rows, :] = nr
        buf[S5_NPAIR + p, rows, :] = ni
        new[p], new[S5_NPAIR + p] = nr, ni
    return tuple(new)


def _s5_scan_io(x0r_ref, x0i_ref, ar_ref, ai_ref, nb, order=None):
    ar = [_s5_paired(ar_ref, p, nb) for p in range(S5_NPAIR)]
    ai = [_s5_paired(ai_ref, p, nb) for p in range(S5_NPAIR)]
    xs0 = tuple([_s5_paired(x0r_ref, p, nb, order) for p in range(S5_NPAIR)]
                + [_s5_paired(x0i_ref, p, nb, order) for p in range(S5_NPAIR)])
    return ar, ai, xs0


def _s5_store_final(xs, xfr_ref, xfi_ref, nb, order=None):
    for p in range(S5_NPAIR):
        for hf in range(2):
            cols = _lanes(p + hf * S5_NPAIR)
            for x, ref in ((xs[p], xfr_ref), (xs[S5_NPAIR + p], xfi_ref)):
                if order is None:
                    ref[:, cols] = x[hf * nb:(hf + 1) * nb]
                else:
                    for i, o in enumerate(order):
                        ref[o:o + 1, cols] = x[hf * nb + i:hf * nb + i + 1]


def _s5_kernel(u_ref, x0r_ref, x0i_ref, ar_ref, ai_ref, bbr_ref, bbi_ref, cr_ref, ci_ref, d_ref,
               z_ref, xfr_ref, xfi_ref, buf, bw_scr, cw_scr, *, nb, seq):
    _s5_block_weights(bbr_ref, bbi_ref, cr_ref, ci_ref, bw_scr, cw_scr)
    ar, ai, xs = _s5_scan_io(x0r_ref, x0i_ref, ar_ref, ai_ref, nb)
    u = u_ref[...]
    bu = _dot(u.astype(BF16), bw_scr[...].astype(BF16))
    for s in range(S5_SLABS):
        q, off = _s5_place(s, nb, seq)
        buf[q, off:off + nb * seq, :] = bu[:, _lanes(s)]
    xs = lax.fori_loop(0, seq, lambda t, xs: _s5_step(buf, t, xs, ar, ai, nb, seq), xs,
                       unroll=8 if nb <= 8 else 1)
    parts = []
    for s in range(S5_SLABS):
        q, off = _s5_place(s, nb, seq)
        parts.append(buf[q, off:off + nb * seq, :])
    y = _dot(jnp.concatenate(parts, axis=1).astype(BF16), cw_scr[...].astype(BF16)) + d_ref[...] * u
    z_ref[...] = jax.nn.gelu(y).astype(z_ref.dtype)
    _s5_store_final(xs, xfr_ref, xfi_ref, nb)


def _s5_chunked_kernel(u_ref, x0r_ref, x0i_ref, ar_ref, ai_ref, bbr_ref, bbi_ref, cr_ref, ci_ref,
                       d_ref, z_ref, xfr_ref, xfi_ref, buf, bw_scr, cw_scr, *, nb, seq, tc, pitch):
    lead_odd = pitch % 8
    assert nb % 2 == 0 and nb >= 4 and lead_odd == 4 and (nb * pitch) % 8 == 0
    inner = list(range(1, nb - 1))
    even, odd = [0, nb - 1] + inner[nb // 2:], inner[:nb // 2]
    order = [(odd if i % 2 else even)[i // 2] for i in range(nb)]

    _s5_block_weights(bbr_ref, bbi_ref, cr_ref, ci_ref, bw_scr, cw_scr)
    bw = bw_scr[...].astype(BF16)
    cw = cw_scr[...].astype(BF16)
    ar, ai, xs = _s5_scan_io(x0r_ref, x0i_ref, ar_ref, ai_ref, nb, order)
    dvec = d_ref[...]

    def chunk(c, xs):
        t0 = c * tc
        rows = [pl.ds(pl.multiple_of(order[i] * seq + t0, tc), tc) for i in range(nb)]
        lead = [lead_odd if i % 2 else 0 for i in range(nb)]

        def u_window(i):
            if not lead[i]:
                return u_ref[rows[i], :]
            wide = u_ref[pl.ds(pl.multiple_of(order[i] * seq + t0 - 8, 8), tc + 16), :]
            return wide[8 - lead[i]:8 + tc + lead[i]]

        for i in range(nb):
            bu = _dot(u_window(i).astype(BF16), bw)
            for s in range(S5_SLABS):
                q, off = _s5_place(s, nb, pitch)
                r0 = off + i * pitch - lead[i]
                buf[q, r0:r0 + tc + 2 * lead[i], :] = bu[:, _lanes(s)]
        xs = lax.fori_loop(0, tc, lambda t, xs: _s5_step(buf, t, xs, ar, ai, nb, pitch), xs,
                           unroll=8)
        for i in range(nb):
            parts = []
            for s in range(S5_SLABS):
                q, off = _s5_place(s, nb, pitch)
                r0 = off + i * pitch - lead[i]
                parts.append(buf[q, r0:r0 + tc + 2 * lead[i], :])
            y = _dot(jnp.concatenate(parts, axis=1).astype(BF16), cw)[lead[i]:lead[i] + tc]
            y = y + dvec * u_ref[rows[i], :]
            z_ref[rows[i], :] = jax.nn.gelu(y).astype(z_ref.dtype)
        return xs

    xs = lax.fori_loop(0, seq // tc, chunk, xs)
    _s5_store_final(xs, xfr_ref, xfi_ref, nb, order)


def _s5(proj, row_block0, nb, seq, tc, x0r, x0i, ar, ai, bbr, bbi, c_re, c_im, d):
    rows = nb * seq
    u_col0 = 4 * RET_WIDTH // S5_CH
    nj = S5_GROUPS // S5_GB
    st = pl.BlockSpec((nb, S5_ST), lambda j: (0, j))
    coef = pl.BlockSpec((1, S5_ST), lambda j: (0, j))
    bspec = pl.BlockSpec((S5_GB, S5_GROUP, S5_STATE), lambda j: (j, 0, 0))
    cspec = pl.BlockSpec((S5_GB, S5_STATE, S5_GROUP), lambda j: (j, 0, 0))
    weights = [pltpu.VMEM((S5_CH, 2 * S5_ST), F32), pltpu.VMEM((2 * S5_ST, S5_CH), F32)]
    if tc == seq:
        body = functools.partial(_s5_kernel, nb=nb, seq=seq)
        pitch = seq
    else:
        pitch = tc + 4
        body = functools.partial(_s5_chunked_kernel, nb=nb, seq=seq, tc=tc, pitch=pitch)
    scratch = [pltpu.VMEM((S5_SLABS // 2, 2 * nb * pitch, LANES), F32)] + weights
    return pl.pallas_call(
        body,
        grid=(nj,),
        in_specs=[pl.BlockSpec((rows, S5_CH), lambda j: (row_block0, u_col0 + j)),
                  st, st, coef, coef, bspec, bspec, cspec, cspec,
                  pl.BlockSpec((1, S5_CH), lambda j: (0, j))],
        out_specs=[pl.BlockSpec((rows, S5_CH), lambda j: (0, j)), st, st],
        out_shape=[jax.ShapeDtypeStruct((rows, S5_WIDTH), BF16),
                   jax.ShapeDtypeStruct((nb, S5_NSTATE), F32),
                   jax.ShapeDtypeStruct((nb, S5_NSTATE), F32)],
        scratch_shapes=scratch,
        compiler_params=_params("parallel"),
        name="s5_scan",
    )(proj, x0r, x0i, ar, ai, bbr, bbi, c_re, c_im, d.reshape(1, S5_WIDTH))


def _s5_post_kernel(zp_ref, zs_ref, w_ref, b_ref, g_ref, o_ref, z_scr, gate_scr, ssq_scr,
                    *, n_first, tn):
    i, j = pl.program_id(0), pl.program_id(1)
    nj = gate_scr.shape[0]

    @pl.when(j == 0)
    def _():
        z = jnp.where(i < n_first, zp_ref[...], zs_ref[...])
        for jj in range(nj):
            z_scr[jj] = z[:, jj * tn:(jj + 1) * tn]
        ssq_scr[...] = jnp.zeros_like(ssq_scr)

    z = jnp.concatenate([z_scr[jj] for jj in range(nj)], axis=1)
    t = _dot(z, w_ref[...].astype(BF16)) + b_ref[...]
    s = z_scr[j].astype(F32) * jax.nn.sigmoid(t)
    gate_scr[j] = s
    ssq_scr[...] += jnp.sum(s * s, axis=-1, keepdims=True)

    @pl.when(j == nj - 1)
    def _():
        r = lax.rsqrt(ssq_scr[...] / (tn * nj) + EPS)
        for jj in range(nj):
            cols = slice(jj * tn, (jj + 1) * tn)
            o_ref[:, cols] = (gate_scr[jj] * r * g_ref[:, cols]).astype(o_ref.dtype)


def _s5_post(zp, zs, w_glu, b_glu, norm_g, tm, tn):
    d = zp.shape[1]
    n_first = zp.shape[0] // tm
    nblk = n_first + zs.shape[0] // tm
    nj = d // tn
    return pl.pallas_call(
        functools.partial(_s5_post_kernel, n_first=n_first, tn=tn),
        grid=(nblk, nj),
        in_specs=[pl.BlockSpec((tm, d), lambda i, j: (jnp.minimum(i, n_first - 1), 0)),
                  pl.BlockSpec((tm, d), lambda i, j: (jnp.maximum(i - n_first, 0), 0)),
                  pl.BlockSpec((d, tn), lambda i, j: (0, j)),
                  pl.BlockSpec((1, tn), lambda i, j: (0, j)),
                  pl.BlockSpec((1, d), lambda i, j: (0, 0))],
        out_specs=pl.BlockSpec((tm, d), lambda i, j: (i, 0)),
        out_shape=jax.ShapeDtypeStruct((nblk * tm, d), BF16),
        scratch_shapes=[pltpu.VMEM((nj, tm, tn), BF16), pltpu.VMEM((nj, tm, tn), F32),
                        pltpu.VMEM((tm, 1), F32)],
        compiler_params=_params("parallel", "arbitrary"),
        name="s5_post",
    )(zp, zs, w_glu, b_glu.reshape(1, d), norm_g.reshape(1, d))


def kernel(x_prompt, x_sample, state_ret, state_s5_re, state_s5_im, meta_tokens, norm1_g, w_in, ret_gn_g, s5_lam_re, s5_lam_im, s5_log_dt, s5_b_re, s5_b_im, s5_c_re, s5_c_im, s5_d, w_glu, b_glu, s5_norm_g, w_out, norm2_g, w_gate, w_up, w_down, final_norm_g):
    assert norm1_g.shape[0] == 1, "single-layer model"
    batch, seq, d = x_prompt.shape
    dbatch, dseq, _ = x_sample.shape
    rows_p, rows_s = batch * seq, dbatch * dseq
    tm = 1024
    assert rows_p % tm == 0 and rows_s % tm == 0 and seq % RET_CHUNK == 0
    xp = x_prompt.reshape(rows_p, d)
    xs = x_sample.reshape(rows_s, d)
    hd = RET_HEAD_DIM

    ar, ai, bbr, bbi = _s5_discretize(s5_lam_re[0], s5_lam_im[0], s5_log_dt[0], s5_b_re[0], s5_b_im[0])
    ar = ar.reshape(1, S5_NSTATE)
    ai = ai.reshape(1, S5_NSTATE)
    s5_w = (ar, ai, bbr, bbi, jnp.swapaxes(s5_c_re[0], 1, 2), jnp.swapaxes(s5_c_im[0], 1, 2), s5_d[0])

    h_meta = _rmsnorm([meta_tokens], norm1_g[0], N_META, BF16)
    h = _rmsnorm([xp, xs], norm1_g[0], 512, BF16)
    proj, proj_meta = _inproj(h, h_meta, w_in[0], 1536, 512)

    zero_ret = jnp.zeros((1, RET_HEADS, hd, hd), F32)
    _, s_meta = _retention_seq(proj_meta, 0, 1, N_META, N_META, 0, zero_ret, ret_gn_g[0])
    zero_s5 = jnp.zeros((1, S5_NSTATE), F32)
    _, mr, mi = _s5(proj_meta, 0, 1, N_META, N_META, zero_s5, zero_s5, *s5_w)

    ret_p, sret_p = _retention_seq(proj, 0, batch, seq, RET_CHUNK, N_META, s_meta, ret_gn_g[0])
    ret_s, sret_s = _retention_step(proj, rows_p, dbatch, dseq, PAST_LEN, state_ret[0],
                                    ret_gn_g[0], 4)

    z_p, s5r_p, s5i_p = _s5(proj, 0, batch, seq, 256,
                            jnp.broadcast_to(mr, (batch, S5_NSTATE)),
                            jnp.broadcast_to(mi, (batch, S5_NSTATE)), *s5_w)
    z_s, s5r_s, s5i_s = _s5(proj, rows_p // rows_s, dbatch, dseq, dseq,
                            state_s5_re[0].reshape(dbatch, S5_NSTATE),
                            state_s5_im[0].reshape(dbatch, S5_NSTATE), *s5_w)
    s5_out = _s5_post(z_p, z_s, w_glu[0], b_glu[0], s5_norm_g[0], tm, 512)

    x1, x1g, x1_ssq = _outproj(ret_p, ret_s, s5_out, w_out[0], xp, xs, norm2_g[0], tm, 512)
    a = _ffn_up(x1g, x1_ssq, w_gate[0], w_up[0], tm, 256)
    k_split = (w_down.shape[1] // 512 + 1) * 256
    x2 = _ffn_down(a, w_down[0], x1, 0, k_split, tm, 256)
    x2 = _ffn_down(a, w_down[0], x2, k_split, w_down.shape[1] - k_split, tm, 256)
    y_p, y_s = _final_norm(x2, final_norm_g, rows_p, 512)

    st = (1, -1, S5_GROUPS, S5_STATE)
    return (y_p.reshape(batch, seq, d), y_s.reshape(dbatch, dseq, d),
            sret_p[None], s5r_p.reshape(st), s5i_p.reshape(st),
            sret_s[None], s5r_s.reshape(st), s5i_s.reshape(st))
```

```python
import functools

import numpy as np
import jax
import jax.numpy as jnp
from jax import lax
from jax.experimental import pallas as pl
from jax.experimental.pallas import tpu as pltpu

D_MODEL = 4096
N_META = 16
PAST_LEN = 16384
RET_WIDTH = D_MODEL // 2
S5_WIDTH = D_MODEL - RET_WIDTH
RET_HEADS = 8
RET_HEAD_DIM = RET_WIDTH // RET_HEADS
RET_CHUNK = 128
ROPE_BASE = 10000.0
S5_GROUP = 16
S5_GROUPS = S5_WIDTH // S5_GROUP
S5_STATE = 64
S5_NSTATE = S5_GROUPS * S5_STATE
IN_COLS = 4 * RET_WIDTH + S5_WIDTH
EPS = 1e-6
GN_EPS = 1e-5

LANES = 128
S5_GB = 16
S5_CH = S5_GB * S5_GROUP
S5_ST = S5_GB * S5_STATE
S5_SLABS = 2 * S5_ST // LANES
VMEM_LIMIT = 56 * 1024 * 1024

BF16 = jnp.bfloat16
F32 = jnp.float32


def _params(*sem):
    return pltpu.CompilerParams(dimension_semantics=sem, vmem_limit_bytes=VMEM_LIMIT)


def _dot(a, b):
    return jnp.dot(a, b, preferred_element_type=F32)


def _rmsnorm_kernel(*refs, n_src, n_first):
    x_refs, g_ref, o_ref = refs[:n_src], refs[n_src], refs[n_src + 1]
    x = x_refs[0][...]
    if n_src == 2:
        x = jnp.where(pl.program_id(0) < n_first, x, x_refs[1][...])
    r = lax.rsqrt(jnp.mean(x * x, axis=-1, keepdims=True) + EPS)
    o_ref[...] = (x * r * g_ref[...]).astype(o_ref.dtype)


def _rmsnorm(srcs, g, tm, out_dtype):
    d = srcs[0].shape[1]
    nblk = [s.shape[0] // tm for s in srcs]
    n_first = nblk[0]
    specs = [pl.BlockSpec((tm, d), lambda i: (jnp.minimum(i, n_first - 1), 0))]
    if len(srcs) == 2:
        specs.append(pl.BlockSpec((tm, d), lambda i: (jnp.maximum(i - n_first, 0), 0)))
    rows = sum(s.shape[0] for s in srcs)
    return pl.pallas_call(
        functools.partial(_rmsnorm_kernel, n_src=len(srcs), n_first=n_first),
        grid=(sum(nblk),),
        in_specs=specs + [pl.BlockSpec((1, d), lambda i: (0, 0))],
        out_specs=pl.BlockSpec((tm, d), lambda i: (i, 0)),
        out_shape=jax.ShapeDtypeStruct((rows, d), out_dtype),
        compiler_params=_params("parallel"),
        name="rmsnorm",
    )(*srcs, g.reshape(1, d))


def _final_norm_kernel(x_ref, g_ref, yp_ref, ys_ref, *, n_first):
    x = x_ref[...]
    r = lax.rsqrt(jnp.mean(x * x, axis=-1, keepdims=True) + EPS)
    y = x * r * g_ref[...]
    i = pl.program_id(0)

    @pl.when(i < n_first)
    def _():
        yp_ref[...] = y

    @pl.when(i >= n_first)
    def _():
        ys_ref[...] = y


def _final_norm(x, g, rows_p, tm):
    rows, d = x.shape
    n_first = rows_p // tm
    return pl.pallas_call(
        functools.partial(_final_norm_kernel, n_first=n_first),
        grid=(rows // tm,),
        in_specs=[pl.BlockSpec((tm, d), lambda i: (i, 0)),
                  pl.BlockSpec((1, d), lambda i: (0, 0))],
        out_specs=[pl.BlockSpec((tm, d), lambda i: (jnp.minimum(i, n_first - 1), 0)),
                   pl.BlockSpec((tm, d), lambda i: (jnp.maximum(i - n_first, 0), 0))],
        out_shape=[jax.ShapeDtypeStruct((rows_p, d), F32),
                   jax.ShapeDtypeStruct((rows - rows_p, d), F32)],
        compiler_params=_params("arbitrary"),
        name="final_norm",
    )(x, g.reshape(1, d))


def _inproj_kernel(h_ref, hx_ref, w_ref, o_ref, ox_ref):
    w = w_ref[...].astype(BF16)
    o_ref[...] = _dot(h_ref[...], w)

    @pl.when(pl.program_id(0) == 0)
    def _():
        ox_ref[...] = _dot(hx_ref[...], w)


def _inproj(h, h_extra, w, tm, tn):
    m, k = h.shape
    mx = h_extra.shape[0]
    n = w.shape[1]
    nj = n // tn
    return pl.pallas_call(
        _inproj_kernel,
        grid=(m // tm, nj),
        in_specs=[pl.BlockSpec((tm, k), lambda i, j: (i, 0)),
                  pl.BlockSpec((mx, k), lambda i, j: (0, 0)),
                  pl.BlockSpec((k, tn), lambda i, j: (0, j))],
        out_specs=[pl.BlockSpec((tm, tn), lambda i, j: (i, j)),
                   pl.BlockSpec((mx, tn), lambda i, j: (0, jnp.where(i == 0, j, nj - 1)))],
        out_shape=[jax.ShapeDtypeStruct((m, n), F32), jax.ShapeDtypeStruct((mx, n), F32)],
        compiler_params=_params("arbitrary", "arbitrary"),
        name="inproj",
    )(h, h_extra, w)


def _outproj_kernel(retp_hbm, rets_hbm, s5_hbm, w_ref, xp_ref, xs_ref, g_ref,
                    o_ref, xg_ref, ssq_ref, mix_buf, mix_sem, *, n_first, tm):
    i, j = pl.program_id(0), pl.program_id(1)
    kh = mix_buf.shape[2] // 2
    slot = i % 2

    def ret_copy(block, s, from_prompt):
        src = retp_hbm if from_prompt else rets_hbm
        row0 = (block if from_prompt else block - n_first) * tm
        return pltpu.make_async_copy(src.at[pl.ds(pl.multiple_of(row0, tm), tm)],
                                     mix_buf.at[s, :, pl.ds(0, kh)], mix_sem.at[s, 0])

    def s5_copy(block, s):
        return pltpu.make_async_copy(s5_hbm.at[pl.ds(pl.multiple_of(block * tm, tm), tm)],
                                     mix_buf.at[s, :, pl.ds(kh, kh)], mix_sem.at[s, 1])

    def each_copy(block, s, act):
        @pl.when(block < n_first)
        def _():
            act(ret_copy(block, s, True))

        @pl.when(block >= n_first)
        def _():
            act(ret_copy(block, s, False))

        act(s5_copy(block, s))

    @pl.when((i == 0) & (j == 0))
    def _():
        each_copy(0, 0, lambda c: c.start())

    @pl.when(j == 0)
    def _():
        each_copy(i, slot, lambda c: c.wait())

        @pl.when(i + 1 < pl.num_programs(0))
        def _():
            each_copy(i + 1, 1 - slot, lambda c: c.start())

    x = jnp.where(i < n_first, xp_ref[...], xs_ref[...])
    x1 = x + _dot(mix_buf[slot], w_ref[...].astype(BF16))
    o_ref[...] = x1
    xg_ref[...] = (x1 * g_ref[...]).astype(xg_ref.dtype)
    sq = x1 * x1
    part = sq[:, :LANES]
    for c in range(1, sq.shape[1] // LANES):
        part += sq[:, c * LANES:(c + 1) * LANES]

    @pl.when(j == 0)
    def _():
        ssq_ref[...] = part

    @pl.when(j != 0)
    def _():
        ssq_ref[...] += part


def _outproj(ret_p, ret_s, s5, w_out, xp, xs, g, tm, tn):
    m, kh = s5.shape
    k, n = w_out.shape
    assert ret_p.shape[1] == kh and k == 2 * kh
    n_first = xp.shape[0] // tm
    nj = n // tn
    hbm = pl.BlockSpec(memory_space=pl.ANY)
    return pl.pallas_call(
        functools.partial(_outproj_kernel, n_first=n_first, tm=tm),
        grid=(m // tm, nj),
        in_specs=[hbm, hbm, hbm,
                  pl.BlockSpec((k, tn), lambda i, j: (0, j)),
                  pl.BlockSpec((tm, tn), lambda i, j: (jnp.minimum(i, n_first - 1),
                                                       jnp.where(i < n_first, j, nj - 1))),
                  pl.BlockSpec((tm, tn), lambda i, j: (jnp.maximum(i - n_first, 0),
                                                       jnp.where(i < n_first, 0, j)),
                               pipeline_mode=pl.Buffered(1)),
                  pl.BlockSpec((1, tn), lambda i, j: (0, j))],
        out_specs=[pl.BlockSpec((tm, tn), lambda i, j: (i, j)),
                   pl.BlockSpec((tm, tn), lambda i, j: (i, j)),
                   pl.BlockSpec((tm, LANES), lambda i, j: (i, 0))],
        out_shape=[jax.ShapeDtypeStruct((m, n), F32),
                   jax.ShapeDtypeStruct((m, n), BF16),
                   jax.ShapeDtypeStruct((m, LANES), F32)],
        scratch_shapes=[pltpu.VMEM((2, tm, k), s5.dtype), pltpu.SemaphoreType.DMA((2, 2))],
        compiler_params=_params("arbitrary", "arbitrary"),
        name="outproj",
    )(ret_p, ret_s, s5, w_out, xp, xs, g.reshape(1, n))


def _ffn_up_kernel(xg_ref, ssq_ref, wg_ref, wu_ref, o_ref, r_scr):
    tf = wg_ref.shape[1]

    @pl.when(pl.program_id(1) == 0)
    def _():
        r_scr[...] = lax.rsqrt(jnp.sum(ssq_ref[...], axis=-1, keepdims=True) / xg_ref.shape[1] + EPS)

    w = jnp.concatenate([wg_ref[...].astype(BF16), wu_ref[...].astype(BF16)], axis=1)
    gu = _dot(xg_ref[...], w) * r_scr[...]
    o_ref[...] = (jax.nn.silu(gu[:, :tf]) * gu[:, tf:]).astype(o_ref.dtype)


def _ffn_up(xg, ssq, w_gate, w_up, tm, tf):
    m, k = xg.shape
    f = w_gate.shape[1]
    return pl.pallas_call(
        _ffn_up_kernel,
        grid=(m // tm, f // tf),
        in_specs=[pl.BlockSpec((tm, k), lambda i, j: (i, 0)),
                  pl.BlockSpec((tm, LANES), lambda i, j: (i, 0)),
                  pl.BlockSpec((k, tf), lambda i, j: (0, j)),
                  pl.BlockSpec((k, tf), lambda i, j: (0, j))],
        out_specs=pl.BlockSpec((tm, tf), lambda i, j: (i, j)),
        out_shape=jax.ShapeDtypeStruct((m, f), BF16),
        scratch_shapes=[pltpu.VMEM((tm, 1), F32)],
        compiler_params=_params("parallel", "arbitrary"),
        name="ffn_up",
    )(xg, ssq, w_gate, w_up)


def _ffn_down_kernel(a_hbm, w_ref, x_ref, o_ref, a_lo, a_hi, a_sem, acc, *, tm):
    i, h, j = pl.program_id(0), pl.program_id(1), pl.program_id(2)
    k_lo, k_hi = a_lo.shape[1], a_hi.shape[1]

    def a_copy(block, half):
        rows = pl.ds(pl.multiple_of(block * tm, tm), tm)
        cols, buf = (pl.ds(k_lo, k_hi), a_hi) if half else (pl.ds(0, k_lo), a_lo)
        return pltpu.make_async_copy(a_hbm.at[rows, cols], buf, a_sem.at[half])

    @pl.when((i == 0) & (h == 0) & (j == 0))
    def _():
        a_copy(0, 0).start()

    @pl.when((h == 0) & (j == 0))
    def _():
        a_copy(i, 0).wait()
        a_copy(i, 1).start()

    @pl.when((h == 1) & (j == 0))
    def _():
        a_copy(i, 1).wait()

        @pl.when(i + 1 < pl.num_programs(0))
        def _():
            a_copy(i + 1, 0).start()

    @pl.when(h == 0)
    def _():
        acc[j] = x_ref[...] + _dot(a_lo[...], w_ref[...].astype(BF16))

    @pl.when(h == 1)
    def _():
        o_ref[...] = acc[j] + _dot(a_hi[...], w_ref[k_lo - k_hi:, :].astype(BF16))


def _ffn_down(a, w_down, x, k_lo, tm, tn):
    m, f = a.shape
    n = w_down.shape[1]
    nj = n // tn
    assert 2 * k_lo >= f
    return pl.pallas_call(
        functools.partial(_ffn_down_kernel, tm=tm),
        grid=(m // tm, 2, nj),
        in_specs=[pl.BlockSpec(memory_space=pl.ANY),
                  pl.BlockSpec((pl.Element(k_lo), pl.Element(tn)),
                               lambda i, h, j: (h * (f - k_lo), j * tn)),
                  pl.BlockSpec((tm, tn), lambda i, h, j: (i, jnp.where(h == 0, j, nj - 1)))],
        out_specs=pl.BlockSpec((tm, tn), lambda i, h, j: (i, jnp.where(h == 0, 0, j))),
        out_shape=jax.ShapeDtypeStruct((m, n), F32),
        scratch_shapes=[pltpu.VMEM((tm, k_lo), a.dtype), pltpu.VMEM((tm, f - k_lo), a.dtype),
                        pltpu.SemaphoreType.DMA((2,)), pltpu.VMEM((nj, tm, tn), F32)],
        compiler_params=_params("arbitrary", "arbitrary", "arbitrary"),
        name="ffn_down",
    )(a, w_down, x)


def _rotary(x, cos, sin):
    half = RET_HEAD_DIM // 2
    x1, x2 = x[:, :half], x[:, half:]
    return jnp.concatenate([x1 * cos - x2 * sin, x1 * sin + x2 * cos], axis=-1)


def _retention_chunk(lg, sdec, s, q, k, v, g, cos, sin, gn):
    n = q.shape[0]
    ri = lax.broadcasted_iota(jnp.int32, (n, n), 0)
    ci = lax.broadcasted_iota(jnp.int32, (n, n), 1)
    diff = (ri - ci).astype(F32)
    mask = jnp.where(diff >= 0, jnp.exp(jnp.maximum(diff, 0.0) * lg), 0.0)
    row = lax.broadcasted_iota(jnp.int32, (n, 1), 0).astype(F32)
    q_dec = jnp.exp(lg * (row + 1.0))
    k_dec = jnp.exp(lg * (n - 1.0 - row))

    qr = _rotary(q, cos, sin)
    kr = _rotary(k, cos, sin) * (RET_HEAD_DIM ** -0.5)
    vb = v.astype(BF16)
    scores = lax.dot_general(qr.astype(BF16), kr.astype(BF16), (((1,), (1,)), ((), ())),
                             preferred_element_type=F32) * mask
    o = _dot(scores.astype(BF16), vb) + _dot((qr * q_dec).astype(BF16), s.astype(BF16))
    s_new = sdec * s + lax.dot_general((kr * k_dec).astype(BF16), vb, (((0,), (0,)), ((), ())),
                                       preferred_element_type=F32)
    mu = jnp.mean(o, axis=-1, keepdims=True)
    oc = o - mu
    var = jnp.mean(oc * oc, axis=-1, keepdims=True)
    y = oc * lax.rsqrt(var + GN_EPS) * gn
    return y * jax.nn.silu(g), s_new


def _ret_seq_kernel(dec_ref, q_ref, k_ref, v_ref, g_ref, cos_ref, sin_ref, s0_ref, gn_ref,
                    o_ref, sf_ref, s_scr, *, chunk, nchunks):
    h = pl.program_id(1)
    lg = dec_ref[0, h]
    sdec = dec_ref[1, h]
    s_scr[...] = s0_ref[...]
    gn = gn_ref[...]

    def body(c, carry):
        rows = pl.ds(pl.multiple_of(c * chunk, chunk), chunk)
        y, s_new = _retention_chunk(lg, sdec, s_scr[...], q_ref[rows, :], k_ref[rows, :],
                                    v_ref[rows, :], g_ref[rows, :], cos_ref[rows, :],
                                    sin_ref[rows, :], gn)
        o_ref[rows, :] = y.astype(o_ref.dtype)
        s_scr[...] = s_new
        return carry

    lax.fori_loop(0, nchunks, body, 0, unroll=min(8, nchunks))
    sf_ref[...] = s_scr[...]


def _ret_decay_consts(chunk):
    lg = np.log(1.0 - 2.0 ** (-5.0 - np.arange(RET_HEADS, dtype=np.float64)))
    return jnp.asarray(np.stack([lg, np.exp(lg * chunk)]), dtype=F32)


def _rope_tables(pos):
    half = RET_HEAD_DIM // 2
    inv = ROPE_BASE ** (-np.arange(half, dtype=np.float64) / half)
    ang = np.asarray(pos, dtype=np.float64)[:, None] * inv[None, :]
    return jnp.asarray(np.cos(ang), dtype=F32), jnp.asarray(np.sin(ang), dtype=F32)


def _retention_seq(proj, row_block0, nbatch, seq, chunk, pos0, s0, gn_g):
    cos, sin = _rope_tables(pos0 + np.arange(seq))
    hd = RET_HEAD_DIM

    def col(off):
        return pl.BlockSpec((seq, hd), lambda b, h: (row_block0 + b, off + h))

    tbl = pl.BlockSpec((seq, hd // 2), lambda b, h: (0, 0))
    return pl.pallas_call(
        functools.partial(_ret_seq_kernel, chunk=chunk, nchunks=seq // chunk),
        grid=(nbatch, RET_HEADS),
        in_specs=[pl.BlockSpec(memory_space=pltpu.SMEM),
                  col(0), col(RET_HEADS), col(2 * RET_HEADS), col(3 * RET_HEADS), tbl, tbl,
                  pl.BlockSpec((None, None, hd, hd), lambda b, h: (0, h, 0, 0)),
                  pl.BlockSpec((1, hd), lambda b, h: (0, h))],
        out_specs=[pl.BlockSpec((seq, hd), lambda b, h: (b, h)),
                   pl.BlockSpec((None, None, hd, hd), lambda b, h: (b, h, 0, 0))],
        out_shape=[jax.ShapeDtypeStruct((nbatch * seq, RET_WIDTH), BF16),
                   jax.ShapeDtypeStruct((nbatch, RET_HEADS, hd, hd), F32)],
        scratch_shapes=[pltpu.VMEM((hd, hd), F32)],
        compiler_params=_params("parallel", "arbitrary"),
        name="retention_seq",
    )(_ret_decay_consts(chunk), proj, proj, proj, proj, cos, sin, s0, gn_g.reshape(1, RET_WIDTH))


def _ret_step_kernel(dec_ref, q_ref, k_ref, v_ref, g_ref, cos_ref, sin_ref, s_ref, gn_ref,
                     o_ref, sn_ref, *, nb, seq):
    hd = RET_HEAD_DIM
    cos, sin = cos_ref[...], sin_ref[...]
    for b in range(nb):
        rows = slice(b * seq, (b + 1) * seq)
        for h in range(RET_HEADS):
            cols = slice(h * hd, (h + 1) * hd)
            y, s_new = _retention_chunk(dec_ref[0, h], dec_ref[1, h], s_ref[b, h],
                                        q_ref[rows, cols], k_ref[rows, cols], v_ref[rows, cols],
                                        g_ref[rows, cols], cos, sin, gn_ref[:, cols])
            o_ref[rows, cols] = y.astype(o_ref.dtype)
            sn_ref[b, h] = s_new


def _retention_step(proj, row0, nbatch, seq, pos0, state, gn_g, nb):
    cos, sin = _rope_tables(pos0 + np.arange(seq))
    hd = RET_HEAD_DIM
    rb0 = row0 // (nb * seq)

    def col(off):
        return pl.BlockSpec((nb * seq, RET_WIDTH), lambda i: (rb0 + i, off))

    tbl = pl.BlockSpec((seq, hd // 2), lambda i: (0, 0))
    st = pl.BlockSpec((nb, RET_HEADS, hd, hd), lambda i: (i, 0, 0, 0))
    return pl.pallas_call(
        functools.partial(_ret_step_kernel, nb=nb, seq=seq),
        grid=(nbatch // nb,),
        in_specs=[pl.BlockSpec(memory_space=pltpu.SMEM),
                  col(0), col(1), col(2), col(3), tbl, tbl, st,
                  pl.BlockSpec((1, RET_WIDTH), lambda i: (0, 0))],
        out_specs=[pl.BlockSpec((nb * seq, RET_WIDTH), lambda i: (i, 0)), st],
        out_shape=[jax.ShapeDtypeStruct((nbatch * seq, RET_WIDTH), BF16),
                   jax.ShapeDtypeStruct(state.shape, F32)],
        compiler_params=_params("parallel"),
        name="retention_step",
    )(_ret_decay_consts(seq), proj, proj, proj, proj, cos, sin, state,
      gn_g.reshape(1, RET_WIDTH))


def _s5_disc_kernel(lr_ref, li_ref, ldt_ref, br_ref, bi_ref, ar_ref, ai_ref, bbr_ref, bbi_ref):
    lr, li = lr_ref[...], li_ref[...]
    dt = jnp.exp(ldt_ref[...])
    mag = jnp.exp(lr * dt)
    ar = mag * jnp.cos(li * dt)
    ai = mag * jnp.sin(li * dt)
    nr, ni = ar - 1.0, ai
    den = lr * lr + li * li
    fr = (nr * lr + ni * li) / den
    fi = (ni * lr - nr * li) / den
    br, bi = br_ref[...], bi_ref[...]
    ar_ref[...] = ar
    ai_ref[...] = ai
    bbr_ref[...] = fr * br - fi * bi
    bbi_ref[...] = fr * bi + fi * br


def _s5_discretize(lam_re, lam_im, log_dt, b_re, b_im):
    g, p, c = b_re.shape
    shp3 = jax.ShapeDtypeStruct((g, 1, p), F32)
    shpb = jax.ShapeDtypeStruct((g, c, p), F32)
    return pl.pallas_call(
        _s5_disc_kernel,
        out_shape=[shp3, shp3, shpb, shpb],
        name="s5_discretize",
    )(lam_re.reshape(g, 1, p), lam_im.reshape(g, 1, p), log_dt.reshape(g, 1, 1),
      jnp.swapaxes(b_re, 1, 2), jnp.swapaxes(b_im, 1, 2))


S5_NPAIR = S5_SLABS // 4


def _lanes(s):
    return slice(s * LANES, (s + 1) * LANES)


def _s5_place(s, nb, pitch):
    comp, ls = s // (2 * S5_NPAIR), s % (2 * S5_NPAIR)
    hf, p = ls // S5_NPAIR, ls % S5_NPAIR
    return comp * S5_NPAIR + p, hf * (nb * pitch)


def _s5_paired(ref, p, nb, order=None):
    def slab(hf):
        cols = _lanes(p + hf * S5_NPAIR)
        if order is None or ref.shape[0] == 1:
            return jnp.broadcast_to(ref[:, cols], (nb, LANES))
        return jnp.concatenate([ref[o:o + 1, cols] for o in order], axis=0)
    return jnp.concatenate([slab(0), slab(1)], axis=0)


def _s5_block_weights(bbr_ref, bbi_ref, cr_ref, ci_ref, bw_scr, cw_scr):
    bw_scr[...] = jnp.zeros_like(bw_scr)
    cw_scr[...] = jnp.zeros_like(cw_scr)
    for g in range(S5_GB):
        ch = slice(g * S5_GROUP, (g + 1) * S5_GROUP)
        for comp, (b_ref, c_ref, sign) in enumerate(((bbr_ref, cr_ref, 1.0), (bbi_ref, ci_ref, -1.0))):
            st = slice(comp * S5_ST + g * S5_STATE, comp * S5_ST + (g + 1) * S5_STATE)
            bw_scr[ch, st] = b_ref[g]
            cw_scr[st, ch] = sign * c_ref[g]


def _s5_step(buf, t, xs, ar, ai, nb, pitch):
    rows = pl.ds(t, 2 * nb, stride=pitch)
    new = [None] * (2 * S5_NPAIR)
    for p in range(S5_NPAIR):
        xr, xi = xs[p], xs[S5_NPAIR + p]
        nr = ar[p] * xr - ai[p] * xi + buf[p, rows, :]
        ni = ar[p] * xi + ai[p] * xr + buf[S5_NPAIR + p, rows, :]
        buf[p, rows, :] = nr
        buf[S5_NPAIR + p, rows, :] = ni
        new[p], new[S5_NPAIR + p] = nr, ni
    return tuple(new)


def _s5_scan_io(x0r_ref, x0i_ref, ar_ref, ai_ref, nb, order=None):
    ar = [_s5_paired(ar_ref, p, nb) for p in range(S5_NPAIR)]
    ai = [_s5_paired(ai_ref, p, nb) for p in range(S5_NPAIR)]
    xs0 = tuple([_s5_paired(x0r_ref, p, nb, order) for p in range(S5_NPAIR)]
                + [_s5_paired(x0i_ref, p, nb, order) for p in range(S5_NPAIR)])
    return ar, ai, xs0


def _s5_store_final(xs, xfr_ref, xfi_ref, nb, order=None):
    for p in range(S5_NPAIR):
        for hf in range(2):
            cols = _lanes(p + hf * S5_NPAIR)
            for x, ref in ((xs[p], xfr_ref), (xs[S5_NPAIR + p], xfi_ref)):
                if order is None:
                    ref[:, cols] = x[hf * nb:(hf + 1) * nb]
                else:
                    for i, o in enumerate(order):
                        ref[o:o + 1, cols] = x[hf * nb + i:hf * nb + i + 1]


def _s5_kernel(u_ref, x0r_ref, x0i_ref, ar_ref, ai_ref, bbr_ref, bbi_ref, cr_ref, ci_ref, d_ref,
               z_ref, xfr_ref, xfi_ref, buf, bw_scr, cw_scr, *, nb, seq):
    _s5_block_weights(bbr_ref, bbi_ref, cr_ref, ci_ref, bw_scr, cw_scr)
    ar, ai, xs = _s5_scan_io(x0r_ref, x0i_ref, ar_ref, ai_ref, nb)
    u = u_ref[...]
    bu = _dot(u.astype(BF16), bw_scr[...].astype(BF16))
    for s in range(S5_SLABS):
        q, off = _s5_place(s, nb, seq)
        buf[q, off:off + nb * seq, :] = bu[:, _lanes(s)]
    xs = lax.fori_loop(0, seq, lambda t, xs: _s5_step(buf, t, xs, ar, ai, nb, seq), xs,
                       unroll=8 if nb <= 8 else 1)
    parts = []
    for s in range(S5_SLABS):
        q, off = _s5_place(s, nb, seq)
        parts.append(buf[q, off:off + nb * seq, :])
    y = _dot(jnp.concatenate(parts, axis=1).astype(BF16), cw_scr[...].astype(BF16)) + d_ref[...] * u
    z_ref[...] = jax.nn.gelu(y).astype(z_ref.dtype)
    _s5_store_final(xs, xfr_ref, xfi_ref, nb)


def _s5_chunked_kernel(u_ref, x0r_ref, x0i_ref, ar_ref, ai_ref, bbr_ref, bbi_ref, cr_ref, ci_ref,
                       d_ref, z_ref, xfr_ref, xfi_ref, buf, bw_scr, cw_scr, *, nb, seq, tc, pitch):
    lead_odd = pitch % 8
    assert nb % 2 == 0 and nb >= 4 and lead_odd == 4 and (nb * pitch) % 8 == 0
    inner = list(range(1, nb - 1))
    even, odd = [0, nb - 1] + inner[nb // 2:], inner[:nb // 2]
    order = [(odd if i % 2 else even)[i // 2] for i in range(nb)]

    _s5_block_weights(bbr_ref, bbi_ref, cr_ref, ci_ref, bw_scr, cw_scr)
    bw = bw_scr[...].astype(BF16)
    cw = cw_scr[...].astype(BF16)
    ar, ai, xs = _s5_scan_io(x0r_ref, x0i_ref, ar_ref, ai_ref, nb, order)
    dvec = d_ref[...]

    def chunk(c, xs):
        t0 = c * tc
        rows = [pl.ds(pl.multiple_of(order[i] * seq + t0, tc), tc) for i in range(nb)]
        lead = [lead_odd if i % 2 else 0 for i in range(nb)]

        def u_window(i):
            if not lead[i]:
                return u_ref[rows[i], :]
            wide = u_ref[pl.ds(pl.multiple_of(order[i] * seq + t0 - 8, 8), tc + 16), :]
            return wide[8 - lead[i]:8 + tc + lead[i]]

        for i in range(nb):
            bu = _dot(u_window(i).astype(BF16), bw)
            for s in range(S5_SLABS):
                q, off = _s5_place(s, nb, pitch)
                r0 = off + i * pitch - lead[i]
                buf[q, r0:r0 + tc + 2 * lead[i], :] = bu[:, _lanes(s)]
        xs = lax.fori_loop(0, tc, lambda t, xs: _s5_step(buf, t, xs, ar, ai, nb, pitch), xs,
                           unroll=8)
        for i in range(nb):
            parts = []
            for s in range(S5_SLABS):
                q, off = _s5_place(s, nb, pitch)
                r0 = off + i * pitch - lead[i]
                parts.append(buf[q, r0:r0 + tc + 2 * lead[i], :])
            y = _dot(jnp.concatenate(parts, axis=1).astype(BF16), cw)[lead[i]:lead[i] + tc]
            y = y + dvec * u_ref[rows[i], :]
            z_ref[rows[i], :] = jax.nn.gelu(y).astype(z_ref.dtype)
        return xs

    xs = lax.fori_loop(0, seq // tc, chunk, xs)
    _s5_store_final(xs, xfr_ref, xfi_ref, nb, order)


def _s5(proj, row_block0, nb, seq, tc, x0r, x0i, ar, ai, bbr, bbi, c_re, c_im, d):
    rows = nb * seq
    u_col0 = 4 * RET_WIDTH // S5_CH
    nj = S5_GROUPS // S5_GB
    st = pl.BlockSpec((nb, S5_ST), lambda j: (0, j))
    coef = pl.BlockSpec((1, S5_ST), lambda j: (0, j))
    bspec = pl.BlockSpec((S5_GB, S5_GROUP, S5_STATE), lambda j: (j, 0, 0))
    cspec = pl.BlockSpec((S5_GB, S5_STATE, S5_GROUP), lambda j: (j, 0, 0))
    weights = [pltpu.VMEM((S5_CH, 2 * S5_ST), F32), pltpu.VMEM((2 * S5_ST, S5_CH), F32)]
    if tc == seq:
        body = functools.partial(_s5_kernel, nb=nb, seq=seq)
        pitch = seq
    else:
        pitch = tc + 4
        body = functools.partial(_s5_chunked_kernel, nb=nb, seq=seq, tc=tc, pitch=pitch)
    scratch = [pltpu.VMEM((S5_SLABS // 2, 2 * nb * pitch, LANES), F32)] + weights
    return pl.pallas_call(
        body,
        grid=(nj,),
        in_specs=[pl.BlockSpec((rows, S5_CH), lambda j: (row_block0, u_col0 + j)),
                  st, st, coef, coef, bspec, bspec, cspec, cspec,
                  pl.BlockSpec((1, S5_CH), lambda j: (0, j))],
        out_specs=[pl.BlockSpec((rows, S5_CH), lambda j: (0, j)), st, st],
        out_shape=[jax.ShapeDtypeStruct((rows, S5_WIDTH), BF16),
                   jax.ShapeDtypeStruct((nb, S5_NSTATE), F32),
                   jax.ShapeDtypeStruct((nb, S5_NSTATE), F32)],
        scratch_shapes=scratch,
        compiler_params=_params("parallel"),
        name="s5_scan",
    )(proj, x0r, x0i, ar, ai, bbr, bbi, c_re, c_im, d.reshape(1, S5_WIDTH))


def _s5_post_kernel(zp_ref, zs_ref, w_ref, b_ref, g_ref, o_ref, z_scr, gate_scr, ssq_scr,
                    *, n_first, tn):
    i, j = pl.program_id(0), pl.program_id(1)
    nj = gate_scr.shape[0]

    @pl.when(j == 0)
    def _():
        z = jnp.where(i < n_first, zp_ref[...], zs_ref[...])
        for jj in range(nj):
            z_scr[jj] = z[:, jj * tn:(jj + 1) * tn]
        ssq_scr[...] = jnp.zeros_like(ssq_scr)

    z = jnp.concatenate([z_scr[jj] for jj in range(nj)], axis=1)
    t = _dot(z, w_ref[...].astype(BF16)) + b_ref[...]
    s = z_scr[j].astype(F32) * jax.nn.sigmoid(t)
    gate_scr[j] = s
    ssq_scr[...] += jnp.sum(s * s, axis=-1, keepdims=True)

    @pl.when(j == nj - 1)
    def _():
        r = lax.rsqrt(ssq_scr[...] / (tn * nj) + EPS)
        for jj in range(nj):
            cols = slice(jj * tn, (jj + 1) * tn)
            o_ref[:, cols] = (gate_scr[jj] * r * g_ref[:, cols]).astype(o_ref.dtype)


def _s5_post(zp, zs, w_glu, b_glu, norm_g, tm, tn):
    d = zp.shape[1]
    n_first = zp.shape[0] // tm
    nblk = n_first + zs.shape[0] // tm
    nj = d // tn
    return pl.pallas_call(
        functools.partial(_s5_post_kernel, n_first=n_first, tn=tn),
        grid=(nblk, nj),
        in_specs=[pl.BlockSpec((tm, d), lambda i, j: (jnp.minimum(i, n_first - 1), 0)),
                  pl.BlockSpec((tm, d), lambda i, j: (jnp.maximum(i - n_first, 0), 0)),
                  pl.BlockSpec((d, tn), lambda i, j: (0, j)),
                  pl.BlockSpec((1, tn), lambda i, j: (0, j)),
                  pl.BlockSpec((1, d), lambda i, j: (0, 0))],
        out_specs=pl.BlockSpec((tm, d), lambda i, j: (i, 0)),
        out_shape=jax.ShapeDtypeStruct((nblk * tm, d), BF16),
        scratch_shapes=[pltpu.VMEM((nj, tm, tn), BF16), pltpu.VMEM((nj, tm, tn), F32),
                        pltpu.VMEM((tm, 1), F32)],
        compiler_params=_params("parallel", "arbitrary"),
        name="s5_post",
    )(zp, zs, w_glu, b_glu.reshape(1, d), norm_g.reshape(1, d))


def kernel(x_prompt, x_sample, state_ret, state_s5_re, state_s5_im, meta_tokens, norm1_g, w_in, ret_gn_g, s5_lam_re, s5_lam_im, s5_log_dt, s5_b_re, s5_b_im, s5_c_re, s5_c_im, s5_d, w_glu, b_glu, s5_norm_g, w_out, norm2_g, w_gate, w_up, w_down, final_norm_g):
    assert norm1_g.shape[0] == 1, "single-layer model"
    batch, seq, d = x_prompt.shape
    dbatch, dseq, _ = x_sample.shape
    rows_p, rows_s = batch * seq, dbatch * dseq
    tm = 1024
    assert rows_p % tm == 0 and rows_s % tm == 0 and seq % RET_CHUNK == 0
    xp = x_prompt.reshape(rows_p, d)
    xs = x_sample.reshape(rows_s, d)
    hd = RET_HEAD_DIM

    ar, ai, bbr, bbi = _s5_discretize(s5_lam_re[0], s5_lam_im[0], s5_log_dt[0], s5_b_re[0], s5_b_im[0])
    ar = ar.reshape(1, S5_NSTATE)
    ai = ai.reshape(1, S5_NSTATE)
    s5_w = (ar, ai, bbr, bbi, jnp.swapaxes(s5_c_re[0], 1, 2), jnp.swapaxes(s5_c_im[0], 1, 2), s5_d[0])

    h_meta = _rmsnorm([meta_tokens], norm1_g[0], N_META, BF16)
    h = _rmsnorm([xp, xs], norm1_g[0], 512, BF16)
    proj, proj_meta = _inproj(h, h_meta, w_in[0], 1536, 512)

    zero_ret = jnp.zeros((1, RET_HEADS, hd, hd), F32)
    _, s_meta = _retention_seq(proj_meta, 0, 1, N_META, N_META, 0, zero_ret, ret_gn_g[0])
    zero_s5 = jnp.zeros((1, S5_NSTATE), F32)
    _, mr, mi = _s5(proj_meta, 0, 1, N_META, N_META, zero_s5, zero_s5, *s5_w)

    ret_p, sret_p = _retention_seq(proj, 0, batch, seq, RET_CHUNK, N_META, s_meta, ret_gn_g[0])
    ret_s, sret_s = _retention_step(proj, rows_p, dbatch, dseq, PAST_LEN, state_ret[0],
                                    ret_gn_g[0], 4)

    z_p, s5r_p, s5i_p = _s5(proj, 0, batch, seq, 256,
                            jnp.broadcast_to(mr, (batch, S5_NSTATE)),
                            jnp.broadcast_to(mi, (batch, S5_NSTATE)), *s5_w)
    z_s, s5r_s, s5i_s = _s5(proj, rows_p // rows_s, dbatch, dseq, dseq,
                            state_s5_re[0].reshape(dbatch, S5_NSTATE),
                            state_s5_im[0].reshape(dbatch, S5_NSTATE), *s5_w)
    s5_out = _s5_post(z_p, z_s, w_glu[0], b_glu[0], s5_norm_g[0], tm, 512)

    x1, x1g, x1_ssq = _outproj(ret_p, ret_s, s5_out, w_out[0], xp, xs, norm2_g[0], tm, 512)
    a = _ffn_up(x1g, x1_ssq, w_gate[0], w_up[0], tm, 256)
    x2 = _ffn_down(a, w_down[0], x1, (w_down.shape[1] // 512 + 1) * 256, tm, 256)
    y_p, y_s = _final_norm(x2, final_norm_g, rows_p, 512)

    st = (1, -1, S5_GROUPS, S5_STATE)
    return (y_p.reshape(batch, seq, d), y_s.reshape(dbatch, dseq, d),
            sret_p[None], s5r_p.reshape(st), s5i_p.reshape(st),
            sret_s[None], s5r_s.reshape(st), s5i_s.reshape(st))
```

```python
import functools

import numpy as np
import jax
import jax.numpy as jnp
from jax import lax
from jax.experimental import pallas as pl
from jax.experimental.pallas import tpu as pltpu

D_MODEL = 4096
N_META = 16
PAST_LEN = 16384
RET_WIDTH = D_MODEL // 2
S5_WIDTH = D_MODEL - RET_WIDTH
RET_HEADS = 8
RET_HEAD_DIM = RET_WIDTH // RET_HEADS
RET_CHUNK = 128
ROPE_BASE = 10000.0
S5_GROUP = 16
S5_GROUPS = S5_WIDTH // S5_GROUP
S5_STATE = 64
S5_NSTATE = S5_GROUPS * S5_STATE
IN_COLS = 4 * RET_WIDTH + S5_WIDTH
EPS = 1e-6
GN_EPS = 1e-5

LANES = 128
S5_GB = 16
S5_CH = S5_GB * S5_GROUP
S5_ST = S5_GB * S5_STATE
S5_SLABS = 2 * S5_ST // LANES
VMEM_LIMIT = 56 * 1024 * 1024
VMEM_LIMIT_FFN_DOWN = 58 * 1024 * 1024

BF16 = jnp.bfloat16
F32 = jnp.float32


def _params(*sem, vmem=VMEM_LIMIT):
    return pltpu.CompilerParams(dimension_semantics=sem, vmem_limit_bytes=vmem)


def _dot(a, b):
    return jnp.dot(a, b, preferred_element_type=F32)


def _rmsnorm_kernel(*refs, n_src, n_first):
    x_refs, g_ref, o_ref = refs[:n_src], refs[n_src], refs[n_src + 1]
    x = x_refs[0][...]
    if n_src == 2:
        x = jnp.where(pl.program_id(0) < n_first, x, x_refs[1][...])
    r = lax.rsqrt(jnp.mean(x * x, axis=-1, keepdims=True) + EPS)
    o_ref[...] = (x * r * g_ref[...]).astype(o_ref.dtype)


def _rmsnorm(srcs, g, tm, out_dtype):
    d = srcs[0].shape[1]
    nblk = [s.shape[0] // tm for s in srcs]
    n_first = nblk[0]
    specs = [pl.BlockSpec((tm, d), lambda i: (jnp.minimum(i, n_first - 1), 0))]
    if len(srcs) == 2:
        specs.append(pl.BlockSpec((tm, d), lambda i: (jnp.maximum(i - n_first, 0), 0)))
    rows = sum(s.shape[0] for s in srcs)
    return pl.pallas_call(
        functools.partial(_rmsnorm_kernel, n_src=len(srcs), n_first=n_first),
        grid=(sum(nblk),),
        in_specs=specs + [pl.BlockSpec((1, d), lambda i: (0, 0))],
        out_specs=pl.BlockSpec((tm, d), lambda i: (i, 0)),
        out_shape=jax.ShapeDtypeStruct((rows, d), out_dtype),
        compiler_params=_params("parallel"),
        name="rmsnorm",
    )(*srcs, g.reshape(1, d))


def _final_norm_kernel(x_ref, g_ref, yp_ref, ys_ref, *, n_first):
    x = x_ref[...]
    r = lax.rsqrt(jnp.mean(x * x, axis=-1, keepdims=True) + EPS)
    y = x * r * g_ref[...]
    i = pl.program_id(0)

    @pl.when(i < n_first)
    def _():
        yp_ref[...] = y

    @pl.when(i >= n_first)
    def _():
        ys_ref[...] = y


def _final_norm(x, g, rows_p, tm):
    rows, d = x.shape
    n_first = rows_p // tm
    return pl.pallas_call(
        functools.partial(_final_norm_kernel, n_first=n_first),
        grid=(rows // tm,),
        in_specs=[pl.BlockSpec((tm, d), lambda i: (i, 0)),
                  pl.BlockSpec((1, d), lambda i: (0, 0))],
        out_specs=[pl.BlockSpec((tm, d), lambda i: (jnp.minimum(i, n_first - 1), 0)),
                   pl.BlockSpec((tm, d), lambda i: (jnp.maximum(i - n_first, 0), 0))],
        out_shape=[jax.ShapeDtypeStruct((rows_p, d), F32),
                   jax.ShapeDtypeStruct((rows - rows_p, d), F32)],
        compiler_params=_params("arbitrary"),
        name="final_norm",
    )(x, g.reshape(1, d))


def _inproj_kernel(h_ref, hx_ref, w_ref, o_ref, ox_ref):
    w = w_ref[...].astype(BF16)
    o_ref[...] = _dot(h_ref[...], w)

    @pl.when(pl.program_id(0) == 0)
    def _():
        ox_ref[...] = _dot(hx_ref[...], w)


def _inproj(h, h_extra, w, tm, tn):
    m, k = h.shape
    mx = h_extra.shape[0]
    n = w.shape[1]
    nj = n // tn
    return pl.pallas_call(
        _inproj_kernel,
        grid=(m // tm, nj),
        in_specs=[pl.BlockSpec((tm, k), lambda i, j: (i, 0)),
                  pl.BlockSpec((mx, k), lambda i, j: (0, 0)),
                  pl.BlockSpec((k, tn), lambda i, j: (0, j))],
        out_specs=[pl.BlockSpec((tm, tn), lambda i, j: (i, j)),
                   pl.BlockSpec((mx, tn), lambda i, j: (0, jnp.where(i == 0, j, nj - 1)))],
        out_shape=[jax.ShapeDtypeStruct((m, n), F32), jax.ShapeDtypeStruct((mx, n), F32)],
        compiler_params=_params("arbitrary", "arbitrary"),
        name="inproj",
    )(h, h_extra, w)


def _outproj_kernel(retp_hbm, rets_hbm, s5_hbm, w_ref, xp_ref, xs_ref, g_ref,
                    o_ref, xg_ref, ssq_ref, mix_buf, mix_sem, *, n_first, tm):
    i, j = pl.program_id(0), pl.program_id(1)
    kh = mix_buf.shape[2] // 2
    slot = i % 2

    def ret_copy(block, s, from_prompt):
        src = retp_hbm if from_prompt else rets_hbm
        row0 = (block if from_prompt else block - n_first) * tm
        return pltpu.make_async_copy(src.at[pl.ds(pl.multiple_of(row0, tm), tm)],
                                     mix_buf.at[s, :, pl.ds(0, kh)], mix_sem.at[s, 0])

    def s5_copy(block, s):
        return pltpu.make_async_copy(s5_hbm.at[pl.ds(pl.multiple_of(block * tm, tm), tm)],
                                     mix_buf.at[s, :, pl.ds(kh, kh)], mix_sem.at[s, 1])

    def each_copy(block, s, act):
        @pl.when(block < n_first)
        def _():
            act(ret_copy(block, s, True))

        @pl.when(block >= n_first)
        def _():
            act(ret_copy(block, s, False))

        act(s5_copy(block, s))

    @pl.when((i == 0) & (j == 0))
    def _():
        each_copy(0, 0, lambda c: c.start())

    @pl.when(j == 0)
    def _():
        each_copy(i, slot, lambda c: c.wait())

        @pl.when(i + 1 < pl.num_programs(0))
        def _():
            each_copy(i + 1, 1 - slot, lambda c: c.start())

    x = jnp.where(i < n_first, xp_ref[...], xs_ref[...])
    x1 = x + _dot(mix_buf[slot], w_ref[...].astype(BF16))
    o_ref[...] = x1
    xg_ref[...] = (x1 * g_ref[...]).astype(xg_ref.dtype)
    sq = x1 * x1
    part = sq[:, :LANES]
    for c in range(1, sq.shape[1] // LANES):
        part += sq[:, c * LANES:(c + 1) * LANES]

    @pl.when(j == 0)
    def _():
        ssq_ref[...] = part

    @pl.when(j != 0)
    def _():
        ssq_ref[...] += part


def _outproj(ret_p, ret_s, s5, w_out, xp, xs, g, tm, tn):
    m, kh = s5.shape
    k, n = w_out.shape
    assert ret_p.shape[1] == kh and k == 2 * kh
    n_first = xp.shape[0] // tm
    nj = n // tn
    hbm = pl.BlockSpec(memory_space=pl.ANY)
    return pl.pallas_call(
        functools.partial(_outproj_kernel, n_first=n_first, tm=tm),
        grid=(m // tm, nj),
        in_specs=[hbm, hbm, hbm,
                  pl.BlockSpec((k, tn), lambda i, j: (0, j)),
                  pl.BlockSpec((tm, tn), lambda i, j: (jnp.minimum(i, n_first - 1),
                                                       jnp.where(i < n_first, j, nj - 1))),
                  pl.BlockSpec((tm, tn), lambda i, j: (jnp.maximum(i - n_first, 0),
                                                       jnp.where(i < n_first, 0, j)),
                               pipeline_mode=pl.Buffered(1)),
                  pl.BlockSpec((1, tn), lambda i, j: (0, j))],
        out_specs=[pl.BlockSpec((tm, tn), lambda i, j: (i, j)),
                   pl.BlockSpec((tm, tn), lambda i, j: (i, j)),
                   pl.BlockSpec((tm, LANES), lambda i, j: (i, 0))],
        out_shape=[jax.ShapeDtypeStruct((m, n), F32),
                   jax.ShapeDtypeStruct((m, n), BF16),
                   jax.ShapeDtypeStruct((m, LANES), F32)],
        scratch_shapes=[pltpu.VMEM((2, tm, k), s5.dtype), pltpu.SemaphoreType.DMA((2, 2))],
        compiler_params=_params("arbitrary", "arbitrary"),
        name="outproj",
    )(ret_p, ret_s, s5, w_out, xp, xs, g.reshape(1, n))


def _ffn_up_kernel(xg_ref, ssq_ref, wg_ref, wu_ref, o_ref, r_scr):
    tf = wg_ref.shape[1]

    @pl.when(pl.program_id(1) == 0)
    def _():
        r_scr[...] = lax.rsqrt(jnp.sum(ssq_ref[...], axis=-1, keepdims=True) / xg_ref.shape[1] + EPS)

    w = jnp.concatenate([wg_ref[...].astype(BF16), wu_ref[...].astype(BF16)], axis=1)
    gu = _dot(xg_ref[...], w) * r_scr[...]
    o_ref[...] = (jax.nn.silu(gu[:, :tf]) * gu[:, tf:]).astype(o_ref.dtype)


def _ffn_up(xg, ssq, w_gate, w_up, tm, tf):
    m, k = xg.shape
    f = w_gate.shape[1]
    return pl.pallas_call(
        _ffn_up_kernel,
        grid=(m // tm, f // tf),
        in_specs=[pl.BlockSpec((tm, k), lambda i, j: (i, 0)),
                  pl.BlockSpec((tm, LANES), lambda i, j: (i, 0)),
                  pl.BlockSpec((k, tf), lambda i, j: (0, j)),
                  pl.BlockSpec((k, tf), lambda i, j: (0, j))],
        out_specs=pl.BlockSpec((tm, tf), lambda i, j: (i, j)),
        out_shape=jax.ShapeDtypeStruct((m, f), BF16),
        scratch_shapes=[pltpu.VMEM((tm, 1), F32)],
        compiler_params=_params("parallel", "arbitrary"),
        name="ffn_up",
    )(xg, ssq, w_gate, w_up)


def _ffn_down_kernel(a_hbm, w_ref, x_ref, g_ref, yp_ref, ys_ref, a_lo, a_hi, a_sem, acc, ssq, r_scr,
                     *, tm, n_first):
    i, h, j = pl.program_id(0), pl.program_id(1), pl.program_id(2)
    k_lo, k_hi = a_lo.shape[1], a_hi.shape[1]

    def a_copy(block, half):
        rows = pl.ds(pl.multiple_of(block * tm, tm), tm)
        cols, buf = (pl.ds(k_lo, k_hi), a_hi) if half else (pl.ds(0, k_lo), a_lo)
        return pltpu.make_async_copy(a_hbm.at[rows, cols], buf, a_sem.at[half])

    @pl.when((i == 0) & (h == 0) & (j == 0))
    def _():
        a_copy(0, 0).start()

    @pl.when((h == 0) & (j == 0))
    def _():
        a_copy(i, 0).wait()
        a_copy(i, 1).start()

    @pl.when((h == 1) & (j == 0))
    def _():
        a_copy(i, 1).wait()

        @pl.when(i + 1 < pl.num_programs(0))
        def _():
            a_copy(i + 1, 0).start()

    @pl.when(h == 0)
    def _():
        acc[j] = x_ref[...] + _dot(a_lo[...], w_ref[...].astype(BF16))

    @pl.when(h == 1)
    def _():
        x2 = acc[j] + _dot(a_hi[...], w_ref[k_lo - k_hi:, :].astype(BF16))
        acc[j] = x2
        sq = x2 * x2
        part = sq[:, :LANES]
        for c in range(1, sq.shape[1] // LANES):
            part += sq[:, c * LANES:(c + 1) * LANES]

        @pl.when(j == 0)
        def _():
            ssq[...] = part

        @pl.when(j != 0)
        def _():
            ssq[...] += part

    @pl.when(h == 2)
    def _():
        @pl.when(j == 0)
        def _():
            n = acc.shape[0] * acc.shape[2]
            r_scr[...] = lax.rsqrt(jnp.sum(ssq[...], axis=-1, keepdims=True) / n + EPS)

        y = acc[j] * r_scr[...] * g_ref[...]

        @pl.when(i < n_first)
        def _():
            yp_ref[...] = y

        @pl.when(i >= n_first)
        def _():
            ys_ref[...] = y


def _ffn_down_norm(a, w_down, x, g, rows_p, k_lo, tm, tn):
    m, f = a.shape
    n = w_down.shape[1]
    nj = n // tn
    n_first = rows_p // tm
    assert 2 * k_lo >= f and rows_p % tm == 0
    last = nj - 1
    return pl.pallas_call(
        functools.partial(_ffn_down_kernel, tm=tm, n_first=n_first),
        grid=(m // tm, 3, nj),
        in_specs=[pl.BlockSpec(memory_space=pl.ANY),
                  pl.BlockSpec((pl.Element(k_lo), pl.Element(tn)),
                               lambda i, h, j: (jnp.minimum(h, 1) * (f - k_lo),
                                                jnp.where(h == 2, last, j) * tn)),
                  pl.BlockSpec((tm, tn), lambda i, h, j: (i, jnp.where(h == 0, j, last))),
                  pl.BlockSpec((1, tn), lambda i, h, j: (0, jnp.where(h == 2, j, 0)))],
        out_specs=[pl.BlockSpec((tm, tn), lambda i, h, j: (
                       jnp.minimum(i, n_first - 1),
                       jnp.where(i < n_first, jnp.where(h == 2, j, 0), last))),
                   pl.BlockSpec((tm, tn), lambda i, h, j: (
                       jnp.maximum(i - n_first, 0),
                       jnp.where((i >= n_first) & (h == 2), j, 0)))],
        out_shape=[jax.ShapeDtypeStruct((rows_p, n), F32),
                   jax.ShapeDtypeStruct((m - rows_p, n), F32)],
        scratch_shapes=[pltpu.VMEM((tm, k_lo), a.dtype), pltpu.VMEM((tm, f - k_lo), a.dtype),
                        pltpu.SemaphoreType.DMA((2,)), pltpu.VMEM((nj, tm, tn), F32),
                        pltpu.VMEM((tm, LANES), F32), pltpu.VMEM((tm, 1), F32)],
        compiler_params=_params("arbitrary", "arbitrary", "arbitrary", vmem=VMEM_LIMIT_FFN_DOWN),
        name="ffn_down",
    )(a, w_down, x, g.reshape(1, n))


def _rotary(x, cos, sin):
    half = RET_HEAD_DIM // 2
    x1, x2 = x[:, :half], x[:, half:]
    return jnp.concatenate([x1 * cos - x2 * sin, x1 * sin + x2 * cos], axis=-1)


def _retention_chunk(lg, sdec, s, q, k, v, g, cos, sin, gn):
    n = q.shape[0]
    ri = lax.broadcasted_iota(jnp.int32, (n, n), 0)
    ci = lax.broadcasted_iota(jnp.int32, (n, n), 1)
    diff = (ri - ci).astype(F32)
    mask = jnp.where(diff >= 0, jnp.exp(jnp.maximum(diff, 0.0) * lg), 0.0)
    row = lax.broadcasted_iota(jnp.int32, (n, 1), 0).astype(F32)
    q_dec = jnp.exp(lg * (row + 1.0))
    k_dec = jnp.exp(lg * (n - 1.0 - row))

    qr = _rotary(q, cos, sin)
    kr = _rotary(k, cos, sin) * (RET_HEAD_DIM ** -0.5)
    vb = v.astype(BF16)
    scores = lax.dot_general(qr.astype(BF16), kr.astype(BF16), (((1,), (1,)), ((), ())),
                             preferred_element_type=F32) * mask
    o = _dot(scores.astype(BF16), vb) + _dot((qr * q_dec).astype(BF16), s.astype(BF16))
    s_new = sdec * s + lax.dot_general((kr * k_dec).astype(BF16), vb, (((0,), (0,)), ((), ())),
                                       preferred_element_type=F32)
    mu = jnp.mean(o, axis=-1, keepdims=True)
    oc = o - mu
    var = jnp.mean(oc * oc, axis=-1, keepdims=True)
    y = oc * lax.rsqrt(var + GN_EPS) * gn
    return y * jax.nn.silu(g), s_new


def _ret_seq_kernel(dec_ref, q_ref, k_ref, v_ref, g_ref, cos_ref, sin_ref, s0_ref, gn_ref,
                    o_ref, sf_ref, s_scr, *, chunk, nchunks):
    h = pl.program_id(1)
    lg = dec_ref[0, h]
    sdec = dec_ref[1, h]
    s_scr[...] = s0_ref[...]
    gn = gn_ref[...]

    def body(c, carry):
        rows = pl.ds(pl.multiple_of(c * chunk, chunk), chunk)
        y, s_new = _retention_chunk(lg, sdec, s_scr[...], q_ref[rows, :], k_ref[rows, :],
                                    v_ref[rows, :], g_ref[rows, :], cos_ref[rows, :],
                                    sin_ref[rows, :], gn)
        o_ref[rows, :] = y.astype(o_ref.dtype)
        s_scr[...] = s_new
        return carry

    lax.fori_loop(0, nchunks, body, 0, unroll=min(8, nchunks))
    sf_ref[...] = s_scr[...]


def _ret_decay_consts(chunk):
    lg = np.log(1.0 - 2.0 ** (-5.0 - np.arange(RET_HEADS, dtype=np.float64)))
    return jnp.asarray(np.stack([lg, np.exp(lg * chunk)]), dtype=F32)


def _rope_tables(pos):
    half = RET_HEAD_DIM // 2
    inv = ROPE_BASE ** (-np.arange(half, dtype=np.float64) / half)
    ang = np.asarray(pos, dtype=np.float64)[:, None] * inv[None, :]
    return jnp.asarray(np.cos(ang), dtype=F32), jnp.asarray(np.sin(ang), dtype=F32)


def _retention_seq(proj, row_block0, nbatch, seq, chunk, pos0, s0, gn_g):
    cos, sin = _rope_tables(pos0 + np.arange(seq))
    hd = RET_HEAD_DIM

    def col(off):
        return pl.BlockSpec((seq, hd), lambda b, h: (row_block0 + b, off + h))

    tbl = pl.BlockSpec((seq, hd // 2), lambda b, h: (0, 0))
    return pl.pallas_call(
        functools.partial(_ret_seq_kernel, chunk=chunk, nchunks=seq // chunk),
        grid=(nbatch, RET_HEADS),
        in_specs=[pl.BlockSpec(memory_space=pltpu.SMEM),
                  col(0), col(RET_HEADS), col(2 * RET_HEADS), col(3 * RET_HEADS), tbl, tbl,
                  pl.BlockSpec((None, None, hd, hd), lambda b, h: (0, h, 0, 0)),
                  pl.BlockSpec((1, hd), lambda b, h: (0, h))],
        out_specs=[pl.BlockSpec((seq, hd), lambda b, h: (b, h)),
                   pl.BlockSpec((None, None, hd, hd), lambda b, h: (b, h, 0, 0))],
        out_shape=[jax.ShapeDtypeStruct((nbatch * seq, RET_WIDTH), BF16),
                   jax.ShapeDtypeStruct((nbatch, RET_HEADS, hd, hd), F32)],
        scratch_shapes=[pltpu.VMEM((hd, hd), F32)],
        compiler_params=_params("parallel", "arbitrary"),
        name="retention_seq",
    )(_ret_decay_consts(chunk), proj, proj, proj, proj, cos, sin, s0, gn_g.reshape(1, RET_WIDTH))


def _ret_step_kernel(dec_ref, q_ref, k_ref, v_ref, g_ref, cos_ref, sin_ref, s_ref, gn_ref,
                     o_ref, sn_ref, *, nb, seq):
    hd = RET_HEAD_DIM
    cos, sin = cos_ref[...], sin_ref[...]
    for b in range(nb):
        rows = slice(b * seq, (b + 1) * seq)
        for h in range(RET_HEADS):
            cols = slice(h * hd, (h + 1) * hd)
            y, s_new = _retention_chunk(dec_ref[0, h], dec_ref[1, h], s_ref[b, h],
                                        q_ref[rows, cols], k_ref[rows, cols], v_ref[rows, cols],
                                        g_ref[rows, cols], cos, sin, gn_ref[:, cols])
            o_ref[rows, cols] = y.astype(o_ref.dtype)
            sn_ref[b, h] = s_new


def _retention_step(proj, row0, nbatch, seq, pos0, state, gn_g, nb):
    cos, sin = _rope_tables(pos0 + np.arange(seq))
    hd = RET_HEAD_DIM
    rb0 = row0 // (nb * seq)

    def col(off):
        return pl.BlockSpec((nb * seq, RET_WIDTH), lambda i: (rb0 + i, off))

    tbl = pl.BlockSpec((seq, hd // 2), lambda i: (0, 0))
    st = pl.BlockSpec((nb, RET_HEADS, hd, hd), lambda i: (i, 0, 0, 0))
    return pl.pallas_call(
        functools.partial(_ret_step_kernel, nb=nb, seq=seq),
        grid=(nbatch // nb,),
        in_specs=[pl.BlockSpec(memory_space=pltpu.SMEM),
                  col(0), col(1), col(2), col(3), tbl, tbl, st,
                  pl.BlockSpec((1, RET_WIDTH), lambda i: (0, 0))],
        out_specs=[pl.BlockSpec((nb * seq, RET_WIDTH), lambda i: (i, 0)), st],
        out_shape=[jax.ShapeDtypeStruct((nbatch * seq, RET_WIDTH), BF16),
                   jax.ShapeDtypeStruct(state.shape, F32)],
        compiler_params=_params("parallel"),
        name="retention_step",
    )(_ret_decay_consts(seq), proj, proj, proj, proj, cos, sin, state,
      gn_g.reshape(1, RET_WIDTH))


def _s5_disc_kernel(lr_ref, li_ref, ldt_ref, br_ref, bi_ref, ar_ref, ai_ref, bbr_ref, bbi_ref):
    lr, li = lr_ref[...], li_ref[...]
    dt = jnp.exp(ldt_ref[...])
    mag = jnp.exp(lr * dt)
    ar = mag * jnp.cos(li * dt)
    ai = mag * jnp.sin(li * dt)
    nr, ni = ar - 1.0, ai
    den = lr * lr + li * li
    fr = (nr * lr + ni * li) / den
    fi = (ni * lr - nr * li) / den
    br, bi = br_ref[...], bi_ref[...]
    ar_ref[...] = ar
    ai_ref[...] = ai
    bbr_ref[...] = fr * br - fi * bi
    bbi_ref[...] = fr * bi + fi * br


def _s5_discretize(lam_re, lam_im, log_dt, b_re, b_im):
    g, p, c = b_re.shape
    shp3 = jax.ShapeDtypeStruct((g, 1, p), F32)
    shpb = jax.ShapeDtypeStruct((g, c, p), F32)
    return pl.pallas_call(
        _s5_disc_kernel,
        out_shape=[shp3, shp3, shpb, shpb],
        name="s5_discretize",
    )(lam_re.reshape(g, 1, p), lam_im.reshape(g, 1, p), log_dt.reshape(g, 1, 1),
      jnp.swapaxes(b_re, 1, 2), jnp.swapaxes(b_im, 1, 2))


S5_NPAIR = S5_SLABS // 4


def _lanes(s):
    return slice(s * LANES, (s + 1) * LANES)


def _s5_place(s, nb, pitch):
    comp, ls = s // (2 * S5_NPAIR), s % (2 * S5_NPAIR)
    hf, p = ls // S5_NPAIR, ls % S5_NPAIR
    return comp * S5_NPAIR + p, hf * (nb * pitch)


def _s5_paired(ref, p, nb, order=None):
    def slab(hf):
        cols = _lanes(p + hf * S5_NPAIR)
        if order is None or ref.shape[0] == 1:
            return jnp.broadcast_to(ref[:, cols], (nb, LANES))
        return jnp.concatenate([ref[o:o + 1, cols] for o in order], axis=0)
    return jnp.concatenate([slab(0), slab(1)], axis=0)


def _s5_block_weights(bbr_ref, bbi_ref, cr_ref, ci_ref, bw_scr, cw_scr):
    bw_scr[...] = jnp.zeros_like(bw_scr)
    cw_scr[...] = jnp.zeros_like(cw_scr)
    for g in range(S5_GB):
        ch = slice(g * S5_GROUP, (g + 1) * S5_GROUP)
        for comp, (b_ref, c_ref, sign) in enumerate(((bbr_ref, cr_ref, 1.0), (bbi_ref, ci_ref, -1.0))):
            st = slice(comp * S5_ST + g * S5_STATE, comp * S5_ST + (g + 1) * S5_STATE)
            bw_scr[ch, st] = b_ref[g]
            cw_scr[st, ch] = sign * c_ref[g]


def _s5_step(buf, t, xs, ar, ai, nb, pitch):
    rows = pl.ds(t, 2 * nb, stride=pitch)
    new = [None] * (2 * S5_NPAIR)
    for p in range(S5_NPAIR):
        xr, xi = xs[p], xs[S5_NPAIR + p]
        nr = ar[p] * xr - ai[p] * xi + buf[p, rows, :]
        ni = ar[p] * xi + ai[p] * xr + buf[S5_NPAIR + p, rows, :]
        buf[p, rows, :] = nr
        buf[S5_NPAIR + p, rows, :] = ni
        new[p], new[S5_NPAIR + p] = nr, ni
    return tuple(new)


def _s5_scan_io(x0r_ref, x0i_ref, ar_ref, ai_ref, nb, order=None):
    ar = [_s5_paired(ar_ref, p, nb) for p in range(S5_NPAIR)]
    ai = [_s5_paired(ai_ref, p, nb) for p in range(S5_NPAIR)]
    xs0 = tuple([_s5_paired(x0r_ref, p, nb, order) for p in range(S5_NPAIR)]
                + [_s5_paired(x0i_ref, p, nb, order) for p in range(S5_NPAIR)])
    return ar, ai, xs0


def _s5_store_final(xs, xfr_ref, xfi_ref, nb, order=None):
    for p in range(S5_NPAIR):
        for hf in range(2):
            cols = _lanes(p + hf * S5_NPAIR)
            for x, ref in ((xs[p], xfr_ref), (xs[S5_NPAIR + p], xfi_ref)):
                if order is None:
                    ref[:, cols] = x[hf * nb:(hf + 1) * nb]
                else:
                    for i, o in enumerate(order):
                        ref[o:o + 1, cols] = x[hf * nb + i:hf * nb + i + 1]


def _s5_kernel(u_ref, x0r_ref, x0i_ref, ar_ref, ai_ref, bbr_ref, bbi_ref, cr_ref, ci_ref, d_ref,
               z_ref, xfr_ref, xfi_ref, buf, bw_scr, cw_scr, *, nb, seq):
    _s5_block_weights(bbr_ref, bbi_ref, cr_ref, ci_ref, bw_scr, cw_scr)
    ar, ai, xs = _s5_scan_io(x0r_ref, x0i_ref, ar_ref, ai_ref, nb)
    u = u_ref[...]
    bu = _dot(u.astype(BF16), bw_scr[...].astype(BF16))
    for s in range(S5_SLABS):
        q, off = _s5_place(s, nb, seq)
        buf[q, off:off + nb * seq, :] = bu[:, _lanes(s)]
    xs = lax.fori_loop(0, seq, lambda t, xs: _s5_step(buf, t, xs, ar, ai, nb, seq), xs,
                       unroll=8 if nb <= 8 else 1)
    parts = []
    for s in range(S5_SLABS):
        q, off = _s5_place(s, nb, seq)
        parts.append(buf[q, off:off + nb * seq, :])
    y = _dot(jnp.concatenate(parts, axis=1).astype(BF16), cw_scr[...].astype(BF16)) + d_ref[...] * u
    z_ref[...] = jax.nn.gelu(y).astype(z_ref.dtype)
    _s5_store_final(xs, xfr_ref, xfi_ref, nb)


def _s5_chunked_kernel(u_ref, x0r_ref, x0i_ref, ar_ref, ai_ref, bbr_ref, bbi_ref, cr_ref, ci_ref,
                       d_ref, z_ref, xfr_ref, xfi_ref, buf, bw_scr, cw_scr, *, nb, seq, tc, pitch):
    lead_odd = pitch % 8
    assert nb % 2 == 0 and nb >= 4 and lead_odd == 4 and (nb * pitch) % 8 == 0
    inner = list(range(1, nb - 1))
    even, odd = [0, nb - 1] + inner[nb // 2:], inner[:nb // 2]
    order = [(odd if i % 2 else even)[i // 2] for i in range(nb)]

    _s5_block_weights(bbr_ref, bbi_ref, cr_ref, ci_ref, bw_scr, cw_scr)
    bw = bw_scr[...].astype(BF16)
    cw = cw_scr[...].astype(BF16)
    ar, ai, xs = _s5_scan_io(x0r_ref, x0i_ref, ar_ref, ai_ref, nb, order)
    dvec = d_ref[...]

    def chunk(c, xs):
        t0 = c * tc
        rows = [pl.ds(pl.multiple_of(order[i] * seq + t0, tc), tc) for i in range(nb)]
        lead = [lead_odd if i % 2 else 0 for i in range(nb)]

        def u_window(i):
            if not lead[i]:
                return u_ref[rows[i], :]
            wide = u_ref[pl.ds(pl.multiple_of(order[i] * seq + t0 - 8, 8), tc + 16), :]
            return wide[8 - lead[i]:8 + tc + lead[i]]

        for i in range(nb):
            bu = _dot(u_window(i).astype(BF16), bw)
            for s in range(S5_SLABS):
                q, off = _s5_place(s, nb, pitch)
                r0 = off + i * pitch - lead[i]
                buf[q, r0:r0 + tc + 2 * lead[i], :] = bu[:, _lanes(s)]
        xs = lax.fori_loop(0, tc, lambda t, xs: _s5_step(buf, t, xs, ar, ai, nb, pitch), xs,
                           unroll=8)
        for i in range(nb):
            parts = []
            for s in range(S5_SLABS):
                q, off = _s5_place(s, nb, pitch)
                r0 = off + i * pitch - lead[i]
                parts.append(buf[q, r0:r0 + tc + 2 * lead[i], :])
            y = _dot(jnp.concatenate(parts, axis=1).astype(BF16), cw)[lead[i]:lead[i] + tc]
            y = y + dvec * u_ref[rows[i], :]
            z_ref[rows[i], :] = jax.nn.gelu(y).astype(z_ref.dtype)
        return xs

    xs = lax.fori_loop(0, seq // tc, chunk, xs)
    _s5_store_final(xs, xfr_ref, xfi_ref, nb, order)


def _s5(proj, row_block0, nb, seq, tc, x0r, x0i, ar, ai, bbr, bbi, c_re, c_im, d):
    rows = nb * seq
    u_col0 = 4 * RET_WIDTH // S5_CH
    nj = S5_GROUPS // S5_GB
    st = pl.BlockSpec((nb, S5_ST), lambda j: (0, j))
    coef = pl.BlockSpec((1, S5_ST), lambda j: (0, j))
    bspec = pl.BlockSpec((S5_GB, S5_GROUP, S5_STATE), lambda j: (j, 0, 0))
    cspec = pl.BlockSpec((S5_GB, S5_STATE, S5_GROUP), lambda j: (j, 0, 0))
    weights = [pltpu.VMEM((S5_CH, 2 * S5_ST), F32), pltpu.VMEM((2 * S5_ST, S5_CH), F32)]
    if tc == seq:
        body = functools.partial(_s5_kernel, nb=nb, seq=seq)
        pitch = seq
    else:
        pitch = tc + 4
        body = functools.partial(_s5_chunked_kernel, nb=nb, seq=seq, tc=tc, pitch=pitch)
    scratch = [pltpu.VMEM((S5_SLABS // 2, 2 * nb * pitch, LANES), F32)] + weights
    return pl.pallas_call(
        body,
        grid=(nj,),
        in_specs=[pl.BlockSpec((rows, S5_CH), lambda j: (row_block0, u_col0 + j)),
                  st, st, coef, coef, bspec, bspec, cspec, cspec,
                  pl.BlockSpec((1, S5_CH), lambda j: (0, j))],
        out_specs=[pl.BlockSpec((rows, S5_CH), lambda j: (0, j)), st, st],
        out_shape=[jax.ShapeDtypeStruct((rows, S5_WIDTH), BF16),
                   jax.ShapeDtypeStruct((nb, S5_NSTATE), F32),
                   jax.ShapeDtypeStruct((nb, S5_NSTATE), F32)],
        scratch_shapes=scratch,
        compiler_params=_params("parallel"),
        name="s5_scan",
    )(proj, x0r, x0i, ar, ai, bbr, bbi, c_re, c_im, d.reshape(1, S5_WIDTH))


def _s5_post_kernel(zp_ref, zs_ref, w_ref, b_ref, g_ref, o_ref, z_scr, gate_scr, ssq_scr,
                    *, n_first, tn):
    i, j = pl.program_id(0), pl.program_id(1)
    nj = gate_scr.shape[0]

    @pl.when(j == 0)
    def _():
        z = jnp.where(i < n_first, zp_ref[...], zs_ref[...])
        for jj in range(nj):
            z_scr[jj] = z[:, jj * tn:(jj + 1) * tn]
        ssq_scr[...] = jnp.zeros_like(ssq_scr)

    z = jnp.concatenate([z_scr[jj] for jj in range(nj)], axis=1)
    t = _dot(z, w_ref[...].astype(BF16)) + b_ref[...]
    s = z_scr[j].astype(F32) * jax.nn.sigmoid(t)
    gate_scr[j] = s
    ssq_scr[...] += jnp.sum(s * s, axis=-1, keepdims=True)

    @pl.when(j == nj - 1)
    def _():
        r = lax.rsqrt(ssq_scr[...] / (tn * nj) + EPS)
        for jj in range(nj):
            cols = slice(jj * tn, (jj + 1) * tn)
            o_ref[:, cols] = (gate_scr[jj] * r * g_ref[:, cols]).astype(o_ref.dtype)


def _s5_post(zp, zs, w_glu, b_glu, norm_g, tm, tn):
    d = zp.shape[1]
    n_first = zp.shape[0] // tm
    nblk = n_first + zs.shape[0] // tm
    nj = d // tn
    return pl.pallas_call(
        functools.partial(_s5_post_kernel, n_first=n_first, tn=tn),
        grid=(nblk, nj),
        in_specs=[pl.BlockSpec((tm, d), lambda i, j: (jnp.minimum(i, n_first - 1), 0)),
                  pl.BlockSpec((tm, d), lambda i, j: (jnp.maximum(i - n_first, 0), 0)),
                  pl.BlockSpec((d, tn), lambda i, j: (0, j)),
                  pl.BlockSpec((1, tn), lambda i, j: (0, j)),
                  pl.BlockSpec((1, d), lambda i, j: (0, 0))],
        out_specs=pl.BlockSpec((tm, d), lambda i, j: (i, 0)),
        out_shape=jax.ShapeDtypeStruct((nblk * tm, d), BF16),
        scratch_shapes=[pltpu.VMEM((nj, tm, tn), BF16), pltpu.VMEM((nj, tm, tn), F32),
                        pltpu.VMEM((tm, 1), F32)],
        compiler_params=_params("parallel", "arbitrary"),
        name="s5_post",
    )(zp, zs, w_glu, b_glu.reshape(1, d), norm_g.reshape(1, d))


def kernel(x_prompt, x_sample, state_ret, state_s5_re, state_s5_im, meta_tokens, norm1_g, w_in, ret_gn_g, s5_lam_re, s5_lam_im, s5_log_dt, s5_b_re, s5_b_im, s5_c_re, s5_c_im, s5_d, w_glu, b_glu, s5_norm_g, w_out, norm2_g, w_gate, w_up, w_down, final_norm_g):
    assert norm1_g.shape[0] == 1, "single-layer model"
    batch, seq, d = x_prompt.shape
    dbatch, dseq, _ = x_sample.shape
    rows_p, rows_s = batch * seq, dbatch * dseq
    tm = 1024
    assert rows_p % tm == 0 and rows_s % tm == 0 and seq % RET_CHUNK == 0
    xp = x_prompt.reshape(rows_p, d)
    xs = x_sample.reshape(rows_s, d)
    hd = RET_HEAD_DIM

    ar, ai, bbr, bbi = _s5_discretize(s5_lam_re[0], s5_lam_im[0], s5_log_dt[0], s5_b_re[0], s5_b_im[0])
    ar = ar.reshape(1, S5_NSTATE)
    ai = ai.reshape(1, S5_NSTATE)
    s5_w = (ar, ai, bbr, bbi, jnp.swapaxes(s5_c_re[0], 1, 2), jnp.swapaxes(s5_c_im[0], 1, 2), s5_d[0])

    h_meta = _rmsnorm([meta_tokens], norm1_g[0], N_META, BF16)
    h = _rmsnorm([xp, xs], norm1_g[0], 512, BF16)
    proj, proj_meta = _inproj(h, h_meta, w_in[0], 1536, 512)

    zero_ret = jnp.zeros((1, RET_HEADS, hd, hd), F32)
    _, s_meta = _retention_seq(proj_meta, 0, 1, N_META, N_META, 0, zero_ret, ret_gn_g[0])
    zero_s5 = jnp.zeros((1, S5_NSTATE), F32)
    _, mr, mi = _s5(proj_meta, 0, 1, N_META, N_META, zero_s5, zero_s5, *s5_w)

    ret_p, sret_p = _retention_seq(proj, 0, batch, seq, RET_CHUNK, N_META, s_meta, ret_gn_g[0])
    ret_s, sret_s = _retention_step(proj, rows_p, dbatch, dseq, PAST_LEN, state_ret[0],
                                    ret_gn_g[0], 4)

    z_p, s5r_p, s5i_p = _s5(proj, 0, batch, seq, 256,
                            jnp.broadcast_to(mr, (batch, S5_NSTATE)),
                            jnp.broadcast_to(mi, (batch, S5_NSTATE)), *s5_w)
    z_s, s5r_s, s5i_s = _s5(proj, rows_p // rows_s, dbatch, dseq, dseq,
                            state_s5_re[0].reshape(dbatch, S5_NSTATE),
                            state_s5_im[0].reshape(dbatch, S5_NSTATE), *s5_w)
    s5_out = _s5_post(z_p, z_s, w_glu[0], b_glu[0], s5_norm_g[0], tm, 512)

    x1, x1g, x1_ssq = _outproj(ret_p, ret_s, s5_out, w_out[0], xp, xs, norm2_g[0], tm, 512)
    a = _ffn_up(x1g, x1_ssq, w_gate[0], w_up[0], tm, 256)
    y_p, y_s = _ffn_down_norm(a, w_down[0], x1, final_norm_g, rows_p,
                              (w_down.shape[1] // 512 + 1) * 256, tm, 256)

    st = (1, -1, S5_GROUPS, S5_STATE)
    return (y_p.reshape(batch, seq, d), y_s.reshape(dbatch, dseq, d),
            sret_p[None], s5r_p.reshape(st), s5i_p.reshape(st),
            sret_s[None], s5r_s.reshape(st), s5i_s.reshape(st))
```

```python
import functools

import numpy as np
import jax
import jax.numpy as jnp
from jax import lax
from jax.experimental import pallas as pl
from jax.experimental.pallas import tpu as pltpu

D_MODEL = 4096
N_META = 16
PAST_LEN = 16384
RET_WIDTH = D_MODEL // 2
S5_WIDTH = D_MODEL - RET_WIDTH
RET_HEADS = 8
RET_HEAD_DIM = RET_WIDTH // RET_HEADS
RET_CHUNK = 128
ROPE_BASE = 10000.0
S5_GROUP = 16
S5_GROUPS = S5_WIDTH // S5_GROUP
S5_STATE = 64
S5_NSTATE = S5_GROUPS * S5_STATE
IN_COLS = 4 * RET_WIDTH + S5_WIDTH
EPS = 1e-6
GN_EPS = 1e-5

LANES = 128
S5_GB = 16
S5_CH = S5_GB * S5_GROUP
S5_ST = S5_GB * S5_STATE
S5_SLABS = 2 * S5_ST // LANES
VMEM_LIMIT = 56 * 1024 * 1024

BF16 = jnp.bfloat16
F32 = jnp.float32


def _params(*sem):
    return pltpu.CompilerParams(dimension_semantics=sem, vmem_limit_bytes=VMEM_LIMIT)


def _dot(a, b):
    return jnp.dot(a, b, preferred_element_type=F32)


def _rmsnorm_kernel(*refs, n_src, n_first):
    x_refs, g_ref, o_ref = refs[:n_src], refs[n_src], refs[n_src + 1]
    x = x_refs[0][...]
    if n_src == 2:
        x = jnp.where(pl.program_id(0) < n_first, x, x_refs[1][...])
    r = lax.rsqrt(jnp.mean(x * x, axis=-1, keepdims=True) + EPS)
    o_ref[...] = (x * r * g_ref[...]).astype(o_ref.dtype)


def _rmsnorm(srcs, g, tm, out_dtype):
    d = srcs[0].shape[1]
    nblk = [s.shape[0] // tm for s in srcs]
    n_first = nblk[0]
    specs = [pl.BlockSpec((tm, d), lambda i: (jnp.minimum(i, n_first - 1), 0))]
    if len(srcs) == 2:
        specs.append(pl.BlockSpec((tm, d), lambda i: (jnp.maximum(i - n_first, 0), 0)))
    rows = sum(s.shape[0] for s in srcs)
    return pl.pallas_call(
        functools.partial(_rmsnorm_kernel, n_src=len(srcs), n_first=n_first),
        grid=(sum(nblk),),
        in_specs=specs + [pl.BlockSpec((1, d), lambda i: (0, 0))],
        out_specs=pl.BlockSpec((tm, d), lambda i: (i, 0)),
        out_shape=jax.ShapeDtypeStruct((rows, d), out_dtype),
        compiler_params=_params("parallel"),
        name="rmsnorm",
    )(*srcs, g.reshape(1, d))


def _final_norm_kernel(x_ref, g_ref, yp_ref, ys_ref, *, n_first):
    x = x_ref[...]
    r = lax.rsqrt(jnp.mean(x * x, axis=-1, keepdims=True) + EPS)
    y = x * r * g_ref[...]
    i = pl.program_id(0)

    @pl.when(i < n_first)
    def _():
        yp_ref[...] = y

    @pl.when(i >= n_first)
    def _():
        ys_ref[...] = y


def _final_norm(x, g, rows_p, tm):
    rows, d = x.shape
    n_first = rows_p // tm
    return pl.pallas_call(
        functools.partial(_final_norm_kernel, n_first=n_first),
        grid=(rows // tm,),
        in_specs=[pl.BlockSpec((tm, d), lambda i: (i, 0)),
                  pl.BlockSpec((1, d), lambda i: (0, 0))],
        out_specs=[pl.BlockSpec((tm, d), lambda i: (jnp.minimum(i, n_first - 1), 0)),
                   pl.BlockSpec((tm, d), lambda i: (jnp.maximum(i - n_first, 0), 0))],
        out_shape=[jax.ShapeDtypeStruct((rows_p, d), F32),
                   jax.ShapeDtypeStruct((rows - rows_p, d), F32)],
        compiler_params=_params("arbitrary"),
        name="final_norm",
    )(x, g.reshape(1, d))


def _inproj_kernel(h_ref, hx_ref, w_ref, o_ref, ox_ref):
    w = w_ref[...].astype(BF16)
    o_ref[...] = _dot(h_ref[...], w)

    @pl.when(pl.program_id(0) == 0)
    def _():
        ox_ref[...] = _dot(hx_ref[...], w)


def _inproj(h, h_extra, w, tm, tn):
    m, k = h.shape
    mx = h_extra.shape[0]
    n = w.shape[1]
    nj = n // tn
    return pl.pallas_call(
        _inproj_kernel,
        grid=(m // tm, nj),
        in_specs=[pl.BlockSpec((tm, k), lambda i, j: (i, 0)),
                  pl.BlockSpec((mx, k), lambda i, j: (0, 0)),
                  pl.BlockSpec((k, tn), lambda i, j: (0, j))],
        out_specs=[pl.BlockSpec((tm, tn), lambda i, j: (i, j)),
                   pl.BlockSpec((mx, tn), lambda i, j: (0, jnp.where(i == 0, j, nj - 1)))],
        out_shape=[jax.ShapeDtypeStruct((m, n), F32), jax.ShapeDtypeStruct((mx, n), F32)],
        compiler_params=_params("arbitrary", "arbitrary"),
        name="inproj",
    )(h, h_extra, w)


def _outproj_kernel(retp_hbm, rets_hbm, s5_hbm, w_ref, xp_ref, xs_ref, g_ref,
                    o_ref, xg_ref, ssq_ref, mix_buf, mix_sem, *, n_first, tm):
    i, j = pl.program_id(0), pl.program_id(1)
    kh = mix_buf.shape[2] // 2
    slot = i % 2

    def ret_copy(block, s, from_prompt):
        src = retp_hbm if from_prompt else rets_hbm
        row0 = (block if from_prompt else block - n_first) * tm
        return pltpu.make_async_copy(src.at[pl.ds(pl.multiple_of(row0, tm), tm)],
                                     mix_buf.at[s, :, pl.ds(0, kh)], mix_sem.at[s, 0])

    def s5_copy(block, s):
        return pltpu.make_async_copy(s5_hbm.at[pl.ds(pl.multiple_of(block * tm, tm), tm)],
                                     mix_buf.at[s, :, pl.ds(kh, kh)], mix_sem.at[s, 1])

    def each_copy(block, s, act):
        @pl.when(block < n_first)
        def _():
            act(ret_copy(block, s, True))

        @pl.when(block >= n_first)
        def _():
            act(ret_copy(block, s, False))

        act(s5_copy(block, s))

    @pl.when((i == 0) & (j == 0))
    def _():
        each_copy(0, 0, lambda c: c.start())

    @pl.when(j == 0)
    def _():
        each_copy(i, slot, lambda c: c.wait())

        @pl.when(i + 1 < pl.num_programs(0))
        def _():
            each_copy(i + 1, 1 - slot, lambda c: c.start())

    x = jnp.where(i < n_first, xp_ref[...], xs_ref[...])
    x1 = x + _dot(mix_buf[slot], w_ref[...].astype(BF16))
    o_ref[...] = x1
    xg_ref[...] = (x1 * g_ref[...]).astype(xg_ref.dtype)
    sq = x1 * x1
    part = sq[:, :LANES]
    for c in range(1, sq.shape[1] // LANES):
        part += sq[:, c * LANES:(c + 1) * LANES]

    @pl.when(j == 0)
    def _():
        ssq_ref[...] = part

    @pl.when(j != 0)
    def _():
        ssq_ref[...] += part


def _outproj(ret_p, ret_s, s5, w_out, xp, xs, g, tm, tn):
    m, kh = s5.shape
    k, n = w_out.shape
    assert ret_p.shape[1] == kh and k == 2 * kh
    n_first = xp.shape[0] // tm
    nj = n // tn
    hbm = pl.BlockSpec(memory_space=pl.ANY)
    return pl.pallas_call(
        functools.partial(_outproj_kernel, n_first=n_first, tm=tm),
        grid=(m // tm, nj),
        in_specs=[hbm, hbm, hbm,
                  pl.BlockSpec((k, tn), lambda i, j: (0, j)),
                  pl.BlockSpec((tm, tn), lambda i, j: (jnp.minimum(i, n_first - 1),
                                                       jnp.where(i < n_first, j, nj - 1))),
                  pl.BlockSpec((tm, tn), lambda i, j: (jnp.maximum(i - n_first, 0),
                                                       jnp.where(i < n_first, 0, j)),
                               pipeline_mode=pl.Buffered(1)),
                  pl.BlockSpec((1, tn), lambda i, j: (0, j))],
        out_specs=[pl.BlockSpec((tm, tn), lambda i, j: (i, j)),
                   pl.BlockSpec((tm, tn), lambda i, j: (i, j)),
                   pl.BlockSpec((tm, LANES), lambda i, j: (i, 0))],
        out_shape=[jax.ShapeDtypeStruct((m, n), F32),
                   jax.ShapeDtypeStruct((m, n), BF16),
                   jax.ShapeDtypeStruct((m, LANES), F32)],
        scratch_shapes=[pltpu.VMEM((2, tm, k), s5.dtype), pltpu.SemaphoreType.DMA((2, 2))],
        compiler_params=_params("arbitrary", "arbitrary"),
        name="outproj",
    )(ret_p, ret_s, s5, w_out, xp, xs, g.reshape(1, n))


def _ffn_up_kernel(xg_ref, ssq_ref, wg_ref, wu_ref, o_ref, r_scr, rh_scr):
    tf = wg_ref.shape[1]

    @pl.when(pl.program_id(1) == 0)
    def _():
        r = lax.rsqrt(jnp.sum(ssq_ref[...], axis=-1, keepdims=True) / xg_ref.shape[1] + EPS)
        r_scr[...] = r
        rh_scr[...] = 0.5 * r

    w = jnp.concatenate([wg_ref[...].astype(BF16), wu_ref[...].astype(BF16)], axis=1)
    gu = _dot(xg_ref[...], w)
    gh = gu[:, :tf] * rh_scr[...]
    o_ref[...] = (gh * (1.0 + jnp.tanh(gh)) * (gu[:, tf:] * r_scr[...])).astype(o_ref.dtype)


def _ffn_up(xg, ssq, w_gate, w_up, tm, tf):
    m, k = xg.shape
    f = w_gate.shape[1]
    return pl.pallas_call(
        _ffn_up_kernel,
        grid=(m // tm, f // tf),
        in_specs=[pl.BlockSpec((tm, k), lambda i, j: (i, 0)),
                  pl.BlockSpec((tm, LANES), lambda i, j: (i, 0)),
                  pl.BlockSpec((k, tf), lambda i, j: (0, j)),
                  pl.BlockSpec((k, tf), lambda i, j: (0, j))],
        out_specs=pl.BlockSpec((tm, tf), lambda i, j: (i, j)),
        out_shape=jax.ShapeDtypeStruct((m, f), BF16),
        scratch_shapes=[pltpu.VMEM((tm, 1), F32), pltpu.VMEM((tm, 1), F32)],
        compiler_params=_params("parallel", "arbitrary"),
        name="ffn_up",
    )(xg, ssq, w_gate, w_up)


def _ffn_down_kernel(a_hbm, w_ref, x_ref, o_ref, a_lo, a_hi, a_sem, acc, *, tm):
    i, h, j = pl.program_id(0), pl.program_id(1), pl.program_id(2)
    k_lo, k_hi = a_lo.shape[1], a_hi.shape[1]

    def a_copy(block, half):
        rows = pl.ds(pl.multiple_of(block * tm, tm), tm)
        cols, buf = (pl.ds(k_lo, k_hi), a_hi) if half else (pl.ds(0, k_lo), a_lo)
        return pltpu.make_async_copy(a_hbm.at[rows, cols], buf, a_sem.at[half])

    @pl.when((i == 0) & (h == 0) & (j == 0))
    def _():
        a_copy(0, 0).start()

    @pl.when((h == 0) & (j == 0))
    def _():
        a_copy(i, 0).wait()
        a_copy(i, 1).start()

    @pl.when((h == 1) & (j == 0))
    def _():
        a_copy(i, 1).wait()

        @pl.when(i + 1 < pl.num_programs(0))
        def _():
            a_copy(i + 1, 0).start()

    @pl.when(h == 0)
    def _():
        acc[j] = x_ref[...] + _dot(a_lo[...], w_ref[...].astype(BF16))

    @pl.when(h == 1)
    def _():
        o_ref[...] = acc[j] + _dot(a_hi[...], w_ref[k_lo - k_hi:, :].astype(BF16))


def _ffn_down(a, w_down, x, k_lo, tm, tn):
    m, f = a.shape
    n = w_down.shape[1]
    nj = n // tn
    assert 2 * k_lo >= f
    return pl.pallas_call(
        functools.partial(_ffn_down_kernel, tm=tm),
        grid=(m // tm, 2, nj),
        in_specs=[pl.BlockSpec(memory_space=pl.ANY),
                  pl.BlockSpec((pl.Element(k_lo), pl.Element(tn)),
                               lambda i, h, j: (h * (f - k_lo), j * tn)),
                  pl.BlockSpec((tm, tn), lambda i, h, j: (i, jnp.where(h == 0, j, nj - 1)))],
        out_specs=pl.BlockSpec((tm, tn), lambda i, h, j: (i, jnp.where(h == 0, 0, j))),
        out_shape=jax.ShapeDtypeStruct((m, n), F32),
        scratch_shapes=[pltpu.VMEM((tm, k_lo), a.dtype), pltpu.VMEM((tm, f - k_lo), a.dtype),
                        pltpu.SemaphoreType.DMA((2,)), pltpu.VMEM((nj, tm, tn), F32)],
        compiler_params=_params("arbitrary", "arbitrary", "arbitrary"),
        name="ffn_down",
    )(a, w_down, x)


def _rotary(x, cos, sin):
    half = RET_HEAD_DIM // 2
    x1, x2 = x[:, :half], x[:, half:]
    return jnp.concatenate([x1 * cos - x2 * sin, x1 * sin + x2 * cos], axis=-1)


def _retention_chunk(lg, sdec, s, q, k, v, g, cos, sin, gn):
    n = q.shape[0]
    ri = lax.broadcasted_iota(jnp.int32, (n, n), 0)
    ci = lax.broadcasted_iota(jnp.int32, (n, n), 1)
    diff = (ri - ci).astype(F32)
    mask = jnp.where(diff >= 0, jnp.exp(jnp.maximum(diff, 0.0) * lg), 0.0)
    row = lax.broadcasted_iota(jnp.int32, (n, 1), 0).astype(F32)
    q_dec = jnp.exp(lg * (row + 1.0))
    k_dec = jnp.exp(lg * (n - 1.0 - row))

    qr = _rotary(q, cos, sin)
    kr = _rotary(k, cos, sin) * (RET_HEAD_DIM ** -0.5)
    vb = v.astype(BF16)
    scores = lax.dot_general(qr.astype(BF16), kr.astype(BF16), (((1,), (1,)), ((), ())),
                             preferred_element_type=F32) * mask
    o = _dot(scores.astype(BF16), vb) + _dot((qr * q_dec).astype(BF16), s.astype(BF16))
    s_new = sdec * s + lax.dot_general((kr * k_dec).astype(BF16), vb, (((0,), (0,)), ((), ())),
                                       preferred_element_type=F32)
    mu = jnp.mean(o, axis=-1, keepdims=True)
    oc = o - mu
    var = jnp.mean(oc * oc, axis=-1, keepdims=True)
    y = oc * lax.rsqrt(var + GN_EPS) * gn
    return y * jax.nn.silu(g), s_new


def _ret_seq_kernel(dec_ref, q_ref, k_ref, v_ref, g_ref, cos_ref, sin_ref, s0_ref, gn_ref,
                    o_ref, sf_ref, s_scr, *, chunk, nchunks):
    h = pl.program_id(1)
    lg = dec_ref[0, h]
    sdec = dec_ref[1, h]
    s_scr[...] = s0_ref[...]
    gn = gn_ref[...]

    def body(c, carry):
        rows = pl.ds(pl.multiple_of(c * chunk, chunk), chunk)
        y, s_new = _retention_chunk(lg, sdec, s_scr[...], q_ref[rows, :], k_ref[rows, :],
                                    v_ref[rows, :], g_ref[rows, :], cos_ref[rows, :],
                                    sin_ref[rows, :], gn)
        o_ref[rows, :] = y.astype(o_ref.dtype)
        s_scr[...] = s_new
        return carry

    lax.fori_loop(0, nchunks, body, 0, unroll=min(16, nchunks))
    sf_ref[...] = s_scr[...]


def _ret_decay_consts(chunk):
    lg = np.log(1.0 - 2.0 ** (-5.0 - np.arange(RET_HEADS, dtype=np.float64)))
    return jnp.asarray(np.stack([lg, np.exp(lg * chunk)]), dtype=F32)


def _rope_tables(pos):
    half = RET_HEAD_DIM // 2
    inv = ROPE_BASE ** (-np.arange(half, dtype=np.float64) / half)
    ang = np.asarray(pos, dtype=np.float64)[:, None] * inv[None, :]
    return jnp.asarray(np.cos(ang), dtype=F32), jnp.asarray(np.sin(ang), dtype=F32)


def _retention_seq(proj, row_block0, nbatch, seq, chunk, pos0, s0, gn_g):
    cos, sin = _rope_tables(pos0 + np.arange(seq))
    hd = RET_HEAD_DIM

    def col(off):
        return pl.BlockSpec((seq, hd), lambda b, h: (row_block0 + b, off + h))

    tbl = pl.BlockSpec((seq, hd // 2), lambda b, h: (0, 0))
    return pl.pallas_call(
        functools.partial(_ret_seq_kernel, chunk=chunk, nchunks=seq // chunk),
        grid=(nbatch, RET_HEADS),
        in_specs=[pl.BlockSpec(memory_space=pltpu.SMEM),
                  col(0), col(RET_HEADS), col(2 * RET_HEADS), col(3 * RET_HEADS), tbl, tbl,
                  pl.BlockSpec((None, None, hd, hd), lambda b, h: (0, h, 0, 0)),
                  pl.BlockSpec((1, hd), lambda b, h: (0, h))],
        out_specs=[pl.BlockSpec((seq, hd), lambda b, h: (b, h)),
                   pl.BlockSpec((None, None, hd, hd), lambda b, h: (b, h, 0, 0))],
        out_shape=[jax.ShapeDtypeStruct((nbatch * seq, RET_WIDTH), BF16),
                   jax.ShapeDtypeStruct((nbatch, RET_HEADS, hd, hd), F32)],
        scratch_shapes=[pltpu.VMEM((hd, hd), F32)],
        compiler_params=_params("parallel", "arbitrary"),
        name="retention_seq",
    )(_ret_decay_consts(chunk), proj, proj, proj, proj, cos, sin, s0, gn_g.reshape(1, RET_WIDTH))


def _ret_step_kernel(dec_ref, q_ref, k_ref, v_ref, g_ref, cos_ref, sin_ref, s_ref, gn_ref,
                     o_ref, sn_ref, *, nb, seq):
    hd = RET_HEAD_DIM
    cos, sin = cos_ref[...], sin_ref[...]
    for b in range(nb):
        rows = slice(b * seq, (b + 1) * seq)
        for h in range(RET_HEADS):
            cols = slice(h * hd, (h + 1) * hd)
            y, s_new = _retention_chunk(dec_ref[0, h], dec_ref[1, h], s_ref[b, h],
                                        q_ref[rows, cols], k_ref[rows, cols], v_ref[rows, cols],
                                        g_ref[rows, cols], cos, sin, gn_ref[:, cols])
            o_ref[rows, cols] = y.astype(o_ref.dtype)
            sn_ref[b, h] = s_new


def _retention_step(proj, row0, nbatch, seq, pos0, state, gn_g, nb):
    cos, sin = _rope_tables(pos0 + np.arange(seq))
    hd = RET_HEAD_DIM
    rb0 = row0 // (nb * seq)

    def col(off):
        return pl.BlockSpec((nb * seq, RET_WIDTH), lambda i: (rb0 + i, off))

    tbl = pl.BlockSpec((seq, hd // 2), lambda i: (0, 0))
    st = pl.BlockSpec((nb, RET_HEADS, hd, hd), lambda i: (i, 0, 0, 0))
    return pl.pallas_call(
        functools.partial(_ret_step_kernel, nb=nb, seq=seq),
        grid=(nbatch // nb,),
        in_specs=[pl.BlockSpec(memory_space=pltpu.SMEM),
                  col(0), col(1), col(2), col(3), tbl, tbl, st,
                  pl.BlockSpec((1, RET_WIDTH), lambda i: (0, 0))],
        out_specs=[pl.BlockSpec((nb * seq, RET_WIDTH), lambda i: (i, 0)), st],
        out_shape=[jax.ShapeDtypeStruct((nbatch * seq, RET_WIDTH), BF16),
                   jax.ShapeDtypeStruct(state.shape, F32)],
        compiler_params=_params("parallel"),
        name="retention_step",
    )(_ret_decay_consts(seq), proj, proj, proj, proj, cos, sin, state,
      gn_g.reshape(1, RET_WIDTH))


def _s5_disc_kernel(lr_ref, li_ref, ldt_ref, br_ref, bi_ref, ar_ref, ai_ref, bbr_ref, bbi_ref):
    lr, li = lr_ref[...], li_ref[...]
    dt = jnp.exp(ldt_ref[...])
    mag = jnp.exp(lr * dt)
    ar = mag * jnp.cos(li * dt)
    ai = mag * jnp.sin(li * dt)
    nr, ni = ar - 1.0, ai
    den = lr * lr + li * li
    fr = (nr * lr + ni * li) / den
    fi = (ni * lr - nr * li) / den
    br, bi = br_ref[...], bi_ref[...]
    ar_ref[...] = ar
    ai_ref[...] = ai
    bbr_ref[...] = fr * br - fi * bi
    bbi_ref[...] = fr * bi + fi * br


def _s5_discretize(lam_re, lam_im, log_dt, b_re, b_im):
    g, p, c = b_re.shape
    shp3 = jax.ShapeDtypeStruct((g, 1, p), F32)
    shpb = jax.ShapeDtypeStruct((g, c, p), F32)
    return pl.pallas_call(
        _s5_disc_kernel,
        out_shape=[shp3, shp3, shpb, shpb],
        name="s5_discretize",
    )(lam_re.reshape(g, 1, p), lam_im.reshape(g, 1, p), log_dt.reshape(g, 1, 1),
      jnp.swapaxes(b_re, 1, 2), jnp.swapaxes(b_im, 1, 2))


S5_NPAIR = S5_SLABS // 4


def _lanes(s):
    return slice(s * LANES, (s + 1) * LANES)


def _s5_place(s, nb, pitch):
    comp, ls = s // (2 * S5_NPAIR), s % (2 * S5_NPAIR)
    hf, p = ls // S5_NPAIR, ls % S5_NPAIR
    return comp * S5_NPAIR + p, hf * (nb * pitch)


def _s5_paired(ref, p, nb, order=None):
    def slab(hf):
        cols = _lanes(p + hf * S5_NPAIR)
        if order is None or ref.shape[0] == 1:
            return jnp.broadcast_to(ref[:, cols], (nb, LANES))
        return jnp.concatenate([ref[o:o + 1, cols] for o in order], axis=0)
    return jnp.concatenate([slab(0), slab(1)], axis=0)


def _s5_block_weights(bbr_ref, bbi_ref, cr_ref, ci_ref, bw_scr, cw_scr):
    bw_scr[...] = jnp.zeros_like(bw_scr)
    cw_scr[...] = jnp.zeros_like(cw_scr)
    for g in range(S5_GB):
        ch = slice(g * S5_GROUP, (g + 1) * S5_GROUP)
        for comp, (b_ref, c_ref, sign) in enumerate(((bbr_ref, cr_ref, 1.0), (bbi_ref, ci_ref, -1.0))):
            st = slice(comp * S5_ST + g * S5_STATE, comp * S5_ST + (g + 1) * S5_STATE)
            bw_scr[ch, st] = b_ref[g]
            cw_scr[st, ch] = sign * c_ref[g]


def _s5_step(buf, t, xs, ar, ai, nb, pitch):
    rows = pl.ds(t, 2 * nb, stride=pitch)
    new = [None] * (2 * S5_NPAIR)
    for p in range(S5_NPAIR):
        xr, xi = xs[p], xs[S5_NPAIR + p]
        nr = ar[p] * xr - ai[p] * xi + buf[p, rows, :]
        ni = ar[p] * xi + ai[p] * xr + buf[S5_NPAIR + p, rows, :]
        buf[p, rows, :] = nr
        buf[S5_NPAIR + p, rows, :] = ni
        new[p], new[S5_NPAIR + p] = nr, ni
    return tuple(new)


def _s5_scan_io(x0r_ref, x0i_ref, ar_ref, ai_ref, nb, order=None):
    ar = [_s5_paired(ar_ref, p, nb) for p in range(S5_NPAIR)]
    ai = [_s5_paired(ai_ref, p, nb) for p in range(S5_NPAIR)]
    xs0 = tuple([_s5_paired(x0r_ref, p, nb, order) for p in range(S5_NPAIR)]
                + [_s5_paired(x0i_ref, p, nb, order) for p in range(S5_NPAIR)])
    return ar, ai, xs0


def _s5_store_final(xs, xfr_ref, xfi_ref, nb, order=None):
    for p in range(S5_NPAIR):
        for hf in range(2):
            cols = _lanes(p + hf * S5_NPAIR)
            for x, ref in ((xs[p], xfr_ref), (xs[S5_NPAIR + p], xfi_ref)):
                if order is None:
                    ref[:, cols] = x[hf * nb:(hf + 1) * nb]
                else:
                    for i, o in enumerate(order):
                        ref[o:o + 1, cols] = x[hf * nb + i:hf * nb + i + 1]


def _s5_kernel(u_ref, x0r_ref, x0i_ref, ar_ref, ai_ref, bbr_ref, bbi_ref, cr_ref, ci_ref, d_ref,
               z_ref, xfr_ref, xfi_ref, buf, bw_scr, cw_scr, *, nb, seq):
    _s5_block_weights(bbr_ref, bbi_ref, cr_ref, ci_ref, bw_scr, cw_scr)
    ar, ai, xs = _s5_scan_io(x0r_ref, x0i_ref, ar_ref, ai_ref, nb)
    u = u_ref[...]
    bu = _dot(u.astype(BF16), bw_scr[...].astype(BF16))
    for s in range(S5_SLABS):
        q, off = _s5_place(s, nb, seq)
        buf[q, off:off + nb * seq, :] = bu[:, _lanes(s)]
    xs = lax.fori_loop(0, seq, lambda t, xs: _s5_step(buf, t, xs, ar, ai, nb, seq), xs,
                       unroll=8 if nb <= 8 else 1)
    parts = []
    for s in range(S5_SLABS):
        q, off = _s5_place(s, nb, seq)
        parts.append(buf[q, off:off + nb * seq, :])
    y = _dot(jnp.concatenate(parts, axis=1).astype(BF16), cw_scr[...].astype(BF16)) + d_ref[...] * u
    z_ref[...] = jax.nn.gelu(y).astype(z_ref.dtype)
    _s5_store_final(xs, xfr_ref, xfi_ref, nb)


def _s5_chunked_kernel(u_ref, x0r_ref, x0i_ref, ar_ref, ai_ref, bbr_ref, bbi_ref, cr_ref, ci_ref,
                       d_ref, z_ref, xfr_ref, xfi_ref, buf, bw_scr, cw_scr, *, nb, seq, tc, pitch):
    lead_odd = pitch % 8
    assert nb % 2 == 0 and nb >= 4 and lead_odd == 4 and (nb * pitch) % 8 == 0
    inner = list(range(1, nb - 1))
    even, odd = [0, nb - 1] + inner[nb // 2:], inner[:nb // 2]
    order = [(odd if i % 2 else even)[i // 2] for i in range(nb)]

    _s5_block_weights(bbr_ref, bbi_ref, cr_ref, ci_ref, bw_scr, cw_scr)
    bw = bw_scr[...].astype(BF16)
    cw = cw_scr[...].astype(BF16)
    ar, ai, xs = _s5_scan_io(x0r_ref, x0i_ref, ar_ref, ai_ref, nb, order)
    dvec = d_ref[...]

    def chunk(c, xs):
        t0 = c * tc
        rows = [pl.ds(pl.multiple_of(order[i] * seq + t0, tc), tc) for i in range(nb)]
        lead = [lead_odd if i % 2 else 0 for i in range(nb)]

        def u_window(i):
            if not lead[i]:
                return u_ref[rows[i], :]
            wide = u_ref[pl.ds(pl.multiple_of(order[i] * seq + t0 - 8, 8), tc + 16), :]
            return wide[8 - lead[i]:8 + tc + lead[i]]

        for i in range(nb):
            bu = _dot(u_window(i).astype(BF16), bw)
            for s in range(S5_SLABS):
                q, off = _s5_place(s, nb, pitch)
                r0 = off + i * pitch - lead[i]
                buf[q, r0:r0 + tc + 2 * lead[i], :] = bu[:, _lanes(s)]
        xs = lax.fori_loop(0, tc, lambda t, xs: _s5_step(buf, t, xs, ar, ai, nb, pitch), xs,
                           unroll=8)
        for i in range(nb):
            parts = []
            for s in range(S5_SLABS):
                q, off = _s5_place(s, nb, pitch)
                r0 = off + i * pitch - lead[i]
                parts.append(buf[q, r0:r0 + tc + 2 * lead[i], :])
            y = _dot(jnp.concatenate(parts, axis=1).astype(BF16), cw)[lead[i]:lead[i] + tc]
            y = y + dvec * u_ref[rows[i], :]
            z_ref[rows[i], :] = jax.nn.gelu(y).astype(z_ref.dtype)
        return xs

    xs = lax.fori_loop(0, seq // tc, chunk, xs)
    _s5_store_final(xs, xfr_ref, xfi_ref, nb, order)


def _s5(proj, row_block0, nb, seq, tc, x0r, x0i, ar, ai, bbr, bbi, c_re, c_im, d):
    rows = nb * seq
    u_col0 = 4 * RET_WIDTH // S5_CH
    nj = S5_GROUPS // S5_GB
    st = pl.BlockSpec((nb, S5_ST), lambda j: (0, j))
    coef = pl.BlockSpec((1, S5_ST), lambda j: (0, j))
    bspec = pl.BlockSpec((S5_GB, S5_GROUP, S5_STATE), lambda j: (j, 0, 0))
    cspec = pl.BlockSpec((S5_GB, S5_STATE, S5_GROUP), lambda j: (j, 0, 0))
    weights = [pltpu.VMEM((S5_CH, 2 * S5_ST), F32), pltpu.VMEM((2 * S5_ST, S5_CH), F32)]
    if tc == seq:
        body = functools.partial(_s5_kernel, nb=nb, seq=seq)
        pitch = seq
    else:
        pitch = tc + 4
        body = functools.partial(_s5_chunked_kernel, nb=nb, seq=seq, tc=tc, pitch=pitch)
    scratch = [pltpu.VMEM((S5_SLABS // 2, 2 * nb * pitch, LANES), F32)] + weights
    return pl.pallas_call(
        body,
        grid=(nj,),
        in_specs=[pl.BlockSpec((rows, S5_CH), lambda j: (row_block0, u_col0 + j)),
                  st, st, coef, coef, bspec, bspec, cspec, cspec,
                  pl.BlockSpec((1, S5_CH), lambda j: (0, j))],
        out_specs=[pl.BlockSpec((rows, S5_CH), lambda j: (0, j)), st, st],
        out_shape=[jax.ShapeDtypeStruct((rows, S5_WIDTH), BF16),
                   jax.ShapeDtypeStruct((nb, S5_NSTATE), F32),
                   jax.ShapeDtypeStruct((nb, S5_NSTATE), F32)],
        scratch_shapes=scratch,
        compiler_params=_params("parallel"),
        name="s5_scan",
    )(proj, x0r, x0i, ar, ai, bbr, bbi, c_re, c_im, d.reshape(1, S5_WIDTH))


def _s5_post_kernel(zp_ref, zs_ref, w_ref, b_ref, g_ref, o_ref, z_scr, gate_scr, ssq_scr,
                    *, n_first, tn):
    i, j = pl.program_id(0), pl.program_id(1)
    nj = gate_scr.shape[0]

    @pl.when(j == 0)
    def _():
        z = jnp.where(i < n_first, zp_ref[...], zs_ref[...])
        for jj in range(nj):
            z_scr[jj] = z[:, jj * tn:(jj + 1) * tn]
        ssq_scr[...] = jnp.zeros_like(ssq_scr)

    z = jnp.concatenate([z_scr[jj] for jj in range(nj)], axis=1)
    t = _dot(z, w_ref[...].astype(BF16)) + b_ref[...]
    s = z_scr[j].astype(F32) * jax.nn.sigmoid(t)
    gate_scr[j] = s
    ssq_scr[...] += jnp.sum(s * s, axis=-1, keepdims=True)

    @pl.when(j == nj - 1)
    def _():
        r = lax.rsqrt(ssq_scr[...] / (tn * nj) + EPS)
        for jj in range(nj):
            cols = slice(jj * tn, (jj + 1) * tn)
            o_ref[:, cols] = (gate_scr[jj] * r * g_ref[:, cols]).astype(o_ref.dtype)


def _s5_post(zp, zs, w_glu, b_glu, norm_g, tm, tn):
    d = zp.shape[1]
    n_first = zp.shape[0] // tm
    nblk = n_first + zs.shape[0] // tm
    nj = d // tn
    return pl.pallas_call(
        functools.partial(_s5_post_kernel, n_first=n_first, tn=tn),
        grid=(nblk, nj),
        in_specs=[pl.BlockSpec((tm, d), lambda i, j: (jnp.minimum(i, n_first - 1), 0)),
                  pl.BlockSpec((tm, d), lambda i, j: (jnp.maximum(i - n_first, 0), 0)),
                  pl.BlockSpec((d, tn), lambda i, j: (0, j)),
                  pl.BlockSpec((1, tn), lambda i, j: (0, j)),
                  pl.BlockSpec((1, d), lambda i, j: (0, 0))],
        out_specs=pl.BlockSpec((tm, d), lambda i, j: (i, 0)),
        out_shape=jax.ShapeDtypeStruct((nblk * tm, d), BF16),
        scratch_shapes=[pltpu.VMEM((nj, tm, tn), BF16), pltpu.VMEM((nj, tm, tn), F32),
                        pltpu.VMEM((tm, 1), F32)],
        compiler_params=_params("parallel", "arbitrary"),
        name="s5_post",
    )(zp, zs, w_glu, b_glu.reshape(1, d), norm_g.reshape(1, d))


def kernel(x_prompt, x_sample, state_ret, state_s5_re, state_s5_im, meta_tokens, norm1_g, w_in, ret_gn_g, s5_lam_re, s5_lam_im, s5_log_dt, s5_b_re, s5_b_im, s5_c_re, s5_c_im, s5_d, w_glu, b_glu, s5_norm_g, w_out, norm2_g, w_gate, w_up, w_down, final_norm_g):
    assert norm1_g.shape[0] == 1, "single-layer model"
    batch, seq, d = x_prompt.shape
    dbatch, dseq, _ = x_sample.shape
    rows_p, rows_s = batch * seq, dbatch * dseq
    tm = 1024
    assert rows_p % tm == 0 and rows_s % tm == 0 and seq % RET_CHUNK == 0
    xp = x_prompt.reshape(rows_p, d)
    xs = x_sample.reshape(rows_s, d)
    hd = RET_HEAD_DIM

    ar, ai, bbr, bbi = _s5_discretize(s5_lam_re[0], s5_lam_im[0], s5_log_dt[0], s5_b_re[0], s5_b_im[0])
    ar = ar.reshape(1, S5_NSTATE)
    ai = ai.reshape(1, S5_NSTATE)
    s5_w = (ar, ai, bbr, bbi, jnp.swapaxes(s5_c_re[0], 1, 2), jnp.swapaxes(s5_c_im[0], 1, 2), s5_d[0])

    h_meta = _rmsnorm([meta_tokens], norm1_g[0], N_META, BF16)
    h = _rmsnorm([xp, xs], norm1_g[0], 512, BF16)
    proj, proj_meta = _inproj(h, h_meta, w_in[0], 1536, 512)

    zero_ret = jnp.zeros((1, RET_HEADS, hd, hd), F32)
    _, s_meta = _retention_seq(proj_meta, 0, 1, N_META, N_META, 0, zero_ret, ret_gn_g[0])
    zero_s5 = jnp.zeros((1, S5_NSTATE), F32)
    _, mr, mi = _s5(proj_meta, 0, 1, N_META, N_META, zero_s5, zero_s5, *s5_w)

    ret_p, sret_p = _retention_seq(proj, 0, batch, seq, RET_CHUNK, N_META, s_meta, ret_gn_g[0])
    ret_s, sret_s = _retention_step(proj, rows_p, dbatch, dseq, PAST_LEN, state_ret[0],
                                    ret_gn_g[0], 4)

    z_p, s5r_p, s5i_p = _s5(proj, 0, batch, seq, 512,
                            jnp.broadcast_to(mr, (batch, S5_NSTATE)),
                            jnp.broadcast_to(mi, (batch, S5_NSTATE)), *s5_w)
    z_s, s5r_s, s5i_s = _s5(proj, rows_p // rows_s, dbatch, dseq, dseq,
                            state_s5_re[0].reshape(dbatch, S5_NSTATE),
                            state_s5_im[0].reshape(dbatch, S5_NSTATE), *s5_w)
    s5_out = _s5_post(z_p, z_s, w_glu[0], b_glu[0], s5_norm_g[0], tm, 512)

    x1, x1g, x1_ssq = _outproj(ret_p, ret_s, s5_out, w_out[0], xp, xs, norm2_g[0], tm, 512)
    a = _ffn_up(x1g, x1_ssq, w_gate[0], w_up[0], tm, 256)
    x2 = _ffn_down(a, w_down[0], x1, (w_down.shape[1] // 512 + 1) * 256, tm, 256)
    y_p, y_s = _final_norm(x2, final_norm_g, rows_p, 512)

    st = (1, -1, S5_GROUPS, S5_STATE)
    return (y_p.reshape(batch, seq, d), y_s.reshape(dbatch, dseq, d),
            sret_p[None], s5r_p.reshape(st), s5i_p.reshape(st),
            sret_s[None], s5r_s.reshape(st), s5i_s.reshape(st))
```

```python
import functools

import numpy as np
import jax
import jax.numpy as jnp
from jax import lax
from jax.experimental import pallas as pl
from jax.experimental.pallas import tpu as pltpu

D_MODEL = 4096
N_META = 16
PAST_LEN = 16384
RET_WIDTH = D_MODEL // 2
S5_WIDTH = D_MODEL - RET_WIDTH
RET_HEADS = 8
RET_HEAD_DIM = RET_WIDTH // RET_HEADS
RET_CHUNK = 128
ROPE_BASE = 10000.0
S5_GROUP = 16
S5_GROUPS = S5_WIDTH // S5_GROUP
S5_STATE = 64
S5_NSTATE = S5_GROUPS * S5_STATE
IN_COLS = 4 * RET_WIDTH + S5_WIDTH
EPS = 1e-6
GN_EPS = 1e-5

LANES = 128
S5_GB = 16
S5_CH = S5_GB * S5_GROUP
S5_ST = S5_GB * S5_STATE
S5_SLABS = 2 * S5_ST // LANES
VMEM_LIMIT = 56 * 1024 * 1024

BF16 = jnp.bfloat16
F32 = jnp.float32


def _params(*sem):
    return pltpu.CompilerParams(dimension_semantics=sem, vmem_limit_bytes=VMEM_LIMIT)


def _dot(a, b):
    return jnp.dot(a, b, preferred_element_type=F32)


def _rmsnorm_kernel(*refs, n_src, n_first):
    x_refs, g_ref, o_ref = refs[:n_src], refs[n_src], refs[n_src + 1]
    x = x_refs[0][...]
    if n_src == 2:
        x = jnp.where(pl.program_id(0) < n_first, x, x_refs[1][...])
    r = lax.rsqrt(jnp.mean(x * x, axis=-1, keepdims=True) + EPS)
    o_ref[...] = (x * r * g_ref[...]).astype(o_ref.dtype)


def _rmsnorm(srcs, g, tm, out_dtype):
    d = srcs[0].shape[1]
    nblk = [s.shape[0] // tm for s in srcs]
    n_first = nblk[0]
    specs = [pl.BlockSpec((tm, d), lambda i: (jnp.minimum(i, n_first - 1), 0))]
    if len(srcs) == 2:
        specs.append(pl.BlockSpec((tm, d), lambda i: (jnp.maximum(i - n_first, 0), 0)))
    rows = sum(s.shape[0] for s in srcs)
    return pl.pallas_call(
        functools.partial(_rmsnorm_kernel, n_src=len(srcs), n_first=n_first),
        grid=(sum(nblk),),
        in_specs=specs + [pl.BlockSpec((1, d), lambda i: (0, 0))],
        out_specs=pl.BlockSpec((tm, d), lambda i: (i, 0)),
        out_shape=jax.ShapeDtypeStruct((rows, d), out_dtype),
        compiler_params=_params("parallel"),
        name="rmsnorm",
    )(*srcs, g.reshape(1, d))


def _final_norm_kernel(x_ref, g_ref, yp_ref, ys_ref, *, n_first):
    x = x_ref[...]
    r = lax.rsqrt(jnp.mean(x * x, axis=-1, keepdims=True) + EPS)
    y = x * r * g_ref[...]
    i = pl.program_id(0)

    @pl.when(i < n_first)
    def _():
        yp_ref[...] = y

    @pl.when(i >= n_first)
    def _():
        ys_ref[...] = y


def _final_norm(x, g, rows_p, tm):
    rows, d = x.shape
    n_first = rows_p // tm
    return pl.pallas_call(
        functools.partial(_final_norm_kernel, n_first=n_first),
        grid=(rows // tm,),
        in_specs=[pl.BlockSpec((tm, d), lambda i: (i, 0)),
                  pl.BlockSpec((1, d), lambda i: (0, 0))],
        out_specs=[pl.BlockSpec((tm, d), lambda i: (jnp.minimum(i, n_first - 1), 0)),
                   pl.BlockSpec((tm, d), lambda i: (jnp.maximum(i - n_first, 0), 0))],
        out_shape=[jax.ShapeDtypeStruct((rows_p, d), F32),
                   jax.ShapeDtypeStruct((rows - rows_p, d), F32)],
        compiler_params=_params("arbitrary"),
        name="final_norm",
    )(x, g.reshape(1, d))


def _inproj_kernel(h_ref, hx_ref, w_ref, o_ref, ox_ref):
    w = w_ref[...].astype(BF16)
    o_ref[...] = _dot(h_ref[...], w)

    @pl.when(pl.program_id(0) == 0)
    def _():
        ox_ref[...] = _dot(hx_ref[...], w)


def _inproj(h, h_extra, w, tm, tn):
    m, k = h.shape
    mx = h_extra.shape[0]
    n = w.shape[1]
    nj = n // tn
    return pl.pallas_call(
        _inproj_kernel,
        grid=(m // tm, nj),
        in_specs=[pl.BlockSpec((tm, k), lambda i, j: (i, 0)),
                  pl.BlockSpec((mx, k), lambda i, j: (0, 0)),
                  pl.BlockSpec((k, tn), lambda i, j: (0, j))],
        out_specs=[pl.BlockSpec((tm, tn), lambda i, j: (i, j)),
                   pl.BlockSpec((mx, tn), lambda i, j: (0, jnp.where(i == 0, j, nj - 1)))],
        out_shape=[jax.ShapeDtypeStruct((m, n), F32), jax.ShapeDtypeStruct((mx, n), F32)],
        compiler_params=_params("arbitrary", "arbitrary"),
        name="inproj",
    )(h, h_extra, w)


def _outproj_kernel(retp_hbm, rets_hbm, s5_hbm, w_ref, xp_ref, xs_ref, g_ref,
                    o_ref, xg_ref, ssq_ref, mix_buf, mix_sem, *, n_first, tm):
    i, j = pl.program_id(0), pl.program_id(1)
    kh = mix_buf.shape[2] // 2
    slot = i % 2

    def ret_copy(block, s, from_prompt):
        src = retp_hbm if from_prompt else rets_hbm
        row0 = (block if from_prompt else block - n_first) * tm
        return pltpu.make_async_copy(src.at[pl.ds(pl.multiple_of(row0, tm), tm)],
                                     mix_buf.at[s, :, pl.ds(0, kh)], mix_sem.at[s, 0])

    def s5_copy(block, s):
        return pltpu.make_async_copy(s5_hbm.at[pl.ds(pl.multiple_of(block * tm, tm), tm)],
                                     mix_buf.at[s, :, pl.ds(kh, kh)], mix_sem.at[s, 1])

    def each_copy(block, s, act):
        @pl.when(block < n_first)
        def _():
            act(ret_copy(block, s, True))

        @pl.when(block >= n_first)
        def _():
            act(ret_copy(block, s, False))

        act(s5_copy(block, s))

    @pl.when((i == 0) & (j == 0))
    def _():
        each_copy(0, 0, lambda c: c.start())

    @pl.when(j == 0)
    def _():
        each_copy(i, slot, lambda c: c.wait())

        @pl.when(i + 1 < pl.num_programs(0))
        def _():
            each_copy(i + 1, 1 - slot, lambda c: c.start())

    x = jnp.where(i < n_first, xp_ref[...], xs_ref[...])
    x1 = x + _dot(mix_buf[slot], w_ref[...].astype(BF16))
    o_ref[...] = x1
    xg_ref[...] = (x1 * g_ref[...]).astype(xg_ref.dtype)
    sq = x1 * x1
    part = sq[:, :LANES]
    for c in range(1, sq.shape[1] // LANES):
        part += sq[:, c * LANES:(c + 1) * LANES]

    @pl.when(j == 0)
    def _():
        ssq_ref[...] = part

    @pl.when(j != 0)
    def _():
        ssq_ref[...] += part


def _outproj(ret_p, ret_s, s5, w_out, xp, xs, g, tm, tn):
    m, kh = s5.shape
    k, n = w_out.shape
    assert ret_p.shape[1] == kh and k == 2 * kh
    n_first = xp.shape[0] // tm
    nj = n // tn
    hbm = pl.BlockSpec(memory_space=pl.ANY)
    return pl.pallas_call(
        functools.partial(_outproj_kernel, n_first=n_first, tm=tm),
        grid=(m // tm, nj),
        in_specs=[hbm, hbm, hbm,
                  pl.BlockSpec((k, tn), lambda i, j: (0, j)),
                  pl.BlockSpec((tm, tn), lambda i, j: (jnp.minimum(i, n_first - 1),
                                                       jnp.where(i < n_first, j, nj - 1))),
                  pl.BlockSpec((tm, tn), lambda i, j: (jnp.maximum(i - n_first, 0),
                                                       jnp.where(i < n_first, 0, j)),
                               pipeline_mode=pl.Buffered(1)),
                  pl.BlockSpec((1, tn), lambda i, j: (0, j))],
        out_specs=[pl.BlockSpec((tm, tn), lambda i, j: (i, j)),
                   pl.BlockSpec((tm, tn), lambda i, j: (i, j)),
                   pl.BlockSpec((tm, LANES), lambda i, j: (i, 0))],
        out_shape=[jax.ShapeDtypeStruct((m, n), F32),
                   jax.ShapeDtypeStruct((m, n), BF16),
                   jax.ShapeDtypeStruct((m, LANES), F32)],
        scratch_shapes=[pltpu.VMEM((2, tm, k), s5.dtype), pltpu.SemaphoreType.DMA((2, 2))],
        compiler_params=_params("arbitrary", "arbitrary"),
        name="outproj",
    )(ret_p, ret_s, s5, w_out, xp, xs, g.reshape(1, n))


def _ffn_up_kernel(xg_ref, ssq_ref, wg_ref, wu_ref, o_ref, r_scr, rh_scr):
    tf = wg_ref.shape[1]

    @pl.when(pl.program_id(1) == 0)
    def _():
        r = lax.rsqrt(jnp.sum(ssq_ref[...], axis=-1, keepdims=True) / xg_ref.shape[1] + EPS)
        r_scr[...] = r
        rh_scr[...] = 0.5 * r

    w = jnp.concatenate([wg_ref[...].astype(BF16), wu_ref[...].astype(BF16)], axis=1)
    gu = _dot(xg_ref[...], w)
    gh = gu[:, :tf] * rh_scr[...]
    o_ref[...] = (gh * (1.0 + jnp.tanh(gh)) * (gu[:, tf:] * r_scr[...])).astype(o_ref.dtype)


def _ffn_up(xg, ssq, w_gate, w_up, tm, tf):
    m, k = xg.shape
    f = w_gate.shape[1]
    return pl.pallas_call(
        _ffn_up_kernel,
        grid=(m // tm, f // tf),
        in_specs=[pl.BlockSpec((tm, k), lambda i, j: (i, 0)),
                  pl.BlockSpec((tm, LANES), lambda i, j: (i, 0)),
                  pl.BlockSpec((k, tf), lambda i, j: (0, j)),
                  pl.BlockSpec((k, tf), lambda i, j: (0, j))],
        out_specs=pl.BlockSpec((tm, tf), lambda i, j: (i, j)),
        out_shape=jax.ShapeDtypeStruct((m, f), BF16),
        scratch_shapes=[pltpu.VMEM((tm, 1), F32), pltpu.VMEM((tm, 1), F32)],
        compiler_params=_params("parallel", "arbitrary"),
        name="ffn_up",
    )(xg, ssq, w_gate, w_up)


def _ffn_down_kernel(a_hbm, w_ref, x_ref, o_ref, a_lo, a_hi, a_sem, acc, *, tm):
    i, h, j = pl.program_id(0), pl.program_id(1), pl.program_id(2)
    k_lo, k_hi = a_lo.shape[1], a_hi.shape[1]

    def a_copy(block, half):
        rows = pl.ds(pl.multiple_of(block * tm, tm), tm)
        cols, buf = (pl.ds(k_lo, k_hi), a_hi) if half else (pl.ds(0, k_lo), a_lo)
        return pltpu.make_async_copy(a_hbm.at[rows, cols], buf, a_sem.at[half])

    @pl.when((i == 0) & (h == 0) & (j == 0))
    def _():
        a_copy(0, 0).start()

    @pl.when((h == 0) & (j == 0))
    def _():
        a_copy(i, 0).wait()
        a_copy(i, 1).start()

    @pl.when((h == 1) & (j == 0))
    def _():
        a_copy(i, 1).wait()

        @pl.when(i + 1 < pl.num_programs(0))
        def _():
            a_copy(i + 1, 0).start()

    @pl.when(h == 0)
    def _():
        acc[j] = x_ref[...] + _dot(a_lo[...], w_ref[...].astype(BF16))

    @pl.when(h == 1)
    def _():
        o_ref[...] = acc[j] + _dot(a_hi[...], w_ref[k_lo - k_hi:, :].astype(BF16))


def _ffn_down(a, w_down, x, k_lo, tm, tn):
    m, f = a.shape
    n = w_down.shape[1]
    nj = n // tn
    assert 2 * k_lo >= f
    return pl.pallas_call(
        functools.partial(_ffn_down_kernel, tm=tm),
        grid=(m // tm, 2, nj),
        in_specs=[pl.BlockSpec(memory_space=pl.ANY),
                  pl.BlockSpec((pl.Element(k_lo), pl.Element(tn)),
                               lambda i, h, j: (h * (f - k_lo), j * tn)),
                  pl.BlockSpec((tm, tn), lambda i, h, j: (i, jnp.where(h == 0, j, nj - 1)))],
        out_specs=pl.BlockSpec((tm, tn), lambda i, h, j: (i, jnp.where(h == 0, 0, j))),
        out_shape=jax.ShapeDtypeStruct((m, n), F32),
        scratch_shapes=[pltpu.VMEM((tm, k_lo), a.dtype), pltpu.VMEM((tm, f - k_lo), a.dtype),
                        pltpu.SemaphoreType.DMA((2,)), pltpu.VMEM((nj, tm, tn), F32)],
        compiler_params=_params("arbitrary", "arbitrary", "arbitrary"),
        name="ffn_down",
    )(a, w_down, x)


def _rotary(x, cos, sin):
    half = RET_HEAD_DIM // 2
    x1, x2 = x[:, :half], x[:, half:]
    return jnp.concatenate([x1 * cos - x2 * sin, x1 * sin + x2 * cos], axis=-1)


def _retention_chunk(lg, sdec, s, q, k, v, g, cos, sin, gn):
    n = q.shape[0]
    ri = lax.broadcasted_iota(jnp.int32, (n, n), 0)
    ci = lax.broadcasted_iota(jnp.int32, (n, n), 1)
    diff = (ri - ci).astype(F32)
    mask = jnp.where(diff >= 0, jnp.exp(jnp.maximum(diff, 0.0) * lg), 0.0)
    row = lax.broadcasted_iota(jnp.int32, (n, 1), 0).astype(F32)
    q_dec = jnp.exp(lg * (row + 1.0))
    k_dec = jnp.exp(lg * (n - 1.0 - row))

    qr = _rotary(q, cos, sin)
    kr = _rotary(k, cos, sin) * (RET_HEAD_DIM ** -0.5)
    vb = v.astype(BF16)
    scores = lax.dot_general(qr.astype(BF16), kr.astype(BF16), (((1,), (1,)), ((), ())),
                             preferred_element_type=F32) * mask
    o = _dot(scores.astype(BF16), vb) + _dot((qr * q_dec).astype(BF16), s.astype(BF16))
    s_new = sdec * s + lax.dot_general((kr * k_dec).astype(BF16), vb, (((0,), (0,)), ((), ())),
                                       preferred_element_type=F32)
    mu = jnp.mean(o, axis=-1, keepdims=True)
    oc = o - mu
    var = jnp.mean(oc * oc, axis=-1, keepdims=True)
    y = oc * lax.rsqrt(var + GN_EPS) * gn
    return y * jax.nn.silu(g), s_new


def _ret_seq_kernel(dec_ref, q_ref, k_ref, v_ref, g_ref, cos_ref, sin_ref, s0_ref, gn_ref,
                    o_ref, sf_ref, s_scr, *, chunk, nchunks):
    h = pl.program_id(1)
    lg = dec_ref[0, h]
    sdec = dec_ref[1, h]
    s_scr[...] = s0_ref[...]
    gn = gn_ref[...]

    def body(c, carry):
        rows = pl.ds(pl.multiple_of(c * chunk, chunk), chunk)
        y, s_new = _retention_chunk(lg, sdec, s_scr[...], q_ref[rows, :], k_ref[rows, :],
                                    v_ref[rows, :], g_ref[rows, :], cos_ref[rows, :],
                                    sin_ref[rows, :], gn)
        o_ref[rows, :] = y.astype(o_ref.dtype)
        s_scr[...] = s_new
        return carry

    lax.fori_loop(0, nchunks, body, 0, unroll=min(16, nchunks))
    sf_ref[...] = s_scr[...]


def _ret_decay_consts(chunk):
    lg = np.log(1.0 - 2.0 ** (-5.0 - np.arange(RET_HEADS, dtype=np.float64)))
    return jnp.asarray(np.stack([lg, np.exp(lg * chunk)]), dtype=F32)


def _rope_tables(pos):
    half = RET_HEAD_DIM // 2
    inv = ROPE_BASE ** (-np.arange(half, dtype=np.float64) / half)
    ang = np.asarray(pos, dtype=np.float64)[:, None] * inv[None, :]
    return jnp.asarray(np.cos(ang), dtype=F32), jnp.asarray(np.sin(ang), dtype=F32)


def _retention_seq(proj, row_block0, nbatch, seq, chunk, pos0, s0, gn_g):
    cos, sin = _rope_tables(pos0 + np.arange(seq))
    hd = RET_HEAD_DIM

    def col(off):
        return pl.BlockSpec((seq, hd), lambda b, h: (row_block0 + b, off + h))

    tbl = pl.BlockSpec((seq, hd // 2), lambda b, h: (0, 0))
    return pl.pallas_call(
        functools.partial(_ret_seq_kernel, chunk=chunk, nchunks=seq // chunk),
        grid=(nbatch, RET_HEADS),
        in_specs=[pl.BlockSpec(memory_space=pltpu.SMEM),
                  col(0), col(RET_HEADS), col(2 * RET_HEADS), col(3 * RET_HEADS), tbl, tbl,
                  pl.BlockSpec((None, None, hd, hd), lambda b, h: (0, h, 0, 0)),
                  pl.BlockSpec((1, hd), lambda b, h: (0, h))],
        out_specs=[pl.BlockSpec((seq, hd), lambda b, h: (b, h)),
                   pl.BlockSpec((None, None, hd, hd), lambda b, h: (b, h, 0, 0))],
        out_shape=[jax.ShapeDtypeStruct((nbatch * seq, RET_WIDTH), BF16),
                   jax.ShapeDtypeStruct((nbatch, RET_HEADS, hd, hd), F32)],
        scratch_shapes=[pltpu.VMEM((hd, hd), F32)],
        compiler_params=_params("parallel", "arbitrary"),
        name="retention_seq",
    )(_ret_decay_consts(chunk), proj, proj, proj, proj, cos, sin, s0, gn_g.reshape(1, RET_WIDTH))


def _ret_step_kernel(dec_ref, q_ref, k_ref, v_ref, g_ref, cos_ref, sin_ref, s_ref, gn_ref,
                     o_ref, sn_ref, *, nb, seq):
    hd = RET_HEAD_DIM
    cos, sin = cos_ref[...], sin_ref[...]
    for b in range(nb):
        rows = slice(b * seq, (b + 1) * seq)
        for h in range(RET_HEADS):
            cols = slice(h * hd, (h + 1) * hd)
            y, s_new = _retention_chunk(dec_ref[0, h], dec_ref[1, h], s_ref[b, h],
                                        q_ref[rows, cols], k_ref[rows, cols], v_ref[rows, cols],
                                        g_ref[rows, cols], cos, sin, gn_ref[:, cols])
            o_ref[rows, cols] = y.astype(o_ref.dtype)
            sn_ref[b, h] = s_new


def _retention_step(proj, row0, nbatch, seq, pos0, state, gn_g, nb):
    cos, sin = _rope_tables(pos0 + np.arange(seq))
    hd = RET_HEAD_DIM
    rb0 = row0 // (nb * seq)

    def col(off):
        return pl.BlockSpec((nb * seq, RET_WIDTH), lambda i: (rb0 + i, off))

    tbl = pl.BlockSpec((seq, hd // 2), lambda i: (0, 0))
    st = pl.BlockSpec((nb, RET_HEADS, hd, hd), lambda i: (i, 0, 0, 0))
    return pl.pallas_call(
        functools.partial(_ret_step_kernel, nb=nb, seq=seq),
        grid=(nbatch // nb,),
        in_specs=[pl.BlockSpec(memory_space=pltpu.SMEM),
                  col(0), col(1), col(2), col(3), tbl, tbl, st,
                  pl.BlockSpec((1, RET_WIDTH), lambda i: (0, 0))],
        out_specs=[pl.BlockSpec((nb * seq, RET_WIDTH), lambda i: (i, 0)), st],
        out_shape=[jax.ShapeDtypeStruct((nbatch * seq, RET_WIDTH), BF16),
                   jax.ShapeDtypeStruct(state.shape, F32)],
        compiler_params=_params("parallel"),
        name="retention_step",
    )(_ret_decay_consts(seq), proj, proj, proj, proj, cos, sin, state,
      gn_g.reshape(1, RET_WIDTH))


def _s5_disc_kernel(lr_ref, li_ref, ldt_ref, br_ref, bi_ref, ar_ref, ai_ref, bbr_ref, bbi_ref):
    lr, li = lr_ref[...], li_ref[...]
    dt = jnp.exp(ldt_ref[...])
    mag = jnp.exp(lr * dt)
    ar = mag * jnp.cos(li * dt)
    ai = mag * jnp.sin(li * dt)
    nr, ni = ar - 1.0, ai
    den = lr * lr + li * li
    fr = (nr * lr + ni * li) / den
    fi = (ni * lr - nr * li) / den
    br, bi = br_ref[...], bi_ref[...]
    ar_ref[...] = ar
    ai_ref[...] = ai
    bbr_ref[...] = fr * br - fi * bi
    bbi_ref[...] = fr * bi + fi * br


def _s5_discretize(lam_re, lam_im, log_dt, b_re, b_im):
    g, p, c = b_re.shape
    shp3 = jax.ShapeDtypeStruct((g, 1, p), F32)
    shpb = jax.ShapeDtypeStruct((g, c, p), F32)
    return pl.pallas_call(
        _s5_disc_kernel,
        out_shape=[shp3, shp3, shpb, shpb],
        name="s5_discretize",
    )(lam_re.reshape(g, 1, p), lam_im.reshape(g, 1, p), log_dt.reshape(g, 1, 1),
      jnp.swapaxes(b_re, 1, 2), jnp.swapaxes(b_im, 1, 2))


S5_NPAIR = S5_SLABS // 4


def _lanes(s):
    return slice(s * LANES, (s + 1) * LANES)


def _s5_place(s, nb, pitch):
    comp, ls = s // (2 * S5_NPAIR), s % (2 * S5_NPAIR)
    hf, p = ls // S5_NPAIR, ls % S5_NPAIR
    return comp * S5_NPAIR + p, hf * (nb * pitch)


def _s5_paired(ref, p, nb, order=None):
    def slab(hf):
        cols = _lanes(p + hf * S5_NPAIR)
        if order is None or ref.shape[0] == 1:
            return jnp.broadcast_to(ref[:, cols], (nb, LANES))
        return jnp.concatenate([ref[o:o + 1, cols] for o in order], axis=0)
    return jnp.concatenate([slab(0), slab(1)], axis=0)


def _s5_block_weights(bbr_ref, bbi_ref, cr_ref, ci_ref, bw_scr, cw_scr):
    bw_scr[...] = jnp.zeros_like(bw_scr)
    cw_scr[...] = jnp.zeros_like(cw_scr)
    for g in range(S5_GB):
        ch = slice(g * S5_GROUP, (g + 1) * S5_GROUP)
        for comp, (b_ref, c_ref, sign) in enumerate(((bbr_ref, cr_ref, 1.0), (bbi_ref, ci_ref, -1.0))):
            st = slice(comp * S5_ST + g * S5_STATE, comp * S5_ST + (g + 1) * S5_STATE)
            bw_scr[ch, st] = b_ref[g]
            cw_scr[st, ch] = sign * c_ref[g]


def _s5_step(buf, t, xs, ar, ai, nb, pitch):
    rows = pl.ds(t, 2 * nb, stride=pitch)
    new = [None] * (2 * S5_NPAIR)
    for p in range(S5_NPAIR):
        xr, xi = xs[p], xs[S5_NPAIR + p]
        nr = ar[p] * xr - ai[p] * xi + buf[p, rows, :]
        ni = ar[p] * xi + ai[p] * xr + buf[S5_NPAIR + p, rows, :]
        buf[p, rows, :] = nr
        buf[S5_NPAIR + p, rows, :] = ni
        new[p], new[S5_NPAIR + p] = nr, ni
    return tuple(new)


def _s5_scan_io(x0r_ref, x0i_ref, ar_ref, ai_ref, nb, order=None):
    ar = [_s5_paired(ar_ref, p, nb) for p in range(S5_NPAIR)]
    ai = [_s5_paired(ai_ref, p, nb) for p in range(S5_NPAIR)]
    xs0 = tuple([_s5_paired(x0r_ref, p, nb, order) for p in range(S5_NPAIR)]
                + [_s5_paired(x0i_ref, p, nb, order) for p in range(S5_NPAIR)])
    return ar, ai, xs0


def _s5_store_final(xs, xfr_ref, xfi_ref, nb, order=None):
    for p in range(S5_NPAIR):
        for hf in range(2):
            cols = _lanes(p + hf * S5_NPAIR)
            for x, ref in ((xs[p], xfr_ref), (xs[S5_NPAIR + p], xfi_ref)):
                if order is None:
                    ref[:, cols] = x[hf * nb:(hf + 1) * nb]
                else:
                    for i, o in enumerate(order):
                        ref[o:o + 1, cols] = x[hf * nb + i:hf * nb + i + 1]


def _s5_kernel(u_ref, x0r_ref, x0i_ref, ar_ref, ai_ref, bbr_ref, bbi_ref, cr_ref, ci_ref, d_ref,
               z_ref, xfr_ref, xfi_ref, buf, bw_scr, cw_scr, *, nb, seq):
    _s5_block_weights(bbr_ref, bbi_ref, cr_ref, ci_ref, bw_scr, cw_scr)
    ar, ai, xs = _s5_scan_io(x0r_ref, x0i_ref, ar_ref, ai_ref, nb)
    bw = bw_scr[...].astype(BF16)
    cw = cw_scr[...].astype(BF16)
    rows = nb * seq
    piece = min(rows, 256)
    for r0 in range(0, rows, piece):
        bu = _dot(u_ref[r0:r0 + piece, :].astype(BF16), bw)
        for s in range(S5_SLABS):
            q, off = _s5_place(s, nb, seq)
            buf[q, off + r0:off + r0 + piece, :] = bu[:, _lanes(s)]
    xs = lax.fori_loop(0, seq, lambda t, xs: _s5_step(buf, t, xs, ar, ai, nb, seq), xs,
                       unroll=8 if nb <= 8 else 1)
    for r0 in range(0, rows, piece):
        parts = []
        for s in range(S5_SLABS):
            q, off = _s5_place(s, nb, seq)
            parts.append(buf[q, off + r0:off + r0 + piece, :])
        y = _dot(jnp.concatenate(parts, axis=1).astype(BF16), cw) + d_ref[...] * u_ref[r0:r0 + piece, :]
        z_ref[r0:r0 + piece, :] = jax.nn.gelu(y).astype(z_ref.dtype)
    _s5_store_final(xs, xfr_ref, xfi_ref, nb)


def _s5_chunked_kernel(u_ref, x0r_ref, x0i_ref, ar_ref, ai_ref, bbr_ref, bbi_ref, cr_ref, ci_ref,
                       d_ref, z_ref, xfr_ref, xfi_ref, buf, bw_scr, cw_scr, *, nb, seq, tc, pitch):
    lead_odd = pitch % 8
    assert nb % 2 == 0 and nb >= 4 and lead_odd == 4 and (nb * pitch) % 8 == 0
    inner = list(range(1, nb - 1))
    even, odd = [0, nb - 1] + inner[nb // 2:], inner[:nb // 2]
    order = [(odd if i % 2 else even)[i // 2] for i in range(nb)]

    _s5_block_weights(bbr_ref, bbi_ref, cr_ref, ci_ref, bw_scr, cw_scr)
    bw = bw_scr[...].astype(BF16)
    cw = cw_scr[...].astype(BF16)
    ar, ai, xs = _s5_scan_io(x0r_ref, x0i_ref, ar_ref, ai_ref, nb, order)
    dvec = d_ref[...]

    def chunk(c, xs):
        t0 = c * tc
        rows = [pl.ds(pl.multiple_of(order[i] * seq + t0, tc), tc) for i in range(nb)]
        lead = [lead_odd if i % 2 else 0 for i in range(nb)]

        def u_window(i):
            if not lead[i]:
                return u_ref[rows[i], :]
            wide = u_ref[pl.ds(pl.multiple_of(order[i] * seq + t0 - 8, 8), tc + 16), :]
            return wide[8 - lead[i]:8 + tc + lead[i]]

        for i in range(nb):
            bu = _dot(u_window(i).astype(BF16), bw)
            for s in range(S5_SLABS):
                q, off = _s5_place(s, nb, pitch)
                r0 = off + i * pitch - lead[i]
                buf[q, r0:r0 + tc + 2 * lead[i], :] = bu[:, _lanes(s)]
        xs = lax.fori_loop(0, tc, lambda t, xs: _s5_step(buf, t, xs, ar, ai, nb, pitch), xs,
                           unroll=8)
        for i in range(nb):
            parts = []
            for s in range(S5_SLABS):
                q, off = _s5_place(s, nb, pitch)
                r0 = off + i * pitch - lead[i]
                parts.append(buf[q, r0:r0 + tc + 2 * lead[i], :])
            y = _dot(jnp.concatenate(parts, axis=1).astype(BF16), cw)[lead[i]:lead[i] + tc]
            y = y + dvec * u_ref[rows[i], :]
            z_ref[rows[i], :] = jax.nn.gelu(y).astype(z_ref.dtype)
        return xs

    xs = lax.fori_loop(0, seq // tc, chunk, xs)
    _s5_store_final(xs, xfr_ref, xfi_ref, nb, order)


def _s5(proj, row_block0, nb, seq, tc, x0r, x0i, ar, ai, bbr, bbi, c_re, c_im, d):
    rows = nb * seq
    u_col0 = 4 * RET_WIDTH // S5_CH
    nj = S5_GROUPS // S5_GB
    st = pl.BlockSpec((nb, S5_ST), lambda j: (0, j))
    coef = pl.BlockSpec((1, S5_ST), lambda j: (0, j))
    bspec = pl.BlockSpec((S5_GB, S5_GROUP, S5_STATE), lambda j: (j, 0, 0))
    cspec = pl.BlockSpec((S5_GB, S5_STATE, S5_GROUP), lambda j: (j, 0, 0))
    weights = [pltpu.VMEM((S5_CH, 2 * S5_ST), F32), pltpu.VMEM((2 * S5_ST, S5_CH), F32)]
    if tc == seq:
        body = functools.partial(_s5_kernel, nb=nb, seq=seq)
        pitch = seq
    else:
        pitch = tc + 4
        body = functools.partial(_s5_chunked_kernel, nb=nb, seq=seq, tc=tc, pitch=pitch)
    scratch = [pltpu.VMEM((S5_SLABS // 2, 2 * nb * pitch, LANES), F32)] + weights
    return pl.pallas_call(
        body,
        grid=(nj,),
        in_specs=[pl.BlockSpec((rows, S5_CH), lambda j: (row_block0, u_col0 + j)),
                  st, st, coef, coef, bspec, bspec, cspec, cspec,
                  pl.BlockSpec((1, S5_CH), lambda j: (0, j))],
        out_specs=[pl.BlockSpec((rows, S5_CH), lambda j: (0, j)), st, st],
        out_shape=[jax.ShapeDtypeStruct((rows, S5_WIDTH), BF16),
                   jax.ShapeDtypeStruct((nb, S5_NSTATE), F32),
                   jax.ShapeDtypeStruct((nb, S5_NSTATE), F32)],
        scratch_shapes=scratch,
        compiler_params=_params("parallel"),
        name="s5_scan",
    )(proj, x0r, x0i, ar, ai, bbr, bbi, c_re, c_im, d.reshape(1, S5_WIDTH))


def _s5_post_kernel(zp_ref, zs_ref, w_ref, b_ref, g_ref, o_ref, z_scr, gate_scr, ssq_scr,
                    *, n_first, tn):
    i, j = pl.program_id(0), pl.program_id(1)
    nj = gate_scr.shape[0]

    @pl.when(j == 0)
    def _():
        z = jnp.where(i < n_first, zp_ref[...], zs_ref[...])
        for jj in range(nj):
            z_scr[jj] = z[:, jj * tn:(jj + 1) * tn]
        ssq_scr[...] = jnp.zeros_like(ssq_scr)

    z = jnp.concatenate([z_scr[jj] for jj in range(nj)], axis=1)
    t = _dot(z, w_ref[...].astype(BF16)) + b_ref[...]
    s = z_scr[j].astype(F32) * jax.nn.sigmoid(t)
    gate_scr[j] = s
    ssq_scr[...] += jnp.sum(s * s, axis=-1, keepdims=True)

    @pl.when(j == nj - 1)
    def _():
        r = lax.rsqrt(ssq_scr[...] / (tn * nj) + EPS)
        for jj in range(nj):
            cols = slice(jj * tn, (jj + 1) * tn)
            o_ref[:, cols] = (gate_scr[jj] * r * g_ref[:, cols]).astype(o_ref.dtype)


def _s5_post(zp, zs, w_glu, b_glu, norm_g, tm, tn):
    d = zp.shape[1]
    n_first = zp.shape[0] // tm
    nblk = n_first + zs.shape[0] // tm
    nj = d // tn
    return pl.pallas_call(
        functools.partial(_s5_post_kernel, n_first=n_first, tn=tn),
        grid=(nblk, nj),
        in_specs=[pl.BlockSpec((tm, d), lambda i, j: (jnp.minimum(i, n_first - 1), 0)),
                  pl.BlockSpec((tm, d), lambda i, j: (jnp.maximum(i - n_first, 0), 0)),
                  pl.BlockSpec((d, tn), lambda i, j: (0, j)),
                  pl.BlockSpec((1, tn), lambda i, j: (0, j)),
                  pl.BlockSpec((1, d), lambda i, j: (0, 0))],
        out_specs=pl.BlockSpec((tm, d), lambda i, j: (i, 0)),
        out_shape=jax.ShapeDtypeStruct((nblk * tm, d), BF16),
        scratch_shapes=[pltpu.VMEM((nj, tm, tn), BF16), pltpu.VMEM((nj, tm, tn), F32),
                        pltpu.VMEM((tm, 1), F32)],
        compiler_params=_params("parallel", "arbitrary"),
        name="s5_post",
    )(zp, zs, w_glu, b_glu.reshape(1, d), norm_g.reshape(1, d))


def kernel(x_prompt, x_sample, state_ret, state_s5_re, state_s5_im, meta_tokens, norm1_g, w_in, ret_gn_g, s5_lam_re, s5_lam_im, s5_log_dt, s5_b_re, s5_b_im, s5_c_re, s5_c_im, s5_d, w_glu, b_glu, s5_norm_g, w_out, norm2_g, w_gate, w_up, w_down, final_norm_g):
    assert norm1_g.shape[0] == 1, "single-layer model"
    batch, seq, d = x_prompt.shape
    dbatch, dseq, _ = x_sample.shape
    rows_p, rows_s = batch * seq, dbatch * dseq
    tm = 1024
    assert rows_p % tm == 0 and rows_s % tm == 0 and seq % RET_CHUNK == 0
    xp = x_prompt.reshape(rows_p, d)
    xs = x_sample.reshape(rows_s, d)
    hd = RET_HEAD_DIM

    ar, ai, bbr, bbi = _s5_discretize(s5_lam_re[0], s5_lam_im[0], s5_log_dt[0], s5_b_re[0], s5_b_im[0])
    ar = ar.reshape(1, S5_NSTATE)
    ai = ai.reshape(1, S5_NSTATE)
    s5_w = (ar, ai, bbr, bbi, jnp.swapaxes(s5_c_re[0], 1, 2), jnp.swapaxes(s5_c_im[0], 1, 2), s5_d[0])

    h_meta = _rmsnorm([meta_tokens], norm1_g[0], N_META, BF16)
    h = _rmsnorm([xp, xs], norm1_g[0], 512, BF16)
    proj, proj_meta = _inproj(h, h_meta, w_in[0], 1536, 512)

    zero_ret = jnp.zeros((1, RET_HEADS, hd, hd), F32)
    _, s_meta = _retention_seq(proj_meta, 0, 1, N_META, N_META, 0, zero_ret, ret_gn_g[0])
    zero_s5 = jnp.zeros((1, S5_NSTATE), F32)
    _, mr, mi = _s5(proj_meta, 0, 1, N_META, N_META, zero_s5, zero_s5, *s5_w)

    ret_p, sret_p = _retention_seq(proj, 0, batch, seq, RET_CHUNK, N_META, s_meta, ret_gn_g[0])
    ret_s, sret_s = _retention_step(proj, rows_p, dbatch, dseq, PAST_LEN, state_ret[0],
                                    ret_gn_g[0], 4)

    z_p, s5r_p, s5i_p = _s5(proj, 0, batch, seq, 512,
                            jnp.broadcast_to(mr, (batch, S5_NSTATE)),
                            jnp.broadcast_to(mi, (batch, S5_NSTATE)), *s5_w)
    z_s, s5r_s, s5i_s = _s5(proj, rows_p // rows_s, dbatch, dseq, dseq,
                            state_s5_re[0].reshape(dbatch, S5_NSTATE),
                            state_s5_im[0].reshape(dbatch, S5_NSTATE), *s5_w)
    s5_out = _s5_post(z_p, z_s, w_glu[0], b_glu[0], s5_norm_g[0], tm, 512)

    x1, x1g, x1_ssq = _outproj(ret_p, ret_s, s5_out, w_out[0], xp, xs, norm2_g[0], tm, 512)
    a = _ffn_up(x1g, x1_ssq, w_gate[0], w_up[0], tm, 256)
    x2 = _ffn_down(a, w_down[0], x1, (w_down.shape[1] // 512 + 1) * 256, tm, 256)
    y_p, y_s = _final_norm(x2, final_norm_g, rows_p, 512)

    st = (1, -1, S5_GROUPS, S5_STATE)
    return (y_p.reshape(batch, seq, d), y_s.reshape(dbatch, dseq, d),
            sret_p[None], s5r_p.reshape(st), s5i_p.reshape(st),
            sret_s[None], s5r_s.reshape(st), s5i_s.reshape(st))
```
